```python
import jax, jax.numpy as jnp
from jax import lax
import numpy as np

D_MODEL = 1024
BATCH = 32
SEQ = 2048
DEPTH = 4
DEC_BATCH = 8
DEC_SEQ = 16
PAST_LEN = 2048

CHUNK = 64
BAND_CHUNKS = 8
BAND_PAST = BAND_CHUNKS * CHUNK
BAND = BAND_PAST + CHUNK
D_ATT = D_MODEL // 2
N_HEADS_A = 8
HD_A = D_ATT // N_HEADS_A
MAX_REL = 128
D_MLSTM = D_MODEL - D_ATT
N_HEADS_M = 4
HD_M = D_MLSTM // N_HEADS_M
D_MIX = D_ATT + D_MLSTM
D_IN = 3 * D_ATT + 4 * D_MLSTM + 2 * N_HEADS_M
N_GROUPS = 4
EXP_PER_GROUP = 4
N_EXPERTS = N_GROUPS * EXP_PER_GROUP
TOP_K = 2
D_EXPERT = 256
EPS = 1e-6
NEG = -1e30

kernel_name = "hymba_streaming_band_attn_mlstm_hmoe"


def rmsnorm(x, g):
    xf = x.astype(jnp.float32)
    y = xf * lax.rsqrt(jnp.mean(xf * xf, axis=-1, keepdims=True) + EPS)
    return (y * g.astype(jnp.float32)).astype(x.dtype)


def ada_mod(c, w, b, n):
    return jnp.split(jax.nn.silu(c) @ w + b, n, axis=-1)


def modulate(h, shift, scale):
    return h * (1 + scale[:, None, :]) + shift[:, None, :]


def in_proj(h, w_in, b_i, b_f):
    B, S, _ = h.shape
    p = h @ w_in
    cuts = [int(v) for v in np.cumsum([D_ATT] * 3 + [D_MLSTM] * 4 + [N_HEADS_M])]
    qa, ka, va, qm, km, vm, om, ig, fg = jnp.split(p, cuts, axis=-1)
    heads = lambda t, n: t.reshape(B, S, n, -1).transpose(0, 2, 1, 3)
    ig = (ig + b_i).transpose(0, 2, 1)
    lf = jax.nn.log_sigmoid((fg + b_f).astype(jnp.float32)).transpose(0, 2, 1)
    return (heads(qa, N_HEADS_A), heads(ka, N_HEADS_A), heads(va, N_HEADS_A),
            heads(qm, N_HEADS_M), heads(km, N_HEADS_M) * (HD_M ** -0.5), heads(vm, N_HEADS_M),
            om, ig, lf)


def rel_bias_lookup(table, dist):
    return table[:, jnp.clip(dist, -MAX_REL, MAX_REL) + MAX_REL].astype(jnp.float32)


def attend(q, k, v, bias, valid):
    s = jnp.einsum('bhqd,bhkd->bhqk', q, k).astype(jnp.float32) * (HD_A ** -0.5) + bias
    s = jnp.where(valid, s, NEG)
    p = jax.nn.softmax(s, axis=-1)
    return jnp.einsum('bhqk,bhkd->bhqd', p.astype(v.dtype), v)


def band_attn_prompt(q, k, v, table):
    B, H, S, d = q.shape
    nc = S // CHUNK
    pad = ((0, 0), (0, 0), (BAND_PAST, 0), (0, 0))
    kp, vp = jnp.pad(k, pad), jnp.pad(v, pad)
    qc = jnp.moveaxis(q.reshape(B, H, nc, CHUNK, d), 2, 0)
    kidx = jnp.arange(BAND)
    bias = rel_bias_lookup(table, jnp.arange(CHUNK)[:, None] - kidx[None, :] + BAND_PAST)

    def one_chunk(args):
        c, qb = args
        kb = lax.dynamic_slice_in_dim(kp, c * CHUNK, BAND, axis=2)
        vb = lax.dynamic_slice_in_dim(vp, c * CHUNK, BAND, axis=2)
        valid = (c * CHUNK - BAND_PAST + kidx) >= 0
        return attend(qb, kb, vb, bias, valid)

    out = lax.map(one_chunk, (jnp.arange(nc), qc))
    return jnp.moveaxis(out, 0, 2).reshape(B, H, S, d)


def band_attn_sample(q, k, v, k_cache, v_cache, table):
    W, T = k_cache.shape[2], q.shape[2]
    kb = jnp.concatenate([k_cache, k], axis=2)
    vb = jnp.concatenate([v_cache, v], axis=2)
    dist = W + jnp.arange(T)[:, None] - jnp.arange(W + T)[None, :]
    return attend(q, kb, vb, rel_bias_lookup(table, dist), jnp.ones((W + T,), bool))


def mlstm_chunk(carry, xs):
    C, n, m = carry
    q, k, v, ig, lf = (a.astype(jnp.float32) for a in xs)
    L = q.shape[2]
    b = jnp.cumsum(lf, axis=-1)
    causal = jnp.tril(jnp.ones((L, L), bool))
    dmat = jnp.where(causal, b[..., :, None] - b[..., None, :] + ig[..., None, :], -jnp.inf)
    m_inter = b + m[..., None]
    m_t = jnp.maximum(m_inter, jnp.max(dmat, axis=-1))
    decay = jnp.exp(m_inter - m_t)
    s = jnp.exp(dmat - m_t[..., None]) * jnp.einsum('bhtd,bhsd->bhts', q, k)
    num = decay[..., None] * jnp.einsum('bhtd,bhde->bhte', q, C) + jnp.einsum('bhts,bhse->bhte', s, v)
    den = decay * jnp.einsum('bhtd,bhd->bht', q, n) + jnp.sum(s, axis=-1)
    h = num / jnp.maximum(jnp.abs(den), jnp.exp(-m_t))[..., None]
    b_last = b[..., -1]
    g = b_last[..., None] - b + ig
    m_new = jnp.maximum(b_last + m, jnp.max(g, axis=-1))
    wk = jnp.exp(g - m_new[..., None])[..., None] * k
    carry_decay = jnp.exp(b_last + m - m_new)
    C_new = carry_decay[..., None, None] * C + jnp.einsum('bhsd,bhse->bhde', wk, v)
    n_new = carry_decay[..., None] * n + jnp.sum(wk, axis=2)
    return (C_new, n_new, m_new), h


def mlstm_prompt(q, k, v, ig, lf):
    B, H, S, d = q.shape
    nc = S // CHUNK
    to_chunks = lambda a: jnp.moveaxis(a.reshape((B, H, nc, CHUNK) + a.shape[3:]), 2, 0)
    carry0 = (jnp.zeros((B, H, d, d), jnp.float32), jnp.zeros((B, H, d), jnp.float32),
              jnp.zeros((B, H), jnp.float32))
    (C, n, m), hs = lax.scan(mlstm_chunk, carry0,
                             (to_chunks(q), to_chunks(k), to_chunks(v), to_chunks(ig), to_chunks(lf)))
    h = jnp.moveaxis(hs, 0, 2).reshape(B, H, S, d)
    return h.astype(q.dtype), (C.astype(q.dtype), n.astype(q.dtype), m.astype(q.dtype))


def mlstm_sample(q, k, v, ig, lf, C, n, m):
    carry = (C.astype(jnp.float32), n.astype(jnp.float32), m.astype(jnp.float32))
    (C2, n2, m2), h = mlstm_chunk(carry, (q, k, v, ig, lf))
    return h.astype(q.dtype), (C2.astype(q.dtype), n2.astype(q.dtype), m2.astype(q.dtype))


def mlstm_out(hm, g, om):
    B, H, S, d = hm.shape
    hn = rmsnorm(hm, g.reshape(H, 1, d))
    return hn.transpose(0, 2, 1, 3).reshape(B, S, H * d) * jax.nn.sigmoid(om)


def merge_heads(a):
    B, H, S, d = a.shape
    return a.transpose(0, 2, 1, 3).reshape(B, S, H * d)


def hmoe(h, w_rg, b_rg, w_re, b_re, w_gate, w_up, w_down):
    B, S, D = h.shape
    t = h.reshape(B * S, D)
    pg = jax.nn.softmax((t @ w_rg + b_rg).astype(jnp.float32), axis=-1)
    p_top, g_idx = lax.top_k(pg, 1)
    le = (t @ w_re + b_re).astype(jnp.float32).reshape(-1, N_GROUPS, EXP_PER_GROUP)
    le = jnp.take_along_axis(le, g_idx[:, :, None], axis=1)[:, 0]
    w_top, e_idx = lax.top_k(jax.nn.softmax(le, axis=-1), TOP_K)
    w_top = w_top / jnp.sum(w_top, axis=-1, keepdims=True) * p_top
    gates = jnp.sum(jax.nn.one_hot(g_idx * EXP_PER_GROUP + e_idx, N_EXPERTS, dtype=jnp.float32)
                    * w_top[..., None], axis=1).astype(h.dtype)
    y = jnp.zeros_like(t)
    for e in range(N_EXPERTS):
        y = y + gates[:, e:e + 1] * ((jax.nn.silu(t @ w_gate[e]) * (t @ w_up[e])) @ w_down[e])
    return y.reshape(B, S, D)


def trunk(x, c, mix_fn, w_ada, b_ada, g_norm, w_in, b_igate, b_fgate, g_mlstm, w_out,
          w_router_g, b_router_g, w_router_e, b_router_e, w_gate, w_up, w_down,
          w_ada_f, b_ada_f, g_final):
    states = []
    for l in range(DEPTH):
        sh1, sc1, gt1, sh2, sc2, gt2 = ada_mod(c, w_ada[l], b_ada[l], 6)
        h = modulate(rmsnorm(x, g_norm[l, 0]), sh1, sc1)
        qa, ka, va, qm, km, vm, om, ig, lf = in_proj(h, w_in[l], b_igate[l], b_fgate[l])
        att, hm, st = mix_fn(l, qa, ka, va, qm, km, vm, ig, lf)
        mix = jnp.concatenate([merge_heads(att), mlstm_out(hm, g_mlstm[l], om)], axis=-1) @ w_out[l]
        x = x + gt1[:, None, :] * mix
        h = modulate(rmsnorm(x, g_norm[l, 1]), sh2, sc2)
        x = x + gt2[:, None, :] * hmoe(h, w_router_g[l], b_router_g[l], w_router_e[l], b_router_e[l],
                                        w_gate[l], w_up[l], w_down[l])
        states.append(st)
    shf, scf = ada_mod(c, w_ada_f, b_ada_f, 2)
    return modulate(rmsnorm(x, g_final), shf, scf), states


def setup_inputs(seed: int = 0) -> dict:
    key = jax.random.key(seed)
    ks = jax.random.split(key, 32)
    nrm = lambda k, shape, s: jax.random.normal(k, shape, jnp.float32) * s
    w_cache = min(BAND_PAST, PAST_LEN)
    return {
        "x_prompt": nrm(ks[0], (BATCH, SEQ, D_MODEL), 1.0),
        "x_sample": nrm(ks[1], (DEC_BATCH, DEC_SEQ, D_MODEL), 1.0),
        "c_prompt": nrm(ks[2], (BATCH, D_MODEL), 1.0),
        "c_sample": nrm(ks[3], (DEC_BATCH, D_MODEL), 1.0),
        "cache_attn_k": nrm(ks[4], (DEPTH, DEC_BATCH, N_HEADS_A, w_cache, HD_A), 1.0),
        "cache_attn_v": nrm(ks[5], (DEPTH, DEC_BATCH, N_HEADS_A, w_cache, HD_A), 1.0),
        "state_mlstm_C": nrm(ks[6], (DEPTH, DEC_BATCH, N_HEADS_M, HD_M, HD_M), 4.0 * HD_M ** -0.5),
        "state_mlstm_n": nrm(ks[7], (DEPTH, DEC_BATCH, N_HEADS_M, HD_M), 4.0 * HD_M ** -0.5),
        "state_mlstm_m": nrm(ks[8], (DEPTH, DEC_BATCH, N_HEADS_M), 1.0),
        "w_ada": nrm(ks[9], (DEPTH, D_MODEL, 6 * D_MODEL), 0.5 * D_MODEL ** -0.5),
        "b_ada": nrm(ks[10], (DEPTH, 6 * D_MODEL), 0.02),
        "g_norm": 1.0 + nrm(ks[11], (DEPTH, 2, D_MODEL), 0.02),
        "w_in": nrm(ks[12], (DEPTH, D_MODEL, D_IN), D_MODEL ** -0.5),
        "b_igate": nrm(ks[13], (DEPTH, N_HEADS_M), 0.1),
        "b_fgate": 3.0 + 3.0 * jax.random.uniform(ks[14], (DEPTH, N_HEADS_M), jnp.float32),
        "rel_bias": nrm(ks[15], (DEPTH, N_HEADS_A, 2 * MAX_REL + 1), 0.5),
        "g_mlstm": 1.0 + nrm(ks[16], (DEPTH, D_MLSTM), 0.02),
        "w_out": nrm(ks[17], (DEPTH, D_MIX, D_MODEL), D_MIX ** -0.5),
        "w_router_g": nrm(ks[18], (DEPTH, D_MODEL, N_GROUPS), D_MODEL ** -0.5),
        "b_router_g": nrm(ks[19], (DEPTH, N_GROUPS), 0.01),
        "w_router_e": nrm(ks[20], (DEPTH, D_MODEL, N_EXPERTS), D_MODEL ** -0.5),
        "b_router_e": nrm(ks[21], (DEPTH, N_EXPERTS), 0.01),
        "w_gate": nrm(ks[22], (DEPTH, N_EXPERTS, D_MODEL, D_EXPERT), D_MODEL ** -0.5),
        "w_up": nrm(ks[23], (DEPTH, N_EXPERTS, D_MODEL, D_EXPERT), D_MODEL ** -0.5),
        "w_down": nrm(ks[24], (DEPTH, N_EXPERTS, D_EXPERT, D_MODEL), D_EXPERT ** -0.5),
        "w_ada_f": nrm(ks[25], (D_MODEL, 2 * D_MODEL), 0.5 * D_MODEL ** -0.5),
        "b_ada_f": nrm(ks[26], (2 * D_MODEL,), 0.02),
        "g_final": 1.0 + nrm(ks[27], (D_MODEL,), 0.02),
    }


def reference(x_prompt, x_sample, c_prompt, c_sample, cache_attn_k, cache_attn_v,
              state_mlstm_C, state_mlstm_n, state_mlstm_m, w_ada, b_ada, g_norm, w_in,
              b_igate, b_fgate, rel_bias, g_mlstm, w_out, w_router_g, b_router_g,
              w_router_e, b_router_e, w_gate, w_up, w_down, w_ada_f, b_ada_f, g_final):
    w_keep = min(BAND_PAST, x_prompt.shape[1])

    def mix_prompt(l, qa, ka, va, qm, km, vm, ig, lf):
        att = band_attn_prompt(qa, ka, va, rel_bias[l])
        hm, (C, n, m) = mlstm_prompt(qm, km, vm, ig, lf)
        return att, hm, (ka[:, :, -w_keep:], va[:, :, -w_keep:], C, n, m)

    def mix_sample(l, qa, ka, va, qm, km, vm, ig, lf):
        att = band_attn_sample(qa, ka, va, cache_attn_k[l], cache_attn_v[l], rel_bias[l])
        hm, (C, n, m) = mlstm_sample(qm, km, vm, ig, lf,
                                     state_mlstm_C[l], state_mlstm_n[l], state_mlstm_m[l])
        return att, hm, (ka, va, C, n, m)

    weights = (w_ada, b_ada, g_norm, w_in, b_igate, b_fgate, g_mlstm, w_out, w_router_g, b_router_g,
               w_router_e, b_router_e, w_gate, w_up, w_down, w_ada_f, b_ada_f, g_final)
    y_prompt, sp = trunk(x_prompt, c_prompt, mix_prompt, *weights)
    y_sample, ss = trunk(x_sample, c_sample, mix_sample, *weights)

    k_prompt = jnp.stack([s[0] for s in sp])
    v_prompt = jnp.stack([s[1] for s in sp])
    C_prompt = jnp.stack([s[2] for s in sp])
    n_prompt = jnp.stack([s[3] for s in sp])
    m_prompt = jnp.stack([s[4] for s in sp])
    k_sample = jnp.stack([s[0] for s in ss])
    v_sample = jnp.stack([s[1] for s in ss])
    C_sample = jnp.stack([s[2] for s in ss])
    n_sample = jnp.stack([s[3] for s in ss])
    m_sample = jnp.stack([s[4] for s in ss])
    return (y_prompt, y_sample, k_prompt, v_prompt, C_prompt, n_prompt, m_prompt,
            k_sample, v_sample, C_sample, n_sample, m_sample)
```

```python
import functools

import jax
import jax.numpy as jnp
from jax import lax
from jax.experimental import pallas as pl
from jax.experimental.pallas import tpu as pltpu

F32 = jnp.float32
BF16 = jnp.bfloat16

D_MODEL = 1024
CHUNK = 64
BAND_CHUNKS = 8
BAND_PAST = BAND_CHUNKS * CHUNK
D_ATT = 512
N_HEADS_A = 8
HD_A = 64
MAX_REL = 128
D_MLSTM = 512
N_HEADS_M = 4
HD_M = 128
N_GROUPS = 4
EXP_PER_GROUP = 4
N_EXPERTS = 16
D_EXPERT = 256
EPS = 1e-6
NEG = -1e30

LANES = 128
N_SLABS = 7
SLAB = 512
Q_BLOCK = 256
K_WINDOW = BAND_PAST + Q_BLOCK
ROUTER_E0 = N_GROUPS
VMEM_LIMIT = 56 * 1024 * 1024


def _cparams(sem):
    return pltpu.CompilerParams(dimension_semantics=sem, vmem_limit_bytes=VMEM_LIMIT)


def _ada_kernel(c_ref, w_ref, b_ref, o_ref):
    c = c_ref[...]
    a = (c * jax.nn.sigmoid(c)).astype(BF16)
    o_ref[0] = jnp.dot(a, w_ref[0].astype(BF16), preferred_element_type=F32) + b_ref[0]


def _ada(c, w, b, tn):
    nl, d, n = w.shape
    bc = c.shape[0]
    return pl.pallas_call(
        _ada_kernel,
        out_shape=jax.ShapeDtypeStruct((nl, bc, n), F32),
        grid=(nl, n // tn),
        in_specs=[pl.BlockSpec((bc, d), lambda l, j: (0, 0)),
                  pl.BlockSpec((1, d, tn), lambda l, j: (l, 0, j)),
                  pl.BlockSpec((1, 1, tn), lambda l, j: (l, 0, j))],
        out_specs=pl.BlockSpec((1, bc, tn), lambda l, j: (l, 0, j)),
        compiler_params=_cparams(("parallel", "parallel")),
        name="ada",
    )(c, w, b.reshape(nl, 1, n))


def _norm_mod(x, g, scale, shift):
    r = lax.rsqrt(jnp.mean(x * x, axis=-1, keepdims=True) + EPS)
    return (x * r * g) * (1.0 + scale) + shift


def _inproj_kernel(x_ref, sh_ref, sc_ref, g_ref, w_ref, wg_ref, bg_ref,
                   qa_ref, ka_ref, va_ref, qm_ref, km_ref, vm_ref, om_ref, gc_ref, kt_ref, vt_ref,
                   *, first_tail_tile):
    hb = _norm_mod(x_ref[0], g_ref[...], sc_ref[...], sh_ref[...]).astype(BF16)
    outs = (qa_ref, ka_ref, va_ref, qm_ref, km_ref, vm_ref, om_ref)
    in_tail = pl.program_id(1) >= first_tail_tile
    for j, o_ref in enumerate(outs):
        p = jnp.dot(hb, w_ref[:, j * SLAB:(j + 1) * SLAB], preferred_element_type=F32)
        if j == 4:
            p = p * (HD_M ** -0.5)
        o_ref[0] = p.astype(BF16)
        if j in (1, 2):
            t_ref = kt_ref if j == 1 else vt_ref

            @pl.when(in_tail)
            def _():
                t_ref[0] = p
    pg = jnp.dot(hb, wg_ref[...], preferred_element_type=F32) + bg_ref[...]
    lsig = jnp.minimum(pg, 0.0) - jnp.log1p(jnp.exp(-jnp.abs(pg)))
    lane = lax.broadcasted_iota(jnp.int32, pg.shape, 1)
    gc = jnp.where(lane < N_HEADS_M, pg, lsig)
    gc_ref[0] = gc[:, :2 * N_HEADS_M]


def _in_proj(x, mods, g, w_main, w_gate, b_gate, ts, keep):
    B, S, D = x.shape
    n_tiles = S // ts
    n_tail = keep // ts
    first_tail = n_tiles - n_tail
    slab = jax.ShapeDtypeStruct((B, S, SLAB), BF16)
    tail = jax.ShapeDtypeStruct((B, keep, SLAB), F32)
    slab_spec = pl.BlockSpec((1, ts, SLAB), lambda b, s: (b, s, 0))
    tail_spec = pl.BlockSpec((1, ts, SLAB), lambda b, s: (b, jnp.maximum(s - first_tail, 0), 0))
    return pl.pallas_call(
        functools.partial(_inproj_kernel, first_tail_tile=first_tail),
        out_shape=(slab,) * N_SLABS + (jax.ShapeDtypeStruct((B, S, 2 * N_HEADS_M), F32), tail, tail),
        grid=(B, n_tiles),
        in_specs=[pl.BlockSpec((1, ts, D), lambda b, s: (b, s, 0)),
                  pl.BlockSpec((None, None, 1, D), lambda b, s: (b, 0, 0, 0)),
                  pl.BlockSpec((None, None, 1, D), lambda b, s: (b, 1, 0, 0)),
                  pl.BlockSpec((1, D), lambda b, s: (0, 0)),
                  pl.BlockSpec((D, N_SLABS * SLAB), lambda b, s: (0, 0)),
                  pl.BlockSpec((D, LANES), lambda b, s: (0, 0)),
                  pl.BlockSpec((1, LANES), lambda b, s: (0, 0))],
        out_specs=(slab_spec,) * N_SLABS + (pl.BlockSpec((1, ts, 2 * N_HEADS_M), lambda b, s: (b, s, 0)),
                                            tail_spec, tail_spec),
        compiler_params=_cparams(("parallel", "arbitrary")),
        name="in_proj",
    )(x, mods, mods, g, w_main, w_gate, b_gate)


def _attn_prompt_kernel(q_ref, k_ref, v_ref, bias_ref, o_ref):
    i = pl.program_id(2)
    ws = pl.multiple_of(jnp.maximum(i * Q_BLOCK - BAND_PAST, 0), Q_BLOCK)
    variant = jnp.minimum(i, BAND_PAST // Q_BLOCK)
    q = q_ref[0]
    kw = k_ref[0, pl.ds(ws, K_WINDOW), :]
    vw = v_ref[0, pl.ds(ws, K_WINDOW), :]
    lane = lax.broadcasted_iota(jnp.int32, q.shape, 1)
    outs = []
    for hh in range(2):
        in_head = (lane >= hh * HD_A) & (lane < (hh + 1) * HD_A)
        qh = jnp.where(in_head, q * (HD_A ** -0.5), 0.0).astype(BF16)
        s = lax.dot_general(qh, kw, (((1,), (1,)), ((), ())), preferred_element_type=F32)
        s = s + bias_ref[variant, hh]
        mx = jnp.max(s, axis=-1, keepdims=True)
        p = jnp.exp(s - mx)
        l = jnp.sum(p, axis=-1, keepdims=True)
        o = jnp.dot(p.astype(BF16), vw, preferred_element_type=F32)
        outs.append(o / l)
    o_ref[0] = jnp.where(lane < HD_A, outs[0], outs[1]).astype(BF16)


def _band_bias_table(table):
    a = jnp.arange(Q_BLOCK)[:, None]
    j = jnp.arange(K_WINDOW)[None, :]
    tabs = []
    for off in range(0, BAND_PAST + 1, Q_BLOCK):
        dist = a + off - j
        qc = (a + off) // CHUNK
        kc = j // CHUNK
        valid = (kc <= qc) & (kc >= qc - BAND_CHUNKS)
        bias = table[:, jnp.clip(dist, -MAX_REL, MAX_REL) + MAX_REL].astype(F32)
        tabs.append(jnp.where(valid[None], bias, NEG))
    return jnp.stack(tabs)


def _attn_prompt(q, k, v, bias_tab):
    B, S, _ = q.shape
    nv = bias_tab.shape[0]
    return pl.pallas_call(
        _attn_prompt_kernel,
        out_shape=jax.ShapeDtypeStruct((B, S, D_ATT), BF16),
        grid=(N_HEADS_A // 2, B, S // Q_BLOCK),
        in_specs=[pl.BlockSpec((1, Q_BLOCK, LANES), lambda hp, b, i: (b, i, hp)),
                  pl.BlockSpec((1, S, LANES), lambda hp, b, i: (b, 0, hp)),
                  pl.BlockSpec((1, S, LANES), lambda hp, b, i: (b, 0, hp)),
                  pl.BlockSpec((nv, 2, Q_BLOCK, K_WINDOW), lambda hp, b, i: (0, hp, 0, 0))],
        out_specs=pl.BlockSpec((1, Q_BLOCK, LANES), lambda hp, b, i: (b, i, hp)),
        compiler_params=_cparams(("parallel", "parallel", "arbitrary")),
        name="attn_prompt",
    )(q, k, v, bias_tab)


def _attn_sample_kernel(q_ref, kn_ref, vn_ref, kc_ref, vc_ref, bc_ref, bn_ref, o_ref):
    for h in range(N_HEADS_A):
        q = (q_ref[0, h] * (HD_A ** -0.5)).astype(BF16)
        kn = kn_ref[0, h].astype(BF16)
        vn = vn_ref[0, h].astype(BF16)
        kc = kc_ref[0, h].astype(BF16)
        vc = vc_ref[0, h].astype(BF16)
        nt = (((1,), (1,)), ((), ()))
        sc = lax.dot_general(q, kc, nt, preferred_element_type=F32) + bc_ref[h]
        sn = lax.dot_general(q, kn, nt, preferred_element_type=F32) + bn_ref[h]
        mx = jnp.maximum(jnp.max(sc, axis=-1, keepdims=True), jnp.max(sn, axis=-1, keepdims=True))
        pc = jnp.exp(sc - mx)
        pn = jnp.exp(sn - mx)
        l = jnp.sum(pc, axis=-1, keepdims=True) + jnp.sum(pn, axis=-1, keepdims=True)
        o = (jnp.dot(pc.astype(BF16), vc, preferred_element_type=F32)
             + jnp.dot(pn.astype(BF16), vn, preferred_element_type=F32))
        o_ref[0, h] = o / l


def _attn_sample(q, kn, vn, kc, vc, bias_c, bias_n):
    B, H, T, d = q.shape
    W = kc.shape[2]
    new = pl.BlockSpec((1, H, T, d), lambda b: (b, 0, 0, 0))
    old = pl.BlockSpec((1, H, W, d), lambda b: (b, 0, 0, 0))
    return pl.pallas_call(
        _attn_sample_kernel,
        out_shape=jax.ShapeDtypeStruct((B, H, T, d), F32),
        grid=(B,),
        in_specs=[new, new, new, old, old,
                  pl.BlockSpec((H, T, W), lambda b: (0, 0, 0)),
                  pl.BlockSpec((H, T, T), lambda b: (0, 0, 0))],
        out_specs=new,
        compiler_params=_cparams(("parallel",)),
        name="attn_sample",
    )(q, kn, vn, kc, vc, bias_c, bias_n)


def _mlstm_kernel(q_ref, k_ref, v_ref, om_ref, gc_ref, gml_ref, c0_ref, n0_ref, m0_ref,
                  h_ref, c_ref, n_ref, m_ref, *, L, nc):
    c_ref[0] = c0_ref[0]
    n_ref[0] = n0_ref[0]
    m_ref[0] = m0_ref[0]
    tt = lax.broadcasted_iota(jnp.int32, (L, L), 0)
    ss = lax.broadcasted_iota(jnp.int32, (L, L), 1)
    causal = ss <= tt
    eye = ss == tt
    nt = (((1,), (1,)), ((), ()))
    tn = (((0,), (0,)), ((), ()))

    def chunk(ci, carry):
        off = pl.multiple_of(ci * L, L)
        g = gc_ref[0, pl.ds(off, L), :]
        for hh in range(N_HEADS_M):
            cols = slice(hh * HD_M, (hh + 1) * HD_M)
            ig_c = g[:, hh:hh + 1]
            lf_c = g[:, N_HEADS_M + hh:N_HEADS_M + hh + 1]
            ig_r = jnp.sum(jnp.where(eye, ig_c, 0.0), axis=0, keepdims=True)
            lf_r = jnp.sum(jnp.where(eye, lf_c, 0.0), axis=0, keepdims=True)
            b_r = jnp.sum(jnp.where(tt <= ss, lf_c, 0.0), axis=0, keepdims=True)
            b_c = jnp.sum(jnp.where(causal, lf_r, 0.0), axis=1, keepdims=True)
            a_r = ig_r - b_r
            a_c = ig_c - b_c
            m_prev = m_ref[0, hh][:, 0:1]
            Mt = jnp.maximum(jnp.max(jnp.where(causal, a_r, -jnp.inf), axis=1, keepdims=True), m_prev)
            P = jnp.where(causal, jnp.exp(a_r - Mt), 0.0)
            decay = jnp.exp(m_prev - Mt)
            q = q_ref[0, pl.ds(off, L), cols]
            k = k_ref[0, pl.ds(off, L), cols]
            v = v_ref[0, pl.ds(off, L), cols]
            C = c_ref[0, hh]
            n = n_ref[0, hh]
            S = P * lax.dot_general(q, k, nt, preferred_element_type=F32)
            num = decay * jnp.dot(q, C.astype(BF16), preferred_element_type=F32) \
                + jnp.dot(S.astype(BF16), v, preferred_element_type=F32)
            den = decay * jnp.sum(q.astype(F32) * n, axis=1, keepdims=True) + jnp.sum(S, axis=1, keepdims=True)
            hv = num / jnp.maximum(jnp.abs(den), jnp.exp(-(b_c + Mt)))
            hn = hv * lax.rsqrt(jnp.mean(hv * hv, axis=-1, keepdims=True) + EPS) * gml_ref[:, cols]
            om = om_ref[0, pl.ds(off, L), cols].astype(F32)
            h_ref[0, pl.ds(off, L), cols] = (hn * jax.nn.sigmoid(om)).astype(BF16)
            M_last = Mt[L - 1:L, :]
            b_last = b_c[L - 1:L, :]
            kw = k.astype(F32) * jnp.exp(a_c - M_last)
            cd = jnp.exp(m_prev - M_last)
            c_ref[0, hh] = cd * C + lax.dot_general(kw.astype(BF16), v, tn, preferred_element_type=F32)
            n_ref[0, hh] = cd * n + jnp.sum(kw, axis=0, keepdims=True)
            m_ref[0, hh] = jnp.broadcast_to(b_last + M_last, (1, HD_M))
        return carry

    lax.fori_loop(0, nc, chunk, 0)


def _mlstm(q, k, v, om, gc, gml, c0, n0, m0, L):
    B, S, _ = q.shape
    H = N_HEADS_M
    seq = pl.BlockSpec((1, S, D_MLSTM), lambda b: (b, 0, 0))
    cst = pl.BlockSpec((1, H, HD_M, HD_M), lambda b: (b, 0, 0, 0))
    vec = pl.BlockSpec((1, H, 1, HD_M), lambda b: (b, 0, 0, 0))
    return pl.pallas_call(
        functools.partial(_mlstm_kernel, L=L, nc=S // L),
        out_shape=(jax.ShapeDtypeStruct((B, S, D_MLSTM), BF16),
                   jax.ShapeDtypeStruct((B, H, HD_M, HD_M), F32),
                   jax.ShapeDtypeStruct((B, H, 1, HD_M), F32),
                   jax.ShapeDtypeStruct((B, H, 1, HD_M), F32)),
        grid=(B,),
        in_specs=[seq, seq, seq, seq,
                  pl.BlockSpec((1, S, 2 * H), lambda b: (b, 0, 0)),
                  pl.BlockSpec((1, D_MLSTM), lambda b: (0, 0)),
                  cst, vec, vec],
        out_specs=(seq, cst, vec, vec),
        compiler_params=_cparams(("parallel",)),
        name="mlstm",
    )(q, k, v, om, gc, gml, c0, n0, m0)


def _outproj_kernel(att_ref, hm_ref, x_ref, gt_ref, sh_ref, sc_ref, g_ref, wo_ref, wr_ref, br_ref,
                    x1_ref, h2_ref, gates_ref):
    mix = (jnp.dot(att_ref[0], wo_ref[:D_ATT, :], preferred_element_type=F32)
           + jnp.dot(hm_ref[0], wo_ref[D_ATT:, :], preferred_element_type=F32))
    x1 = x_ref[0] + gt_ref[...] * mix
    x1_ref[0] = x1
    hb = _norm_mod(x1, g_ref[...], sc_ref[...], sh_ref[...]).astype(BF16)
    h2_ref[0] = hb
    logit = jnp.dot(hb, wr_ref[...], preferred_element_type=F32) + br_ref[...]
    lane = lax.broadcasted_iota(jnp.int32, logit.shape, 1)
    big = jnp.int32(LANES)
    is_g = lane < N_GROUPS
    mg = jnp.max(jnp.where(is_g, logit, -jnp.inf), axis=-1, keepdims=True)
    eg = jnp.where(is_g, jnp.exp(logit - mg), 0.0)
    pg = eg / jnp.sum(eg, axis=-1, keepdims=True)
    p_top = jnp.max(pg, axis=-1, keepdims=True)
    g_idx = jnp.min(jnp.where(is_g & (pg == p_top), lane, big), axis=-1, keepdims=True)
    e_lo = ROUTER_E0 + g_idx * EXP_PER_GROUP
    sel = (lane >= e_lo) & (lane < e_lo + EXP_PER_GROUP)
    me = jnp.max(jnp.where(sel, logit, -jnp.inf), axis=-1, keepdims=True)
    ee = jnp.where(sel, jnp.exp(logit - me), 0.0)
    pe = ee / jnp.sum(ee, axis=-1, keepdims=True)
    v1 = jnp.max(jnp.where(sel, pe, -1.0), axis=-1, keepdims=True)
    i1 = jnp.min(jnp.where(sel & (pe == v1), lane, big), axis=-1, keepdims=True)
    sel2 = sel & (lane != i1)
    v2 = jnp.max(jnp.where(sel2, pe, -1.0), axis=-1, keepdims=True)
    i2 = jnp.min(jnp.where(sel2 & (pe == v2), lane, big), axis=-1, keepdims=True)
    tot = v1 + v2
    w1 = v1 / tot * p_top
    w2 = v2 / tot * p_top
    gates_ref[0] = jnp.where(lane == i1, w1, jnp.where(lane == i2, w2, 0.0))


def _out_proj(att, hm, x, mods, g, w_out, w_router, b_router, ts):
    B, S, D = x.shape
    tile = lambda w: pl.BlockSpec((1, ts, w), lambda b, s: (b, s, 0))
    mod = lambda i: pl.BlockSpec((None, None, 1, D), lambda b, s: (b, i, 0, 0))
    const = lambda shape: pl.BlockSpec(shape, lambda b, s: (0, 0))
    return pl.pallas_call(
        _outproj_kernel,
        out_shape=(jax.ShapeDtypeStruct((B, S, D), F32),
                   jax.ShapeDtypeStruct((B, S, D), BF16),
                   jax.ShapeDtypeStruct((B, S, LANES), F32)),
        grid=(B, S // ts),
        in_specs=[tile(D_ATT), tile(D_MLSTM), tile(D), mod(2), mod(3), mod(4), const((1, D)),
                  const((D, D)), const((D, LANES)), const((1, LANES))],
        out_specs=(tile(D), tile(D), tile(LANES)),
        compiler_params=_cparams(("parallel", "parallel")),
        name="out_proj",
    )(att, hm, x, mods, mods, mods, g, w_out, w_router, b_router)


def _moe_kernel(h_ref, gates_ref, x_ref, gt_ref, wg_ref, wu_ref, wd_ref, o_ref, acc_ref):
    e = pl.program_id(2)

    @pl.when(e == 0)
    def _():
        acc_ref[...] = jnp.zeros_like(acc_ref)

    h = h_ref[0]
    a = jnp.dot(h, wg_ref[0], preferred_element_type=F32)
    u = jnp.dot(h, wu_ref[0], preferred_element_type=F32)
    act = (a * jax.nn.sigmoid(a) * u).astype(BF16)
    y = jnp.dot(act, wd_ref[0], preferred_element_type=F32)
    gates = gates_ref[0]
    lane = lax.broadcasted_iota(jnp.int32, gates.shape, 1)
    ge = jnp.sum(jnp.where(lane == ROUTER_E0 + e, gates, 0.0), axis=-1, keepdims=True)
    acc_ref[...] += ge * y

    @pl.when(e == N_EXPERTS - 1)
    def _():
        o_ref[0] = x_ref[0] + gt_ref[...] * acc_ref[...]


def _moe(h2, gates, x1, mods, w_gate, w_up, w_down, ts):
    B, S, D = x1.shape
    tile = lambda w: pl.BlockSpec((1, ts, w), lambda b, s, e: (b, s, 0))
    return pl.pallas_call(
        _moe_kernel,
        out_shape=jax.ShapeDtypeStruct((B, S, D), F32),
        grid=(B, S // ts, N_EXPERTS),
        in_specs=[tile(D), tile(LANES), tile(D),
                  pl.BlockSpec((None, None, 1, D), lambda b, s, e: (b, 5, 0, 0)),
                  pl.BlockSpec((1, D, D_EXPERT), lambda b, s, e: (e, 0, 0)),
                  pl.BlockSpec((1, D, D_EXPERT), lambda b, s, e: (e, 0, 0)),
                  pl.BlockSpec((1, D_EXPERT, D), lambda b, s, e: (e, 0, 0))],
        out_specs=tile(D),
        scratch_shapes=[pltpu.VMEM((ts, D), F32)],
        compiler_params=_cparams(("parallel", "parallel", "arbitrary")),
        name="moe",
    )(h2, gates, x1, mods, w_gate, w_up, w_down)


def _final_kernel(x_ref, sh_ref, sc_ref, g_ref, o_ref):
    o_ref[0] = _norm_mod(x_ref[0], g_ref[...], sc_ref[...], sh_ref[...])


def _final(x, mods_f, g, ts):
    B, S, D = x.shape
    return pl.pallas_call(
        _final_kernel,
        out_shape=jax.ShapeDtypeStruct((B, S, D), F32),
        grid=(B, S // ts),
        in_specs=[pl.BlockSpec((1, ts, D), lambda b, s: (b, s, 0)),
                  pl.BlockSpec((None, None, 1, D), lambda b, s: (b, 0, 0, 0)),
                  pl.BlockSpec((None, None, 1, D), lambda b, s: (b, 1, 0, 0)),
                  pl.BlockSpec((1, D), lambda b, s: (0, 0))],
        out_specs=pl.BlockSpec((1, ts, D), lambda b, s: (b, s, 0)),
        compiler_params=_cparams(("parallel", "parallel")),
        name="final_norm",
    )(x, mods_f, mods_f, g)


def _heads(t, n):
    B, S, _ = t.shape
    return t.reshape(B, S, n, -1).transpose(0, 2, 1, 3)


def _trunk(x, mods, mods_f, wts, attn_fn, state_fn, ts, keep, L):
    B, S, D = x.shape
    states = []
    for l in range(mods.shape[0]):
        w = wts[l]
        m = mods[l].reshape(B, 6, 1, D)
        qa, ka, va, qm, km, vm, om, gc, kt, vt = _in_proj(
            x, m, w["g1"], w["w_main"], w["w_gate"], w["b_gate"], ts, keep)
        att = attn_fn(l, qa, ka, va, kt, vt)
        c0, n0, m0 = state_fn(l, B)
        hm, C, n, mm = _mlstm(qm, km, vm, om, gc, w["g_mlstm"], c0, n0, m0, L)
        x1, h2, gates = _out_proj(att, hm, x, m, w["g2"], w["w_out"], w["w_router"], w["b_router"], ts)
        x = _moe(h2, gates, x1, m, w["w_gate_e"], w["w_up_e"], w["w_down_e"], ts)
        states.append((_heads(kt, N_HEADS_A), _heads(vt, N_HEADS_A), C, n[:, :, 0, :], mm[:, :, 0, 0]))
    y = _final(x, mods_f.reshape(B, 2, 1, D), wts[0]["g_final"], ts)
    return y, states


def kernel(x_prompt, x_sample, c_prompt, c_sample, cache_attn_k, cache_attn_v, state_mlstm_C, state_mlstm_n,
           state_mlstm_m, w_ada, b_ada, g_norm, w_in, b_igate, b_fgate, rel_bias, g_mlstm, w_out, w_router_g,
           b_router_g, w_router_e, b_router_e, w_gate, w_up, w_down, w_ada_f, b_ada_f, g_final):
    depth = w_ada.shape[0]
    Bp, Sp, D = x_prompt.shape
    Bs, Ss, _ = x_sample.shape

    c_all = jnp.concatenate([c_prompt, c_sample], axis=0)
    pad = (-c_all.shape[0]) % 16
    c_all = jnp.pad(c_all, ((0, pad), (0, 0)))
    mods_all = _ada(c_all, w_ada, b_ada, 1536)
    mods_fin = _ada(c_all, w_ada_f[None], b_ada_f[None], 1024)[0]

    wts = []
    for l in range(depth):
        wl = w_in[l]
        n_main = N_SLABS * SLAB
        w_gate_cols = jnp.pad(wl[:, n_main:], ((0, 0), (0, LANES - 2 * N_HEADS_M)))
        b_gate_cols = jnp.pad(jnp.concatenate([b_igate[l], b_fgate[l]]), (0, LANES - 2 * N_HEADS_M))
        w_r = jnp.concatenate([w_router_g[l], w_router_e[l]], axis=1)
        b_r = jnp.concatenate([b_router_g[l], b_router_e[l]])
        npad = LANES - N_GROUPS - N_EXPERTS
        wts.append(dict(
            g1=g_norm[l, 0][None], g2=g_norm[l, 1][None], g_final=g_final[None],
            w_main=wl[:, :n_main].astype(BF16), w_gate=w_gate_cols.astype(BF16), b_gate=b_gate_cols[None],
            g_mlstm=g_mlstm[l][None], w_out=w_out[l].astype(BF16),
            w_router=jnp.pad(w_r, ((0, 0), (0, npad))).astype(BF16), b_router=jnp.pad(b_r, (0, npad))[None],
            w_gate_e=w_gate[l].astype(BF16), w_up_e=w_up[l].astype(BF16), w_down_e=w_down[l].astype(BF16)))

    bias_tabs = [_band_bias_table(rel_bias[l]) for l in range(depth)]

    def attn_p(l, qa, ka, va, kt, vt):
        return _attn_prompt(qa, ka, va, bias_tabs[l])

    def state_p(l, B):
        return (jnp.zeros((B, N_HEADS_M, HD_M, HD_M), F32), jnp.zeros((B, N_HEADS_M, 1, HD_M), F32),
                jnp.zeros((B, N_HEADS_M, 1, HD_M), F32))

    keep_p = min(BAND_PAST, Sp)
    y_p, sp = _trunk(x_prompt, mods_all[:, :Bp], mods_fin[:Bp], wts, attn_p, state_p,
                     ts=512, keep=keep_p, L=256)

    Wc = cache_attn_k.shape[3]
    tq = jnp.arange(Ss)[:, None]
    dist_c = Wc + tq - jnp.arange(Wc)[None, :]
    dist_n = tq - jnp.arange(Ss)[None, :]
    lookup = lambda l, dist: rel_bias[l][:, jnp.clip(dist, -MAX_REL, MAX_REL) + MAX_REL].astype(F32)

    def attn_s(l, qa, ka, va, kt, vt):
        o = _attn_sample(_heads(qa.astype(F32), N_HEADS_A), _heads(kt, N_HEADS_A), _heads(vt, N_HEADS_A),
                         cache_attn_k[l], cache_attn_v[l], lookup(l, dist_c), lookup(l, dist_n))
        return o.transpose(0, 2, 1, 3).reshape(Bs, Ss, D_ATT).astype(BF16)

    def state_s(l, B):
        return (state_mlstm_C[l], state_mlstm_n[l][:, :, None, :],
                jnp.broadcast_to(state_mlstm_m[l][:, :, None, None], (B, N_HEADS_M, 1, HD_M)))

    y_s, ss = _trunk(x_sample, mods_all[:, Bp:Bp + Bs], mods_fin[Bp:Bp + Bs], wts, attn_s, state_s,
                     ts=Ss, keep=Ss, L=Ss)

    stack = lambda sts, i: jnp.stack([s[i] for s in sts])
    return (y_p, y_s,
            stack(sp, 0), stack(sp, 1), stack(sp, 2), stack(sp, 3), stack(sp, 4),
            stack(ss, 0), stack(ss, 1), stack(ss, 2), stack(ss, 3), stack(ss, 4))
```

```python
import functools

import jax
import jax.numpy as jnp
from jax import lax
from jax.experimental import pallas as pl
from jax.experimental.pallas import tpu as pltpu

F32 = jnp.float32
BF16 = jnp.bfloat16

D_MODEL = 1024
CHUNK = 64
BAND_CHUNKS = 8
BAND_PAST = BAND_CHUNKS * CHUNK
D_ATT = 512
N_HEADS_A = 8
HD_A = 64
MAX_REL = 128
D_MLSTM = 512
N_HEADS_M = 4
HD_M = 128
N_GROUPS = 4
EXP_PER_GROUP = 4
N_EXPERTS = 16
D_EXPERT = 256
EPS = 1e-6
NEG = -1e30

LANES = 128
SUBLANES = 8
N_SLABS = 7
SLAB = 512
Q_BLOCK = 256
K_WINDOW = BAND_PAST + Q_BLOCK
ROUTER_E0 = N_GROUPS
PACKED = D_MODEL // 2
ROW_WORDS = PACKED + LANES
MOE_TILE = 512
VMEM_LIMIT = 56 * 1024 * 1024


def _cparams(sem):
    return pltpu.CompilerParams(dimension_semantics=sem, vmem_limit_bytes=VMEM_LIMIT)


def _ada_kernel(c_ref, w_ref, b_ref, o_ref):
    c = c_ref[...]
    a = (c * jax.nn.sigmoid(c)).astype(BF16)
    o_ref[0] = jnp.dot(a, w_ref[0].astype(BF16), preferred_element_type=F32) + b_ref[0]


def _ada(c, w, b, tn):
    nl, d, n = w.shape
    bc = c.shape[0]
    return pl.pallas_call(
        _ada_kernel,
        out_shape=jax.ShapeDtypeStruct((nl, bc, n), F32),
        grid=(nl, n // tn),
        in_specs=[pl.BlockSpec((bc, d), lambda l, j: (0, 0)),
                  pl.BlockSpec((1, d, tn), lambda l, j: (l, 0, j)),
                  pl.BlockSpec((1, 1, tn), lambda l, j: (l, 0, j))],
        out_specs=pl.BlockSpec((1, bc, tn), lambda l, j: (l, 0, j)),
        compiler_params=_cparams(("parallel", "parallel")),
        name="ada",
    )(c, w, b.reshape(nl, 1, n))


def _norm_mod(x, g, scale, shift):
    r = lax.rsqrt(jnp.mean(x * x, axis=-1, keepdims=True) + EPS)
    return (x * r * g) * (1.0 + scale) + shift


def _inproj_kernel(x_ref, sh_ref, sc_ref, g_ref, w_ref, wg_ref, bg_ref,
                   qa_ref, ka_ref, va_ref, qm_ref, km_ref, vm_ref, om_ref, gc_ref, kt_ref, vt_ref,
                   *, first_tail_tile):
    hb = _norm_mod(x_ref[0], g_ref[...], sc_ref[...], sh_ref[...]).astype(BF16)
    outs = (qa_ref, ka_ref, va_ref, qm_ref, km_ref, vm_ref, om_ref)
    in_tail = pl.program_id(1) >= first_tail_tile
    for j, o_ref in enumerate(outs):
        p = jnp.dot(hb, w_ref[:, j * SLAB:(j + 1) * SLAB], preferred_element_type=F32)
        if j == 4:
            p = p * (HD_M ** -0.5)
        o_ref[0] = p.astype(BF16)
        if j in (1, 2):
            t_ref = kt_ref if j == 1 else vt_ref

            @pl.when(in_tail)
            def _():
                t_ref[0] = p
    pg = jnp.dot(hb, wg_ref[...], preferred_element_type=F32) + bg_ref[...]
    lsig = jnp.minimum(pg, 0.0) - jnp.log1p(jnp.exp(-jnp.abs(pg)))
    lane = lax.broadcasted_iota(jnp.int32, pg.shape, 1)
    gc = jnp.where(lane < N_HEADS_M, pg, lsig)
    gc_ref[0] = gc[:, :2 * N_HEADS_M]


def _in_proj(x, mods, g, w_main, w_gate, b_gate, ts, keep):
    B, S, D = x.shape
    n_tiles = S // ts
    n_tail = keep // ts
    first_tail = n_tiles - n_tail
    slab = jax.ShapeDtypeStruct((B, S, SLAB), BF16)
    tail = jax.ShapeDtypeStruct((B, keep, SLAB), F32)
    slab_spec = pl.BlockSpec((1, ts, SLAB), lambda b, s: (b, s, 0))
    tail_spec = pl.BlockSpec((1, ts, SLAB), lambda b, s: (b, jnp.maximum(s - first_tail, 0), 0))
    return pl.pallas_call(
        functools.partial(_inproj_kernel, first_tail_tile=first_tail),
        out_shape=(slab,) * N_SLABS + (jax.ShapeDtypeStruct((B, S, 2 * N_HEADS_M), F32), tail, tail),
        grid=(B, n_tiles),
        in_specs=[pl.BlockSpec((1, ts, D), lambda b, s: (b, s, 0)),
                  pl.BlockSpec((None, None, 1, D), lambda b, s: (b, 0, 0, 0)),
                  pl.BlockSpec((None, None, 1, D), lambda b, s: (b, 1, 0, 0)),
                  pl.BlockSpec((1, D), lambda b, s: (0, 0)),
                  pl.BlockSpec((D, N_SLABS * SLAB), lambda b, s: (0, 0)),
                  pl.BlockSpec((D, LANES), lambda b, s: (0, 0)),
                  pl.BlockSpec((1, LANES), lambda b, s: (0, 0))],
        out_specs=(slab_spec,) * N_SLABS + (pl.BlockSpec((1, ts, 2 * N_HEADS_M), lambda b, s: (b, s, 0)),
                                            tail_spec, tail_spec),
        compiler_params=_cparams(("parallel", "arbitrary")),
        name="in_proj",
    )(x, mods, mods, g, w_main, w_gate, b_gate)


def _attn_prompt_kernel(q_ref, k_ref, v_ref, bias_ref, o_ref):
    i = pl.program_id(2)
    ws = pl.multiple_of(jnp.maximum(i * Q_BLOCK - BAND_PAST, 0), Q_BLOCK)
    variant = jnp.minimum(i, BAND_PAST // Q_BLOCK)
    q = q_ref[0]
    kw = k_ref[0, pl.ds(ws, K_WINDOW), :]
    vw = v_ref[0, pl.ds(ws, K_WINDOW), :]
    lane = lax.broadcasted_iota(jnp.int32, q.shape, 1)
    outs = []
    for hh in range(2):
        in_head = (lane >= hh * HD_A) & (lane < (hh + 1) * HD_A)
        qh = jnp.where(in_head, q * (HD_A ** -0.5), 0.0).astype(BF16)
        s = lax.dot_general(qh, kw, (((1,), (1,)), ((), ())), preferred_element_type=F32)
        s = s + bias_ref[variant, hh]
        mx = jnp.max(s, axis=-1, keepdims=True)
        p = jnp.exp(s - mx)
        l = jnp.sum(p, axis=-1, keepdims=True)
        o = jnp.dot(p.astype(BF16), vw, preferred_element_type=F32)
        outs.append(o / l)
    o_ref[0] = jnp.where(lane < HD_A, outs[0], outs[1]).astype(BF16)


def _band_bias_table(table):
    H = table.shape[0]
    a = jnp.arange(Q_BLOCK)[:, None]
    j = jnp.arange(K_WINDOW)[None, :]
    period = Q_BLOCK + K_WINDOW
    u = jnp.arange(period)
    tabs = []
    for off in range(0, BAND_PAST + 1, Q_BLOCK):
        prof = table[:, jnp.clip(u + off - (K_WINDOW - 1), -MAX_REL, MAX_REL) + MAX_REL].astype(F32)
        skew = jnp.tile(prof, (1, Q_BLOCK + 1))[:, :Q_BLOCK * (period + 1)].reshape(H, Q_BLOCK, period + 1)
        bias = skew[:, :, :K_WINDOW][:, :, ::-1]
        qc = (a + off) // CHUNK
        kc = j // CHUNK
        valid = (kc <= qc) & (kc >= qc - BAND_CHUNKS)
        tabs.append(jnp.where(valid[None], bias, NEG))
    return jnp.stack(tabs)


def _attn_prompt(q, k, v, bias_tab):
    B, S, _ = q.shape
    nv = bias_tab.shape[0]
    return pl.pallas_call(
        _attn_prompt_kernel,
        out_shape=jax.ShapeDtypeStruct((B, S, D_ATT), BF16),
        grid=(N_HEADS_A // 2, B, S // Q_BLOCK),
        in_specs=[pl.BlockSpec((1, Q_BLOCK, LANES), lambda hp, b, i: (b, i, hp)),
                  pl.BlockSpec((1, S, LANES), lambda hp, b, i: (b, 0, hp)),
                  pl.BlockSpec((1, S, LANES), lambda hp, b, i: (b, 0, hp)),
                  pl.BlockSpec((nv, 2, Q_BLOCK, K_WINDOW), lambda hp, b, i: (0, hp, 0, 0))],
        out_specs=pl.BlockSpec((1, Q_BLOCK, LANES), lambda hp, b, i: (b, i, hp)),
        compiler_params=_cparams(("parallel", "parallel", "arbitrary")),
        name="attn_prompt",
    )(q, k, v, bias_tab)


def _attn_sample_kernel(q_ref, kn_ref, vn_ref, kc_ref, vc_ref, bc_ref, bn_ref, o_ref):
    for h in range(N_HEADS_A):
        q = (q_ref[0, h] * (HD_A ** -0.5)).astype(BF16)
        kn = kn_ref[0, h].astype(BF16)
        vn = vn_ref[0, h].astype(BF16)
        kc = kc_ref[0, h].astype(BF16)
        vc = vc_ref[0, h].astype(BF16)
        nt = (((1,), (1,)), ((), ()))
        sc = lax.dot_general(q, kc, nt, preferred_element_type=F32) + bc_ref[h]
        sn = lax.dot_general(q, kn, nt, preferred_element_type=F32) + bn_ref[h]
        mx = jnp.maximum(jnp.max(sc, axis=-1, keepdims=True), jnp.max(sn, axis=-1, keepdims=True))
        pc = jnp.exp(sc - mx)
        pn = jnp.exp(sn - mx)
        l = jnp.sum(pc, axis=-1, keepdims=True) + jnp.sum(pn, axis=-1, keepdims=True)
        o = (jnp.dot(pc.astype(BF16), vc, preferred_element_type=F32)
             + jnp.dot(pn.astype(BF16), vn, preferred_element_type=F32))
        o_ref[0, h] = o / l


def _attn_sample(q, kn, vn, kc, vc, bias_c, bias_n):
    B, H, T, d = q.shape
    W = kc.shape[2]
    new = pl.BlockSpec((1, H, T, d), lambda b: (b, 0, 0, 0))
    old = pl.BlockSpec((1, H, W, d), lambda b: (b, 0, 0, 0))
    return pl.pallas_call(
        _attn_sample_kernel,
        out_shape=jax.ShapeDtypeStruct((B, H, T, d), F32),
        grid=(B,),
        in_specs=[new, new, new, old, old,
                  pl.BlockSpec((H, T, W), lambda b: (0, 0, 0)),
                  pl.BlockSpec((H, T, T), lambda b: (0, 0, 0))],
        out_specs=new,
        compiler_params=_cparams(("parallel",)),
        name="attn_sample",
    )(q, kn, vn, kc, vc, bias_c, bias_n)


def _mlstm_kernel(q_ref, k_ref, v_ref, om_ref, gc_ref, gml_ref, c0_ref, n0_ref, m0_ref,
                  h_ref, c_ref, n_ref, m_ref, *, L, nc):
    c_ref[0] = c0_ref[0]
    n_ref[0] = n0_ref[0]
    m_ref[0] = m0_ref[0]
    tt = lax.broadcasted_iota(jnp.int32, (L, L), 0)
    ss = lax.broadcasted_iota(jnp.int32, (L, L), 1)
    causal = ss <= tt
    eye = ss == tt
    nt = (((1,), (1,)), ((), ()))
    tn = (((0,), (0,)), ((), ()))

    def chunk(ci, carry):
        off = pl.multiple_of(ci * L, L)
        g = gc_ref[0, pl.ds(off, L), :]
        for hh in range(N_HEADS_M):
            cols = slice(hh * HD_M, (hh + 1) * HD_M)
            ig_c = g[:, hh:hh + 1]
            lf_c = g[:, N_HEADS_M + hh:N_HEADS_M + hh + 1]
            ig_r = jnp.sum(jnp.where(eye, ig_c, 0.0), axis=0, keepdims=True)
            lf_r = jnp.sum(jnp.where(eye, lf_c, 0.0), axis=0, keepdims=True)
            b_r = jnp.sum(jnp.where(tt <= ss, lf_c, 0.0), axis=0, keepdims=True)
            b_c = jnp.sum(jnp.where(causal, lf_r, 0.0), axis=1, keepdims=True)
            a_r = ig_r - b_r
            a_c = ig_c - b_c
            m_prev = m_ref[0, hh][:, 0:1]
            Mt = jnp.maximum(jnp.max(jnp.where(causal, a_r, -jnp.inf), axis=1, keepdims=True), m_prev)
            P = jnp.where(causal, jnp.exp(a_r - Mt), 0.0)
            decay = jnp.exp(m_prev - Mt)
            q = q_ref[0, pl.ds(off, L), cols]
            k = k_ref[0, pl.ds(off, L), cols]
            v = v_ref[0, pl.ds(off, L), cols]
            C = c_ref[0, hh]
            n = n_ref[0, hh]
            S = P * lax.dot_general(q, k, nt, preferred_element_type=F32)
            num = decay * jnp.dot(q, C.astype(BF16), preferred_element_type=F32) \
                + jnp.dot(S.astype(BF16), v, preferred_element_type=F32)
            den = decay * jnp.sum(q.astype(F32) * n, axis=1, keepdims=True) + jnp.sum(S, axis=1, keepdims=True)
            hv = num / jnp.maximum(jnp.abs(den), jnp.exp(-(b_c + Mt)))
            hn = hv * lax.rsqrt(jnp.mean(hv * hv, axis=-1, keepdims=True) + EPS) * gml_ref[:, cols]
            om = om_ref[0, pl.ds(off, L), cols].astype(F32)
            h_ref[0, pl.ds(off, L), cols] = (hn * jax.nn.sigmoid(om)).astype(BF16)
            M_last = Mt[L - 1:L, :]
            b_last = b_c[L - 1:L, :]
            kw = k.astype(F32) * jnp.exp(a_c - M_last)
            cd = jnp.exp(m_prev - M_last)
            c_ref[0, hh] = cd * C + lax.dot_general(kw.astype(BF16), v, tn, preferred_element_type=F32)
            n_ref[0, hh] = cd * n + jnp.sum(kw, axis=0, keepdims=True)
            m_ref[0, hh] = jnp.broadcast_to(b_last + M_last, (1, HD_M))
        return carry

    lax.fori_loop(0, nc, chunk, 0)


def _mlstm(q, k, v, om, gc, gml, c0, n0, m0, L):
    B, S, _ = q.shape
    H = N_HEADS_M
    seq = pl.BlockSpec((1, S, D_MLSTM), lambda b: (b, 0, 0))
    cst = pl.BlockSpec((1, H, HD_M, HD_M), lambda b: (b, 0, 0, 0))
    vec = pl.BlockSpec((1, H, 1, HD_M), lambda b: (b, 0, 0, 0))
    return pl.pallas_call(
        functools.partial(_mlstm_kernel, L=L, nc=S // L),
        out_shape=(jax.ShapeDtypeStruct((B, S, D_MLSTM), BF16),
                   jax.ShapeDtypeStruct((B, H, HD_M, HD_M), F32),
                   jax.ShapeDtypeStruct((B, H, 1, HD_M), F32),
                   jax.ShapeDtypeStruct((B, H, 1, HD_M), F32)),
        grid=(B,),
        in_specs=[seq, seq, seq, seq,
                  pl.BlockSpec((1, S, 2 * H), lambda b: (b, 0, 0)),
                  pl.BlockSpec((1, D_MLSTM), lambda b: (0, 0)),
                  cst, vec, vec],
        out_specs=(seq, cst, vec, vec),
        compiler_params=_cparams(("parallel",)),
        name="mlstm",
    )(q, k, v, om, gc, gml, c0, n0, m0)


def _mix_residual_norm(att_ref, hm_ref, x_ref, gt_ref, sh_ref, sc_ref, g_ref, wo_ref):
    mix = (jnp.dot(att_ref[0], wo_ref[:D_ATT, :], preferred_element_type=F32)
           + jnp.dot(hm_ref[0], wo_ref[D_ATT:, :], preferred_element_type=F32))
    x1 = x_ref[0] + gt_ref[...] * mix
    return x1, _norm_mod(x1, g_ref[...], sc_ref[...], sh_ref[...]).astype(BF16)


def _router(hb, wr_ref, br_ref):
    logit = jnp.dot(hb, wr_ref[...], preferred_element_type=F32) + br_ref[...]
    lane = lax.broadcasted_iota(jnp.int32, logit.shape, 1)
    big = jnp.int32(LANES)
    is_g = lane < N_GROUPS
    mg = jnp.max(jnp.where(is_g, logit, -jnp.inf), axis=-1, keepdims=True)
    eg = jnp.where(is_g, jnp.exp(logit - mg), 0.0)
    pg = eg / jnp.sum(eg, axis=-1, keepdims=True)
    p_top = jnp.max(pg, axis=-1, keepdims=True)
    g_idx = jnp.min(jnp.where(is_g & (pg == p_top), lane, big), axis=-1, keepdims=True)
    e_lo = ROUTER_E0 + g_idx * EXP_PER_GROUP
    sel = (lane >= e_lo) & (lane < e_lo + EXP_PER_GROUP)
    me = jnp.max(jnp.where(sel, logit, -jnp.inf), axis=-1, keepdims=True)
    ee = jnp.where(sel, jnp.exp(logit - me), 0.0)
    pe = ee / jnp.sum(ee, axis=-1, keepdims=True)
    v1 = jnp.max(jnp.where(sel, pe, -1.0), axis=-1, keepdims=True)
    i1 = jnp.min(jnp.where(sel & (pe == v1), lane, big), axis=-1, keepdims=True)
    sel2 = sel & (lane != i1)
    v2 = jnp.max(jnp.where(sel2, pe, -1.0), axis=-1, keepdims=True)
    i2 = jnp.min(jnp.where(sel2 & (pe == v2), lane, big), axis=-1, keepdims=True)
    tot = v1 + v2
    w1 = v1 / tot * p_top
    w2 = v2 / tot * p_top
    return jnp.where(lane == i1, w1, jnp.where(lane == i2, w2, 0.0)), g_idx


def _outproj_kernel(att_ref, hm_ref, x_ref, gt_ref, sh_ref, sc_ref, g_ref, wo_ref, wr_ref, br_ref,
                    x1_ref, h2_ref, gates_ref):
    x1, hb = _mix_residual_norm(att_ref, hm_ref, x_ref, gt_ref, sh_ref, sc_ref, g_ref, wo_ref)
    x1_ref[0] = x1
    h2_ref[0] = hb
    gates_ref[0] = _router(hb, wr_ref, br_ref)[0]


def _pack_bf16_pairs(lo, hi):
    lo_bits = lax.bitcast_convert_type(lo.astype(BF16).astype(F32), jnp.uint32)
    hi_bits = lax.bitcast_convert_type(hi.astype(BF16).astype(F32), jnp.uint32)
    return (hi_bits & jnp.uint32(0xFFFF0000)) | (lo_bits >> 16)


def _unpack_bf16_pairs(packed):
    lo = lax.bitcast_convert_type(packed << 16, F32)
    hi = lax.bitcast_convert_type(packed & jnp.uint32(0xFFFF0000), F32)
    return lo, hi


def _outproj_sort_kernel(att_ref, hm_ref, x_ref, gt_ref, sh_ref, sc_ref, g_ref, wo_ref, wr_ref, br_ref,
                         x1_ref, row_ref, idx_ref, cnt_ref, fill_ref, *, region_rows):
    first = (pl.program_id(0) == 0) & (pl.program_id(1) == 0)

    @pl.when(first)
    def _():
        fill_ref[...] = jnp.zeros_like(fill_ref)

    x1, hb = _mix_residual_norm(att_ref, hm_ref, x_ref, gt_ref, sh_ref, sc_ref, g_ref, wo_ref)
    x1_ref[0] = x1
    gates, g_idx = _router(hb, wr_ref, br_ref)
    hf = hb.astype(F32)
    half = hf.shape[1] // 2
    row_ref[0, :, :half] = _pack_bf16_pairs(hf[:, :half], hf[:, half:])
    row_ref[0, :, half:] = lax.bitcast_convert_type(gates, jnp.uint32)
    ts = gates.shape[0]
    lane = lax.broadcasted_iota(jnp.int32, gates.shape, 1)
    onehot = lane == g_idx
    tt = lax.broadcasted_iota(jnp.int32, (ts, ts), 0)
    ss = lax.broadcasted_iota(jnp.int32, (ts, ts), 1)
    before = jnp.where(ss < tt, 1.0, 0.0).astype(BF16)
    rank = jnp.dot(before, jnp.where(onehot, 1.0, 0.0).astype(BF16), preferred_element_type=F32)
    fill = fill_ref[...]
    slot = jnp.sum(jnp.where(onehot, rank + fill, 0.0), axis=-1, keepdims=True)
    slot = slot + g_idx.astype(F32) * float(region_rows)
    slot_row = jnp.sum(jnp.where(ss == tt, slot, 0.0), axis=0, keepdims=True)
    idx_ref[...] = slot_row.astype(jnp.int32)
    fill = fill + jnp.sum(jnp.where(onehot, 1.0, 0.0), axis=0, keepdims=True)
    fill_ref[...] = fill
    cnt_ref[...] = fill.astype(jnp.int32)


def _out_proj(att, hm, x, mods, g, w_out, w_router, b_router, ts, region_rows=None):
    B, S, D = x.shape
    nt = S // ts
    tile = lambda w: pl.BlockSpec((1, ts, w), lambda b, s: (b, s, 0))
    mod = lambda i: pl.BlockSpec((None, None, 1, D), lambda b, s: (b, i, 0, 0))
    const = lambda shape: pl.BlockSpec(shape, lambda b, s: (0, 0))
    in_specs = [tile(D_ATT), tile(D_MLSTM), tile(D), mod(2), mod(3), mod(4), const((1, D)),
                const((D, D)), const((D, LANES)), const((1, LANES))]
    args = (att, hm, x, mods, mods, mods, g, w_out, w_router, b_router)
    if region_rows is None:
        return pl.pallas_call(
            _outproj_kernel,
            out_shape=(jax.ShapeDtypeStruct((B, S, D), F32),
                       jax.ShapeDtypeStruct((B, S, D), BF16),
                       jax.ShapeDtypeStruct((B, S, LANES), F32)),
            grid=(B, nt), in_specs=in_specs,
            out_specs=(tile(D), tile(D), tile(LANES)),
            compiler_params=_cparams(("parallel", "parallel")),
            name="out_proj",
        )(*args)
    return pl.pallas_call(
        functools.partial(_outproj_sort_kernel, region_rows=region_rows),
        out_shape=(jax.ShapeDtypeStruct((B, S, D), F32),
                   jax.ShapeDtypeStruct((B, S, ROW_WORDS), jnp.uint32),
                   jax.ShapeDtypeStruct((B * nt, 1, ts), jnp.int32),
                   jax.ShapeDtypeStruct((1, LANES), jnp.int32)),
        grid=(B, nt), in_specs=in_specs,
        out_specs=(tile(D), tile(ROW_WORDS),
                   pl.BlockSpec((None, 1, ts), lambda b, s: (b * nt + s, 0, 0)),
                   const((1, LANES))),
        scratch_shapes=[pltpu.VMEM((1, LANES), F32)],
        compiler_params=_cparams(("arbitrary", "arbitrary")),
        name="out_proj_sort",
    )(*args)


def _moe_kernel(h_ref, gates_ref, x_ref, gt_ref, wgu_ref, wd_ref, o_ref, acc_ref):
    e = pl.program_id(2)

    @pl.when(e == 0)
    def _():
        acc_ref[...] = jnp.zeros_like(acc_ref)

    h = h_ref[0]
    au = jnp.dot(h, wgu_ref[0], preferred_element_type=F32)
    a = au[:, :D_EXPERT]
    u = au[:, D_EXPERT:]
    act = (a * jax.nn.sigmoid(a) * u).astype(BF16)
    y = jnp.dot(act, wd_ref[0], preferred_element_type=F32)
    gates = gates_ref[0]
    lane = lax.broadcasted_iota(jnp.int32, gates.shape, 1)
    ge = jnp.sum(jnp.where(lane == ROUTER_E0 + e, gates, 0.0), axis=-1, keepdims=True)
    acc_ref[...] += ge * y

    @pl.when(e == N_EXPERTS - 1)
    def _():
        o_ref[0] = x_ref[0] + gt_ref[...] * acc_ref[...]


def _moe(h2, gates, x1, mods, w_gu, w_down, ts):
    B, S, D = x1.shape
    tile = lambda w: pl.BlockSpec((1, ts, w), lambda b, s, e: (b, s, 0))
    return pl.pallas_call(
        _moe_kernel,
        out_shape=jax.ShapeDtypeStruct((B, S, D), F32),
        grid=(B, S // ts, N_EXPERTS),
        in_specs=[tile(D), tile(LANES), tile(D),
                  pl.BlockSpec((None, None, 1, D), lambda b, s, e: (b, 5, 0, 0)),
                  pl.BlockSpec((1, D, 2 * D_EXPERT), lambda b, s, e: (e, 0, 0)),
                  pl.BlockSpec((1, D_EXPERT, D), lambda b, s, e: (e, 0, 0))],
        out_specs=tile(D),
        scratch_shapes=[pltpu.VMEM((ts, D), F32)],
        compiler_params=_cparams(("parallel", "parallel", "arbitrary")),
        name="moe",
    )(h2, gates, x1, mods, w_gu, w_down)


def _row_copy(src_ref, src_row, dst_ref, dst_row, sem):
    return pltpu.make_async_copy(src_ref.at[pl.ds(src_row, 1), :], dst_ref.at[pl.ds(dst_row, 1), :], sem)


def _dispatch_kernel(cnt_ref, row_ref, idx_ref, hs_ref, buf_ref, zero_ref, sem_ref, zsem_ref, *, region_rows, tm):
    i = pl.program_id(0)
    n = pl.num_programs(0)
    ts = buf_ref.shape[1]
    slot = i % 2

    def wait_slot(s):
        def body(r, c):
            _row_copy(buf_ref.at[s], 0, hs_ref, 0, sem_ref.at[s]).wait()
            return c
        lax.fori_loop(0, ts, body, 0)

    @pl.when(i >= 2)
    def _():
        wait_slot(slot)

    buf_ref[slot] = row_ref[...]

    def issue(r, c):
        _row_copy(buf_ref.at[slot], r, hs_ref, idx_ref[0, r], sem_ref.at[slot]).start()
        return c
    lax.fori_loop(0, ts, issue, 0)

    @pl.when(i == n - 1)
    def _():
        wait_slot(slot)

        @pl.when(n >= 2)
        def _():
            wait_slot(1 - slot)

        zero_ref[...] = jnp.zeros_like(zero_ref)

        def tail_copies(g):
            end = g * region_rows + cnt_ref[g]
            aligned = pl.multiple_of((end + SUBLANES - 1) // SUBLANES * SUBLANES, SUBLANES)
            block = pltpu.make_async_copy(zero_ref, hs_ref.at[pl.ds(aligned, tm), :], zsem_ref.at[g])
            rows = [(end + k < aligned, _row_copy(zero_ref, 0, hs_ref, end + k, zsem_ref.at[g]))
                    for k in range(SUBLANES - 1)]
            return block, rows

        for wait in (False, True):
            for g in range(N_GROUPS):
                block, rows = tail_copies(g)
                if wait:
                    block.wait()
                else:
                    block.start()
                for pred, cp in rows:
                    @pl.when(pred)
                    def _():
                        if wait:
                            cp.wait()
                        else:
                            cp.start()


def _dispatch(rows, idx, counts, ts, tm, region_rows):
    T = rows.shape[0]
    grid_spec = pltpu.PrefetchScalarGridSpec(
        num_scalar_prefetch=1,
        grid=(T // ts,),
        in_specs=[pl.BlockSpec((ts, ROW_WORDS), lambda i, cnt: (i, 0)),
                  pl.BlockSpec((None, 1, ts), lambda i, cnt: (i, 0, 0), memory_space=pltpu.SMEM)],
        out_specs=pl.BlockSpec(memory_space=pl.ANY),
        scratch_shapes=[pltpu.VMEM((2, ts, ROW_WORDS), jnp.uint32),
                        pltpu.VMEM((tm, ROW_WORDS), jnp.uint32),
                        pltpu.SemaphoreType.DMA((2,)),
                        pltpu.SemaphoreType.DMA((N_GROUPS,))])
    return pl.pallas_call(
        functools.partial(_dispatch_kernel, region_rows=region_rows, tm=tm),
        out_shape=jax.ShapeDtypeStruct((N_GROUPS * region_rows, ROW_WORDS), jnp.uint32),
        grid_spec=grid_spec,
        compiler_params=_cparams(("arbitrary",)),
        name="moe_dispatch",
    )(counts, rows, idx)


def _moe_group_kernel(blk_ref, grp_ref, live_ref, hs_ref, wgu_ref, wd_ref, o_ref):
    i = pl.program_id(0)

    @pl.when(live_ref[i] == 0)
    def _():
        o_ref[...] = jnp.zeros_like(o_ref)

    @pl.when(live_ref[i] != 0)
    def _():
        lo, hi = _unpack_bf16_pairs(hs_ref[:, :PACKED])
        lo = lo.astype(BF16)
        hi = hi.astype(BF16)
        gates = lax.bitcast_convert_type(hs_ref[:, PACKED:], F32)
        lane = lax.broadcasted_iota(jnp.int32, gates.shape, 1)
        e0 = ROUTER_E0 + grp_ref[i] * EXP_PER_GROUP
        y = jnp.zeros((lo.shape[0], D_MODEL), F32)
        for e in range(EXP_PER_GROUP):
            au = (jnp.dot(lo, wgu_ref[e, :PACKED, :], preferred_element_type=F32)
                  + jnp.dot(hi, wgu_ref[e, PACKED:, :], preferred_element_type=F32))
            a = au[:, :D_EXPERT]
            u = au[:, D_EXPERT:]
            act = (a * jax.nn.sigmoid(a) * u).astype(BF16)
            ge = jnp.sum(jnp.where(lane == e0 + e, gates, 0.0), axis=-1, keepdims=True)
            y = y + ge * jnp.dot(act, wd_ref[e], preferred_element_type=F32)
        o_ref[...] = _pack_bf16_pairs(y[:, :PACKED], y[:, PACKED:])


def _moe_group(hs, blk, grp, live, w_gu, w_down, tm):
    n_steps = blk.shape[0]
    n_blocks = hs.shape[0] // tm
    grid_spec = pltpu.PrefetchScalarGridSpec(
        num_scalar_prefetch=3,
        grid=(n_steps,),
        in_specs=[pl.BlockSpec((tm, ROW_WORDS), lambda i, blk, grp, live: (blk[i], 0)),
                  pl.BlockSpec((None, EXP_PER_GROUP, D_MODEL, 2 * D_EXPERT), lambda i, blk, grp, live: (grp[i], 0, 0, 0)),
                  pl.BlockSpec((None, EXP_PER_GROUP, D_EXPERT, D_MODEL), lambda i, blk, grp, live: (grp[i], 0, 0, 0))],
        out_specs=pl.BlockSpec((tm, PACKED),
                               lambda i, blk, grp, live: (jnp.where(live[i] != 0, blk[i], n_blocks), 0)))
    return pl.pallas_call(
        _moe_group_kernel,
        out_shape=jax.ShapeDtypeStruct(((n_blocks + 1) * tm, PACKED), jnp.uint32),
        grid_spec=grid_spec,
        compiler_params=_cparams(("arbitrary",)),
        name="moe_group",
    )(blk, grp, live, hs, w_gu, w_down)


def _combine_kernel(x1_ref, gt_ref, idx0_ref, idx1_ref, ys_ref, o_ref, buf_ref, sem_ref):
    i = pl.program_id(0)
    n = pl.num_programs(0)
    ts = buf_ref.shape[1]
    slot = i % 2

    def issue(idx_ref, s):
        def body(r, c):
            _row_copy(ys_ref, idx_ref[0, r], buf_ref.at[s], r, sem_ref.at[s]).start()
            return c
        lax.fori_loop(0, ts, body, 0)

    @pl.when(i == 0)
    def _():
        issue(idx0_ref, 0)

    @pl.when(i + 1 < n)
    def _():
        issue(idx1_ref, 1 - slot)

    def wait_body(r, c):
        _row_copy(ys_ref, 0, buf_ref.at[slot], 0, sem_ref.at[slot]).wait()
        return c
    lax.fori_loop(0, ts, wait_body, 0)

    lo, hi = _unpack_bf16_pairs(buf_ref[slot])
    g = gt_ref[...]
    o_ref[:, :PACKED] = x1_ref[:, :PACKED] + g[:, :PACKED] * lo
    o_ref[:, PACKED:] = x1_ref[:, PACKED:] + g[:, PACKED:] * hi


def _combine(x1, mods, idx, ys, ts):
    B, S, D = x1.shape
    nt = S // ts
    n = B * nt
    out = pl.pallas_call(
        _combine_kernel,
        out_shape=jax.ShapeDtypeStruct((B * S, D), F32),
        grid=(n,),
        in_specs=[pl.BlockSpec((ts, D), lambda i: (i, 0)),
                  pl.BlockSpec((None, None, 1, D), lambda i: (i // nt, 5, 0, 0)),
                  pl.BlockSpec((None, 1, ts), lambda i: (i, 0, 0), memory_space=pltpu.SMEM),
                  pl.BlockSpec((None, 1, ts), lambda i: (jnp.minimum(i + 1, n - 1), 0, 0), memory_space=pltpu.SMEM),
                  pl.BlockSpec(memory_space=pl.ANY)],
        out_specs=pl.BlockSpec((ts, D), lambda i: (i, 0)),
        scratch_shapes=[pltpu.VMEM((2, ts, PACKED), jnp.uint32), pltpu.SemaphoreType.DMA((2,))],
        compiler_params=_cparams(("arbitrary",)),
        name="moe_combine",
    )(x1.reshape(B * S, D), mods, idx, idx, ys)
    return out.reshape(B, S, D)


def _tile_plan(counts, tm, n_steps, region_blocks):
    tiles = (counts + tm - 1) // tm
    ends = jnp.cumsum(tiles)
    total = ends[-1]
    step = jnp.minimum(jnp.arange(n_steps, dtype=jnp.int32), jnp.maximum(total - 1, 0))
    grp = jnp.sum((step[:, None] >= ends[None, :]).astype(jnp.int32), axis=1)
    grp = jnp.minimum(grp, N_GROUPS - 1)
    starts = ends - tiles
    blk = grp * region_blocks + (step - starts[grp])
    live = (jnp.arange(n_steps, dtype=jnp.int32) < total).astype(jnp.int32)
    return blk.astype(jnp.int32), grp.astype(jnp.int32), live


def _moe_sorted(rows, idx, counts, x1, mods, w_gu, w_down, ts, tm):
    B, S, D = x1.shape
    T = B * S
    region_rows = T + tm
    n_steps = T // tm + N_GROUPS
    cnt = counts[0, :N_GROUPS]
    hs = _dispatch(rows.reshape(T, ROW_WORDS), idx, cnt, ts, tm, region_rows)
    blk, grp, live = _tile_plan(cnt, tm, n_steps, region_rows // tm)
    ys = _moe_group(hs, blk, grp, live, w_gu, w_down, tm)
    return _combine(x1, mods, idx, ys, ts)


def _final_kernel(x_ref, sh_ref, sc_ref, g_ref, o_ref):
    o_ref[0] = _norm_mod(x_ref[0], g_ref[...], sc_ref[...], sh_ref[...])


def _final(x, mods_f, g, ts):
    B, S, D = x.shape
    return pl.pallas_call(
        _final_kernel,
        out_shape=jax.ShapeDtypeStruct((B, S, D), F32),
        grid=(B, S // ts),
        in_specs=[pl.BlockSpec((1, ts, D), lambda b, s: (b, s, 0)),
                  pl.BlockSpec((None, None, 1, D), lambda b, s: (b, 0, 0, 0)),
                  pl.BlockSpec((None, None, 1, D), lambda b, s: (b, 1, 0, 0)),
                  pl.BlockSpec((1, D), lambda b, s: (0, 0))],
        out_specs=pl.BlockSpec((1, ts, D), lambda b, s: (b, s, 0)),
        compiler_params=_cparams(("parallel", "parallel")),
        name="final_norm",
    )(x, mods_f, mods_f, g)


def _heads(t, n):
    B, S, _ = t.shape
    return t.reshape(B, S, n, -1).transpose(0, 2, 1, 3)


def _trunk(x, mods, mods_f, wts, attn_fn, state_fn, ts, keep, L, sort_tokens):
    B, S, D = x.shape
    states = []
    for l in range(mods.shape[0]):
        w = wts[l]
        m = mods[l].reshape(B, 6, 1, D)
        qa, ka, va, qm, km, vm, om, gc, kt, vt = _in_proj(
            x, m, w["g1"], w["w_main"], w["w_gate"], w["b_gate"], ts, keep)
        att = attn_fn(l, qa, ka, va, kt, vt)
        c0, n0, m0 = state_fn(l, B)
        hm, C, n, mm = _mlstm(qm, km, vm, om, gc, w["g_mlstm"], c0, n0, m0, L)
        if sort_tokens:
            x1, rows, idx, counts = _out_proj(att, hm, x, m, w["g2"], w["w_out"], w["w_router"], w["b_router"], ts,
                                              region_rows=B * S + MOE_TILE)
            x = _moe_sorted(rows, idx, counts, x1, m, w["w_gu4"], w["w_down4"], ts, MOE_TILE)
        else:
            x1, h2, gates = _out_proj(att, hm, x, m, w["g2"], w["w_out"], w["w_router"], w["b_router"], ts)
            x = _moe(h2, gates, x1, m, w["w_gu"], w["w_down_e"], ts)
        states.append((_heads(kt, N_HEADS_A), _heads(vt, N_HEADS_A), C, n[:, :, 0, :], mm[:, :, 0, 0]))
    y = _final(x, mods_f.reshape(B, 2, 1, D), wts[0]["g_final"], ts)
    return y, states


def kernel(x_prompt, x_sample, c_prompt, c_sample, cache_attn_k, cache_attn_v, state_mlstm_C, state_mlstm_n,
           state_mlstm_m, w_ada, b_ada, g_norm, w_in, b_igate, b_fgate, rel_bias, g_mlstm, w_out, w_router_g,
           b_router_g, w_router_e, b_router_e, w_gate, w_up, w_down, w_ada_f, b_ada_f, g_final):
    depth = w_ada.shape[0]
    Bp, Sp, D = x_prompt.shape
    Bs, Ss, _ = x_sample.shape

    c_all = jnp.concatenate([c_prompt, c_sample], axis=0)
    pad = (-c_all.shape[0]) % 16
    c_all = jnp.pad(c_all, ((0, pad), (0, 0)))
    mods_all = _ada(c_all, w_ada, b_ada, 1536)
    mods_fin = _ada(c_all, w_ada_f[None], b_ada_f[None], 1024)[0]

    wts = []
    for l in range(depth):
        wl = w_in[l]
        n_main = N_SLABS * SLAB
        w_gate_cols = jnp.pad(wl[:, n_main:], ((0, 0), (0, LANES - 2 * N_HEADS_M)))
        b_gate_cols = jnp.pad(jnp.concatenate([b_igate[l], b_fgate[l]]), (0, LANES - 2 * N_HEADS_M))
        w_r = jnp.concatenate([w_router_g[l], w_router_e[l]], axis=1)
        b_r = jnp.concatenate([b_router_g[l], b_router_e[l]])
        npad = LANES - N_GROUPS - N_EXPERTS
        w_gu = jnp.concatenate([w_gate[l], w_up[l]], axis=-1).astype(BF16)
        w_dn = w_down[l].astype(BF16)
        wts.append(dict(
            g1=g_norm[l, 0][None], g2=g_norm[l, 1][None], g_final=g_final[None],
            w_main=wl[:, :n_main].astype(BF16), w_gate=w_gate_cols.astype(BF16), b_gate=b_gate_cols[None],
            g_mlstm=g_mlstm[l][None], w_out=w_out[l].astype(BF16),
            w_router=jnp.pad(w_r, ((0, 0), (0, npad))).astype(BF16), b_router=jnp.pad(b_r, (0, npad))[None],
            w_gu=w_gu, w_down_e=w_dn,
            w_gu4=w_gu.reshape(N_GROUPS, EXP_PER_GROUP, D_MODEL, 2 * D_EXPERT),
            w_down4=w_dn.reshape(N_GROUPS, EXP_PER_GROUP, D_EXPERT, D_MODEL)))

    bias_tabs = [_band_bias_table(rel_bias[l]) for l in range(depth)]

    def attn_p(l, qa, ka, va, kt, vt):
        return _attn_prompt(qa, ka, va, bias_tabs[l])

    def state_p(l, B):
        return (jnp.zeros((B, N_HEADS_M, HD_M, HD_M), F32), jnp.zeros((B, N_HEADS_M, 1, HD_M), F32),
                jnp.zeros((B, N_HEADS_M, 1, HD_M), F32))

    keep_p = min(BAND_PAST, Sp)
    y_p, sp = _trunk(x_prompt, mods_all[:, :Bp], mods_fin[:Bp], wts, attn_p, state_p,
                     ts=512, keep=keep_p, L=256, sort_tokens=True)

    Wc = cache_attn_k.shape[3]
    tq = jnp.arange(Ss)[:, None]
    dist_c = Wc + tq - jnp.arange(Wc)[None, :]
    dist_n = tq - jnp.arange(Ss)[None, :]
    lookup = lambda l, dist: rel_bias[l][:, jnp.clip(dist, -MAX_REL, MAX_REL) + MAX_REL].astype(F32)

    def attn_s(l, qa, ka, va, kt, vt):
        o = _attn_sample(_heads(qa.astype(F32), N_HEADS_A), _heads(kt, N_HEADS_A), _heads(vt, N_HEADS_A),
                         cache_attn_k[l], cache_attn_v[l], lookup(l, dist_c), lookup(l, dist_n))
        return o.transpose(0, 2, 1, 3).reshape(Bs, Ss, D_ATT).astype(BF16)

    def state_s(l, B):
        return (state_mlstm_C[l], state_mlstm_n[l][:, :, None, :],
                jnp.broadcast_to(state_mlstm_m[l][:, :, None, None], (B, N_HEADS_M, 1, HD_M)))

    y_s, ss = _trunk(x_sample, mods_all[:, Bp:Bp + Bs], mods_fin[Bp:Bp + Bs], wts, attn_s, state_s,
                     ts=Ss, keep=Ss, L=Ss, sort_tokens=False)

    stack = lambda sts, i: jnp.stack([s[i] for s in sts])
    return (y_p, y_s,
            stack(sp, 0), stack(sp, 1), stack(sp, 2), stack(sp, 3), stack(sp, 4),
            stack(ss, 0), stack(ss, 1), stack(ss, 2), stack(ss, 3), stack(ss, 4))
```

```python
import functools

import jax
import jax.numpy as jnp
from jax import lax
from jax.experimental import pallas as pl
from jax.experimental.pallas import tpu as pltpu

F32 = jnp.float32
BF16 = jnp.bfloat16

D_MODEL = 1024
CHUNK = 64
BAND_CHUNKS = 8
BAND_PAST = BAND_CHUNKS * CHUNK
D_ATT = 512
N_HEADS_A = 8
HD_A = 64
MAX_REL = 128
D_MLSTM = 512
N_HEADS_M = 4
HD_M = 128
N_GROUPS = 4
EXP_PER_GROUP = 4
N_EXPERTS = 16
D_EXPERT = 256
EPS = 1e-6
NEG = -1e30

LANES = 128
SUBLANES = 8
N_SLABS = 7
SLAB = 512
ATTN_HEADS = 4
Q_BLOCK = 256
K_WINDOW = BAND_PAST + Q_BLOCK
ROUTER_E0 = N_GROUPS
PACKED = D_MODEL // 2
GATE_GROUP = PACKED // LANES
DMA_UNROLL = 8
MOE_TILE = 512
VMEM_LIMIT = 56 * 1024 * 1024


def _cparams(sem):
    return pltpu.CompilerParams(dimension_semantics=sem, vmem_limit_bytes=VMEM_LIMIT)


def _ada_kernel(c_ref, w_ref, b_ref, o_ref):
    c = c_ref[...]
    a = (c * jax.nn.sigmoid(c)).astype(BF16)
    o_ref[0] = jnp.dot(a, w_ref[0].astype(BF16), preferred_element_type=F32) + b_ref[0]


def _ada(c, w, b, tn):
    nl, d, n = w.shape
    bc = c.shape[0]
    return pl.pallas_call(
        _ada_kernel,
        out_shape=jax.ShapeDtypeStruct((nl, bc, n), F32),
        grid=(nl, n // tn),
        in_specs=[pl.BlockSpec((bc, d), lambda l, j: (0, 0)),
                  pl.BlockSpec((1, d, tn), lambda l, j: (l, 0, j)),
                  pl.BlockSpec((1, 1, tn), lambda l, j: (l, 0, j))],
        out_specs=pl.BlockSpec((1, bc, tn), lambda l, j: (l, 0, j)),
        compiler_params=_cparams(("parallel", "parallel")),
        name="ada",
    )(c, w, b.reshape(nl, 1, n))


def _norm_mod(x, g, scale, shift):
    r = lax.rsqrt(jnp.mean(x * x, axis=-1, keepdims=True) + EPS)
    return (x * r * g) * (1.0 + scale) + shift


def _inproj_kernel(x_ref, sh_ref, sc_ref, g_ref, w_ref, wg_ref, bg_ref,
                   qa_ref, ka_ref, va_ref, qm_ref, km_ref, vm_ref, om_ref, gc_ref, kt_ref, vt_ref,
                   *, first_tail_tile):
    hb = _norm_mod(x_ref[0], g_ref[...], sc_ref[...], sh_ref[...]).astype(BF16)
    outs = (qa_ref, ka_ref, va_ref, qm_ref, km_ref, vm_ref, om_ref)
    in_tail = pl.program_id(1) >= first_tail_tile
    for j, o_ref in enumerate(outs):
        p = jnp.dot(hb, w_ref[:, j * SLAB:(j + 1) * SLAB], preferred_element_type=F32)
        if j == 4:
            p = p * (HD_M ** -0.5)
        o_ref[0] = p.astype(BF16)
        if j in (1, 2):
            t_ref = kt_ref if j == 1 else vt_ref

            @pl.when(in_tail)
            def _():
                t_ref[0] = p
    pg = jnp.dot(hb, wg_ref[...], preferred_element_type=F32) + bg_ref[...]
    lsig = jnp.minimum(pg, 0.0) - jnp.log1p(jnp.exp(-jnp.abs(pg)))
    lane = lax.broadcasted_iota(jnp.int32, pg.shape, 1)
    gc = jnp.where(lane < N_HEADS_M, pg, lsig)
    gc_ref[0] = gc[:, :2 * N_HEADS_M]


def _in_proj(x, mods, g, w_main, w_gate, b_gate, ts, keep):
    B, S, D = x.shape
    n_tiles = S // ts
    n_tail = keep // ts
    first_tail = n_tiles - n_tail
    slab = jax.ShapeDtypeStruct((B, S, SLAB), BF16)
    tail = jax.ShapeDtypeStruct((B, keep, SLAB), F32)
    slab_spec = pl.BlockSpec((1, ts, SLAB), lambda b, s: (b, s, 0))
    tail_spec = pl.BlockSpec((1, ts, SLAB), lambda b, s: (b, jnp.maximum(s - first_tail, 0), 0))
    return pl.pallas_call(
        functools.partial(_inproj_kernel, first_tail_tile=first_tail),
        out_shape=(slab,) * N_SLABS + (jax.ShapeDtypeStruct((B, S, 2 * N_HEADS_M), F32), tail, tail),
        grid=(B, n_tiles),
        in_specs=[pl.BlockSpec((1, ts, D), lambda b, s: (b, s, 0)),
                  pl.BlockSpec((None, None, 1, D), lambda b, s: (b, 0, 0, 0)),
                  pl.BlockSpec((None, None, 1, D), lambda b, s: (b, 1, 0, 0)),
                  pl.BlockSpec((1, D), lambda b, s: (0, 0)),
                  pl.BlockSpec((D, N_SLABS * SLAB), lambda b, s: (0, 0)),
                  pl.BlockSpec((D, LANES), lambda b, s: (0, 0)),
                  pl.BlockSpec((1, LANES), lambda b, s: (0, 0))],
        out_specs=(slab_spec,) * N_SLABS + (pl.BlockSpec((1, ts, 2 * N_HEADS_M), lambda b, s: (b, s, 0)),
                                            tail_spec, tail_spec),
        compiler_params=_cparams(("parallel", "arbitrary")),
        name="in_proj",
    )(x, mods, mods, g, w_main, w_gate, b_gate)


def _attn_prompt_kernel(q_ref, k_ref, v_ref, bias_ref, o_ref):
    i = pl.program_id(2)
    ws = pl.multiple_of(jnp.maximum(i * Q_BLOCK - BAND_PAST, 0), Q_BLOCK)
    variant = jnp.minimum(i, BAND_PAST // Q_BLOCK)
    q = q_ref[0]
    kw = k_ref[0, pl.ds(ws, K_WINDOW), :]
    vw = v_ref[0, pl.ds(ws, K_WINDOW), :]
    lane = lax.broadcasted_iota(jnp.int32, q.shape, 1)
    out = None
    for hh in range(ATTN_HEADS):
        in_head = (lane >= hh * HD_A) & (lane < (hh + 1) * HD_A)
        qh = jnp.where(in_head, q * (HD_A ** -0.5), 0.0).astype(BF16)
        s = lax.dot_general(qh, kw, (((1,), (1,)), ((), ())), preferred_element_type=F32)
        s = s + bias_ref[variant, hh]
        mx = jnp.max(s, axis=-1, keepdims=True)
        p = jnp.exp(s - mx)
        l = jnp.sum(p, axis=-1, keepdims=True)
        o = jnp.dot(p.astype(BF16), vw, preferred_element_type=F32) / l
        out = o if out is None else jnp.where(in_head, o, out)
    o_ref[0] = out.astype(BF16)


def _band_bias_table(table):
    H = table.shape[0]
    a = jnp.arange(Q_BLOCK)[:, None]
    j = jnp.arange(K_WINDOW)[None, :]
    period = Q_BLOCK + K_WINDOW
    u = jnp.arange(period)
    v = jnp.where(u < K_WINDOW, u, u - period)
    tabs = []
    for off in range(0, BAND_PAST + 1, Q_BLOCK):
        prof = table[:, jnp.clip(off - v, -MAX_REL, MAX_REL) + MAX_REL].astype(F32)
        skew = jnp.tile(prof, (1, Q_BLOCK))[:, :Q_BLOCK * (period - 1)].reshape(H, Q_BLOCK, period - 1)
        bias = skew[:, :, :K_WINDOW]
        qc = (a + off) // CHUNK
        kc = j // CHUNK
        valid = (kc <= qc) & (kc >= qc - BAND_CHUNKS)
        tabs.append(jnp.where(valid[None], bias, NEG))
    return jnp.stack(tabs)


def _attn_prompt(q, k, v, bias_tab):
    B, S, _ = q.shape
    nv = bias_tab.shape[0]
    width = ATTN_HEADS * HD_A
    return pl.pallas_call(
        _attn_prompt_kernel,
        out_shape=jax.ShapeDtypeStruct((B, S, D_ATT), BF16),
        grid=(N_HEADS_A // ATTN_HEADS, B, S // Q_BLOCK),
        in_specs=[pl.BlockSpec((1, Q_BLOCK, width), lambda hp, b, i: (b, i, hp)),
                  pl.BlockSpec((1, S, width), lambda hp, b, i: (b, 0, hp)),
                  pl.BlockSpec((1, S, width), lambda hp, b, i: (b, 0, hp)),
                  pl.BlockSpec((nv, ATTN_HEADS, Q_BLOCK, K_WINDOW), lambda hp, b, i: (0, hp, 0, 0))],
        out_specs=pl.BlockSpec((1, Q_BLOCK, width), lambda hp, b, i: (b, i, hp)),
        compiler_params=_cparams(("parallel", "parallel", "arbitrary")),
        name="attn_prompt",
    )(q, k, v, bias_tab)


def _attn_sample_kernel(q_ref, kn_ref, vn_ref, kc_ref, vc_ref, bc_ref, bn_ref, o_ref):
    for h in range(N_HEADS_A):
        q = (q_ref[0, h] * (HD_A ** -0.5)).astype(BF16)
        kn = kn_ref[0, h].astype(BF16)
        vn = vn_ref[0, h].astype(BF16)
        kc = kc_ref[0, h].astype(BF16)
        vc = vc_ref[0, h].astype(BF16)
        nt = (((1,), (1,)), ((), ()))
        sc = lax.dot_general(q, kc, nt, preferred_element_type=F32) + bc_ref[h]
        sn = lax.dot_general(q, kn, nt, preferred_element_type=F32) + bn_ref[h]
        mx = jnp.maximum(jnp.max(sc, axis=-1, keepdims=True), jnp.max(sn, axis=-1, keepdims=True))
        pc = jnp.exp(sc - mx)
        pn = jnp.exp(sn - mx)
        l = jnp.sum(pc, axis=-1, keepdims=True) + jnp.sum(pn, axis=-1, keepdims=True)
        o = (jnp.dot(pc.astype(BF16), vc, preferred_element_type=F32)
             + jnp.dot(pn.astype(BF16), vn, preferred_element_type=F32))
        o_ref[0, h] = o / l


def _attn_sample(q, kn, vn, kc, vc, bias_c, bias_n):
    B, H, T, d = q.shape
    W = kc.shape[2]
    new = pl.BlockSpec((1, H, T, d), lambda b: (b, 0, 0, 0))
    old = pl.BlockSpec((1, H, W, d), lambda b: (b, 0, 0, 0))
    return pl.pallas_call(
        _attn_sample_kernel,
        out_shape=jax.ShapeDtypeStruct((B, H, T, d), F32),
        grid=(B,),
        in_specs=[new, new, new, old, old,
                  pl.BlockSpec((H, T, W), lambda b: (0, 0, 0)),
                  pl.BlockSpec((H, T, T), lambda b: (0, 0, 0))],
        out_specs=new,
        compiler_params=_cparams(("parallel",)),
        name="attn_sample",
    )(q, kn, vn, kc, vc, bias_c, bias_n)


def _mlstm_kernel(q_ref, k_ref, v_ref, om_ref, gc_ref, gml_ref, c0_ref, n0_ref, m0_ref,
                  h_ref, c_ref, n_ref, m_ref, *, L, nc):
    c_ref[0] = c0_ref[0]
    n_ref[0] = n0_ref[0]
    m_ref[0] = m0_ref[0]
    tt = lax.broadcasted_iota(jnp.int32, (L, L), 0)
    ss = lax.broadcasted_iota(jnp.int32, (L, L), 1)
    causal = ss <= tt
    eye = ss == tt
    nt = (((1,), (1,)), ((), ()))
    tn = (((0,), (0,)), ((), ()))

    def chunk(ci, carry):
        off = pl.multiple_of(ci * L, L)
        g = gc_ref[0, pl.ds(off, L), :]
        for hh in range(N_HEADS_M):
            cols = slice(hh * HD_M, (hh + 1) * HD_M)
            ig_c = g[:, hh:hh + 1]
            lf_c = g[:, N_HEADS_M + hh:N_HEADS_M + hh + 1]
            ig_r = jnp.sum(jnp.where(eye, ig_c, 0.0), axis=0, keepdims=True)
            lf_r = jnp.sum(jnp.where(eye, lf_c, 0.0), axis=0, keepdims=True)
            b_r = jnp.sum(jnp.where(tt <= ss, lf_c, 0.0), axis=0, keepdims=True)
            b_c = jnp.sum(jnp.where(causal, lf_r, 0.0), axis=1, keepdims=True)
            a_r = ig_r - b_r
            a_c = ig_c - b_c
            m_prev = m_ref[0, hh][:, 0:1]
            Mt = jnp.maximum(jnp.max(jnp.where(causal, a_r, -jnp.inf), axis=1, keepdims=True), m_prev)
            P = jnp.where(causal, jnp.exp(a_r - Mt), 0.0)
            decay = jnp.exp(m_prev - Mt)
            q = q_ref[0, pl.ds(off, L), cols]
            k = k_ref[0, pl.ds(off, L), cols]
            v = v_ref[0, pl.ds(off, L), cols]
            C = c_ref[0, hh]
            n = n_ref[0, hh]
            S = P * lax.dot_general(q, k, nt, preferred_element_type=F32)
            num = decay * jnp.dot(q, C.astype(BF16), preferred_element_type=F32) \
                + jnp.dot(S.astype(BF16), v, preferred_element_type=F32)
            den = decay * jnp.sum(q.astype(F32) * n, axis=1, keepdims=True) + jnp.sum(S, axis=1, keepdims=True)
            hv = num / jnp.maximum(jnp.abs(den), jnp.exp(-(b_c + Mt)))
            hn = hv * lax.rsqrt(jnp.mean(hv * hv, axis=-1, keepdims=True) + EPS) * gml_ref[:, cols]
            om = om_ref[0, pl.ds(off, L), cols].astype(F32)
            h_ref[0, pl.ds(off, L), cols] = (hn * jax.nn.sigmoid(om)).astype(BF16)
            M_last = Mt[L - 1:L, :]
            b_last = b_c[L - 1:L, :]
            kw = k.astype(F32) * jnp.exp(a_c - M_last)
            cd = jnp.exp(m_prev - M_last)
            c_ref[0, hh] = cd * C + lax.dot_general(kw.astype(BF16), v, tn, preferred_element_type=F32)
            n_ref[0, hh] = cd * n + jnp.sum(kw, axis=0, keepdims=True)
            m_ref[0, hh] = jnp.broadcast_to(b_last + M_last, (1, HD_M))
        return carry

    lax.fori_loop(0, nc, chunk, 0)


def _mlstm(q, k, v, om, gc, gml, c0, n0, m0, L):
    B, S, _ = q.shape
    H = N_HEADS_M
    seq = pl.BlockSpec((1, S, D_MLSTM), lambda b: (b, 0, 0))
    cst = pl.BlockSpec((1, H, HD_M, HD_M), lambda b: (b, 0, 0, 0))
    vec = pl.BlockSpec((1, H, 1, HD_M), lambda b: (b, 0, 0, 0))
    return pl.pallas_call(
        functools.partial(_mlstm_kernel, L=L, nc=S // L),
        out_shape=(jax.ShapeDtypeStruct((B, S, D_MLSTM), BF16),
                   jax.ShapeDtypeStruct((B, H, HD_M, HD_M), F32),
                   jax.ShapeDtypeStruct((B, H, 1, HD_M), F32),
                   jax.ShapeDtypeStruct((B, H, 1, HD_M), F32)),
        grid=(B,),
        in_specs=[seq, seq, seq, seq,
                  pl.BlockSpec((1, S, 2 * H), lambda b: (b, 0, 0)),
                  pl.BlockSpec((1, D_MLSTM), lambda b: (0, 0)),
                  cst, vec, vec],
        out_specs=(seq, cst, vec, vec),
        compiler_params=_cparams(("parallel",)),
        name="mlstm",
    )(q, k, v, om, gc, gml, c0, n0, m0)


def _mix_residual_norm(att_ref, hm_ref, x_ref, gt_ref, sh_ref, sc_ref, g_ref, wo_ref):
    mix = (jnp.dot(att_ref[0], wo_ref[:D_ATT, :], preferred_element_type=F32)
           + jnp.dot(hm_ref[0], wo_ref[D_ATT:, :], preferred_element_type=F32))
    x1 = x_ref[0] + gt_ref[...] * mix
    return x1, _norm_mod(x1, g_ref[...], sc_ref[...], sh_ref[...]).astype(BF16)


def _router(hb, wr_ref, br_ref):
    logit = jnp.dot(hb, wr_ref[...], preferred_element_type=F32) + br_ref[...]
    lane = lax.broadcasted_iota(jnp.int32, logit.shape, 1)
    big = jnp.int32(LANES)
    is_g = lane < N_GROUPS
    mg = jnp.max(jnp.where(is_g, logit, -jnp.inf), axis=-1, keepdims=True)
    eg = jnp.where(is_g, jnp.exp(logit - mg), 0.0)
    pg = eg / jnp.sum(eg, axis=-1, keepdims=True)
    p_top = jnp.max(pg, axis=-1, keepdims=True)
    g_idx = jnp.min(jnp.where(is_g & (pg == p_top), lane, big), axis=-1, keepdims=True)
    e_lo = ROUTER_E0 + g_idx * EXP_PER_GROUP
    sel = (lane >= e_lo) & (lane < e_lo + EXP_PER_GROUP)
    me = jnp.max(jnp.where(sel, logit, -jnp.inf), axis=-1, keepdims=True)
    ee = jnp.where(sel, jnp.exp(logit - me), 0.0)
    pe = ee / jnp.sum(ee, axis=-1, keepdims=True)
    v1 = jnp.max(jnp.where(sel, pe, -1.0), axis=-1, keepdims=True)
    i1 = jnp.min(jnp.where(sel & (pe == v1), lane, big), axis=-1, keepdims=True)
    sel2 = sel & (lane != i1)
    v2 = jnp.max(jnp.where(sel2, pe, -1.0), axis=-1, keepdims=True)
    i2 = jnp.min(jnp.where(sel2 & (pe == v2), lane, big), axis=-1, keepdims=True)
    tot = v1 + v2
    w1 = v1 / tot * p_top
    w2 = v2 / tot * p_top
    return jnp.where(lane == i1, w1, jnp.where(lane == i2, w2, 0.0)), g_idx


def _outproj_kernel(att_ref, hm_ref, x_ref, gt_ref, sh_ref, sc_ref, g_ref, wo_ref, wr_ref, br_ref,
                    x1_ref, h2_ref, gates_ref):
    x1, hb = _mix_residual_norm(att_ref, hm_ref, x_ref, gt_ref, sh_ref, sc_ref, g_ref, wo_ref)
    x1_ref[0] = x1
    h2_ref[0] = hb
    gates_ref[0] = _router(hb, wr_ref, br_ref)[0]


def _pack_bf16_pairs(lo, hi):
    lo_bits = lax.bitcast_convert_type(lo.astype(BF16).astype(F32), jnp.uint32)
    hi_bits = lax.bitcast_convert_type(hi.astype(BF16).astype(F32), jnp.uint32)
    return (hi_bits & jnp.uint32(0xFFFF0000)) | (lo_bits >> 16)


def _unpack_bf16_pairs(packed):
    lo = lax.bitcast_convert_type(packed << 16, F32)
    hi = lax.bitcast_convert_type(packed & jnp.uint32(0xFFFF0000), F32)
    return lo, hi


def _outproj_sort_kernel(att_ref, hm_ref, x_ref, gt_ref, sh_ref, sc_ref, g_ref, wo_ref, wr_ref, br_ref,
                         x1_ref, row_ref, idx_ref, cnt_ref, fill_ref, *, region_rows):
    first = (pl.program_id(0) == 0) & (pl.program_id(1) == 0)

    @pl.when(first)
    def _():
        fill_ref[...] = jnp.zeros_like(fill_ref)

    x1, hb = _mix_residual_norm(att_ref, hm_ref, x_ref, gt_ref, sh_ref, sc_ref, g_ref, wo_ref)
    x1_ref[0] = x1
    gates, g_idx = _router(hb, wr_ref, br_ref)
    hf = hb.astype(F32)
    half = hf.shape[1] // 2
    ts = gates.shape[0]
    packed = _pack_bf16_pairs(hf[:, :half], hf[:, half:])
    for k in range(SUBLANES):
        if k < GATE_GROUP:
            piece = packed[:, k * LANES:(k + 1) * LANES]
        elif k == GATE_GROUP:
            piece = lax.bitcast_convert_type(gates, jnp.uint32)
        else:
            piece = jnp.zeros((ts, LANES), jnp.uint32)
        row_ref[_lane_group(k, ts), :] = piece
    lane = lax.broadcasted_iota(jnp.int32, gates.shape, 1)
    onehot = lane == g_idx
    tt = lax.broadcasted_iota(jnp.int32, (ts, ts), 0)
    ss = lax.broadcasted_iota(jnp.int32, (ts, ts), 1)
    before = jnp.where(ss < tt, 1.0, 0.0).astype(BF16)
    rank = jnp.dot(before, jnp.where(onehot, 1.0, 0.0).astype(BF16), preferred_element_type=F32)
    fill = fill_ref[...]
    slot = jnp.sum(jnp.where(onehot, rank + fill, 0.0), axis=-1, keepdims=True)
    slot = slot + g_idx.astype(F32) * float(region_rows)
    slot_row = jnp.sum(jnp.where(ss == tt, slot, 0.0), axis=0, keepdims=True)
    idx_ref[...] = slot_row.astype(jnp.int32)
    fill = fill + jnp.sum(jnp.where(onehot, 1.0, 0.0), axis=0, keepdims=True)
    fill_ref[...] = fill
    cnt_ref[...] = fill.astype(jnp.int32)


def _out_proj(att, hm, x, mods, g, w_out, w_router, b_router, ts, region_rows=None):
    B, S, D = x.shape
    nt = S // ts
    tile = lambda w: pl.BlockSpec((1, ts, w), lambda b, s: (b, s, 0))
    mod = lambda i: pl.BlockSpec((None, None, 1, D), lambda b, s: (b, i, 0, 0))
    const = lambda shape: pl.BlockSpec(shape, lambda b, s: (0, 0))
    in_specs = [tile(D_ATT), tile(D_MLSTM), tile(D), mod(2), mod(3), mod(4), const((1, D)),
                const((D, D)), const((D, LANES)), const((1, LANES))]
    args = (att, hm, x, mods, mods, mods, g, w_out, w_router, b_router)
    if region_rows is None:
        return pl.pallas_call(
            _outproj_kernel,
            out_shape=(jax.ShapeDtypeStruct((B, S, D), F32),
                       jax.ShapeDtypeStruct((B, S, D), BF16),
                       jax.ShapeDtypeStruct((B, S, LANES), F32)),
            grid=(B, nt), in_specs=in_specs,
            out_specs=(tile(D), tile(D), tile(LANES)),
            compiler_params=_cparams(("parallel", "parallel")),
            name="out_proj",
        )(*args)
    return pl.pallas_call(
        functools.partial(_outproj_sort_kernel, region_rows=region_rows),
        out_shape=(jax.ShapeDtypeStruct((B, S, D), F32),
                   jax.ShapeDtypeStruct((B, S * SUBLANES, LANES), jnp.uint32),
                   jax.ShapeDtypeStruct((B * nt, 1, ts), jnp.int32),
                   jax.ShapeDtypeStruct((1, LANES), jnp.int32)),
        grid=(B, nt), in_specs=in_specs,
        out_specs=(tile(D), pl.BlockSpec((None, ts * SUBLANES, LANES), lambda b, s: (b, s, 0)),
                   pl.BlockSpec((None, 1, ts), lambda b, s: (b * nt + s, 0, 0)),
                   const((1, LANES))),
        scratch_shapes=[pltpu.VMEM((1, LANES), F32)],
        compiler_params=_cparams(("arbitrary", "arbitrary")),
        name="out_proj_sort",
    )(*args)


def _moe_kernel(h_ref, gates_ref, x_ref, gt_ref, wgu_ref, wd_ref, o_ref, acc_ref):
    e = pl.program_id(2)

    @pl.when(e == 0)
    def _():
        acc_ref[...] = jnp.zeros_like(acc_ref)

    h = h_ref[0]
    au = jnp.dot(h, wgu_ref[0], preferred_element_type=F32)
    a = au[:, :D_EXPERT]
    u = au[:, D_EXPERT:]
    act = (a * jax.nn.sigmoid(a) * u).astype(BF16)
    y = jnp.dot(act, wd_ref[0], preferred_element_type=F32)
    gates = gates_ref[0]
    lane = lax.broadcasted_iota(jnp.int32, gates.shape, 1)
    ge = jnp.sum(jnp.where(lane == ROUTER_E0 + e, gates, 0.0), axis=-1, keepdims=True)
    acc_ref[...] += ge * y

    @pl.when(e == N_EXPERTS - 1)
    def _():
        o_ref[0] = x_ref[0] + gt_ref[...] * acc_ref[...]


def _moe(h2, gates, x1, mods, w_gu, w_down, ts):
    B, S, D = x1.shape
    tile = lambda w: pl.BlockSpec((1, ts, w), lambda b, s, e: (b, s, 0))
    return pl.pallas_call(
        _moe_kernel,
        out_shape=jax.ShapeDtypeStruct((B, S, D), F32),
        grid=(B, S // ts, N_EXPERTS),
        in_specs=[tile(D), tile(LANES), tile(D),
                  pl.BlockSpec((None, None, 1, D), lambda b, s, e: (b, 5, 0, 0)),
                  pl.BlockSpec((1, D, 2 * D_EXPERT), lambda b, s, e: (e, 0, 0)),
                  pl.BlockSpec((1, D_EXPERT, D), lambda b, s, e: (e, 0, 0))],
        out_specs=tile(D),
        scratch_shapes=[pltpu.VMEM((ts, D), F32)],
        compiler_params=_cparams(("parallel", "parallel", "arbitrary")),
        name="moe",
    )(h2, gates, x1, mods, w_gu, w_down)


def _token_rows(token):
    return pl.ds(pl.multiple_of(token * SUBLANES, SUBLANES), SUBLANES)


def _lane_group(k, n, first_token=0):
    return pl.ds(first_token * SUBLANES + k, n, stride=SUBLANES)


def _for_each_token(ts, fn):
    def body(j, c):
        for u in range(DMA_UNROLL):
            fn(j * DMA_UNROLL + u, u % 2)
        return c
    lax.fori_loop(0, ts // DMA_UNROLL, body, 0)


def _dispatch_kernel(cnt_ref, row_ref, idx_ref, hs_ref, buf_ref, zero_ref, sem_ref, zsem_ref, *, region_rows, tm):
    i = pl.program_id(0)
    n = pl.num_programs(0)
    ts = idx_ref.shape[1]
    slot = i % 2

    def wait_slot(s):
        pltpu.make_async_copy(buf_ref.at[s], hs_ref.at[pl.ds(0, ts * SUBLANES), :], sem_ref.at[s]).wait()

    @pl.when(i >= 2)
    def _():
        wait_slot(slot)

    buf_ref[slot] = row_ref[...]

    def issue(r, priority):
        pltpu.make_async_copy(buf_ref.at[slot, _token_rows(r), :], hs_ref.at[_token_rows(idx_ref[0, r]), :],
                              sem_ref.at[slot]).start(priority=priority)
    _for_each_token(ts, issue)

    @pl.when(i == n - 1)
    def _():
        wait_slot(slot)

        @pl.when(n >= 2)
        def _():
            wait_slot(1 - slot)

        zero_ref[...] = jnp.zeros_like(zero_ref)

        def tail_copy(g):
            first = g * region_rows + cnt_ref[g]
            dst = hs_ref.at[pl.ds(pl.multiple_of(first * SUBLANES, SUBLANES), tm * SUBLANES), :]
            return pltpu.make_async_copy(zero_ref, dst, zsem_ref.at[g])

        for g in range(N_GROUPS):
            tail_copy(g).start()
        for g in range(N_GROUPS):
            tail_copy(g).wait()


def _dispatch(rows, idx, counts, ts, tm, region_rows):
    T = rows.shape[0] // SUBLANES
    grid_spec = pltpu.PrefetchScalarGridSpec(
        num_scalar_prefetch=1,
        grid=(T // ts,),
        in_specs=[pl.BlockSpec((ts * SUBLANES, LANES), lambda i, cnt: (i, 0)),
                  pl.BlockSpec((None, 1, ts), lambda i, cnt: (i, 0, 0), memory_space=pltpu.SMEM)],
        out_specs=pl.BlockSpec(memory_space=pl.ANY),
        scratch_shapes=[pltpu.VMEM((2, ts * SUBLANES, LANES), jnp.uint32),
                        pltpu.VMEM((tm * SUBLANES, LANES), jnp.uint32),
                        pltpu.SemaphoreType.DMA((2,)),
                        pltpu.SemaphoreType.DMA((N_GROUPS,))])
    return pl.pallas_call(
        functools.partial(_dispatch_kernel, region_rows=region_rows, tm=tm),
        out_shape=jax.ShapeDtypeStruct((N_GROUPS * region_rows * SUBLANES, LANES), jnp.uint32),
        grid_spec=grid_spec,
        compiler_params=_cparams(("arbitrary",)),
        name="moe_dispatch",
    )(counts, rows, idx)


def _moe_group_kernel(blk_ref, grp_ref, live_ref, hs_ref, wgu_ref, wd_ref, o_ref):
    i = pl.program_id(0)

    @pl.when(live_ref[i] == 0)
    def _():
        o_ref[...] = jnp.zeros_like(o_ref)

    @pl.when(live_ref[i] != 0)
    def _():
        tm = hs_ref.shape[0] // SUBLANES
        halves = [_unpack_bf16_pairs(hs_ref[_lane_group(k, tm), :]) for k in range(PACKED // LANES)]
        lo = jnp.concatenate([h[0] for h in halves], axis=1).astype(BF16)
        hi = jnp.concatenate([h[1] for h in halves], axis=1).astype(BF16)
        gates = lax.bitcast_convert_type(hs_ref[_lane_group(GATE_GROUP, tm), :], F32)
        lane = lax.broadcasted_iota(jnp.int32, gates.shape, 1)
        e0 = ROUTER_E0 + grp_ref[i] * EXP_PER_GROUP
        y = jnp.zeros((tm, D_MODEL), F32)
        for e in range(EXP_PER_GROUP):
            au = (jnp.dot(lo, wgu_ref[e, :PACKED, :], preferred_element_type=F32)
                  + jnp.dot(hi, wgu_ref[e, PACKED:, :], preferred_element_type=F32))
            a = au[:, :D_EXPERT]
            u = au[:, D_EXPERT:]
            act = (a * jax.nn.sigmoid(a) * u).astype(BF16)
            ge = jnp.sum(jnp.where(lane == e0 + e, gates, 0.0), axis=-1, keepdims=True)
            y = y + ge * jnp.dot(act, wd_ref[e], preferred_element_type=F32)
        for k in range(D_MODEL // LANES):
            o_ref[_lane_group(k, tm), :] = y[:, k * LANES:(k + 1) * LANES]


def _moe_group(hs, blk, grp, live, w_gu, w_down, tm):
    n_steps = blk.shape[0]
    rows = tm * SUBLANES
    n_blocks = hs.shape[0] // rows
    grid_spec = pltpu.PrefetchScalarGridSpec(
        num_scalar_prefetch=3,
        grid=(n_steps,),
        in_specs=[pl.BlockSpec((rows, LANES), lambda i, blk, grp, live: (blk[i], 0)),
                  pl.BlockSpec((None, EXP_PER_GROUP, D_MODEL, 2 * D_EXPERT), lambda i, blk, grp, live: (grp[i], 0, 0, 0)),
                  pl.BlockSpec((None, EXP_PER_GROUP, D_EXPERT, D_MODEL), lambda i, blk, grp, live: (grp[i], 0, 0, 0))],
        out_specs=pl.BlockSpec((rows, LANES),
                               lambda i, blk, grp, live: (jnp.where(live[i] != 0, blk[i], n_blocks), 0)))
    return pl.pallas_call(
        _moe_group_kernel,
        out_shape=jax.ShapeDtypeStruct(((n_blocks + 1) * rows, LANES), F32),
        grid_spec=grid_spec,
        compiler_params=_cparams(("arbitrary",)),
        name="moe_group",
    )(blk, grp, live, hs, w_gu, w_down)


def _combine_kernel(x1_ref, gt_ref, idx0_ref, idx1_ref, ys_ref, o_ref, buf_ref, sem_ref):
    i = pl.program_id(0)
    n = pl.num_programs(0)
    ts = idx0_ref.shape[1]
    slot = i % 2

    def issue(idx_ref, s):
        def one(r, priority):
            pltpu.make_async_copy(ys_ref.at[_token_rows(idx_ref[0, r]), :], buf_ref.at[s, _token_rows(r), :],
                                  sem_ref.at[s]).start(priority=priority)
        _for_each_token(ts, one)

    @pl.when(i == 0)
    def _():
        issue(idx0_ref, 0)

    @pl.when(i + 1 < n)
    def _():
        issue(idx1_ref, 1 - slot)

    pltpu.make_async_copy(ys_ref.at[pl.ds(0, ts * SUBLANES), :], buf_ref.at[slot], sem_ref.at[slot]).wait()

    g = gt_ref[...]
    for k in range(D_MODEL // LANES):
        cols = slice(k * LANES, (k + 1) * LANES)
        y = buf_ref[slot, _lane_group(k, ts), :]
        o_ref[:, cols] = x1_ref[:, cols] + g[:, cols] * y


def _combine(x1, mods, idx, ys, ts):
    B, S, D = x1.shape
    nt = S // ts
    n = B * nt
    out = pl.pallas_call(
        _combine_kernel,
        out_shape=jax.ShapeDtypeStruct((B * S, D), F32),
        grid=(n,),
        in_specs=[pl.BlockSpec((ts, D), lambda i: (i, 0)),
                  pl.BlockSpec((None, None, 1, D), lambda i: (i // nt, 5, 0, 0)),
                  pl.BlockSpec((None, 1, ts), lambda i: (i, 0, 0), memory_space=pltpu.SMEM),
                  pl.BlockSpec((None, 1, ts), lambda i: (jnp.minimum(i + 1, n - 1), 0, 0), memory_space=pltpu.SMEM),
                  pl.BlockSpec(memory_space=pl.ANY)],
        out_specs=pl.BlockSpec((ts, D), lambda i: (i, 0)),
        scratch_shapes=[pltpu.VMEM((2, ts * SUBLANES, LANES), F32), pltpu.SemaphoreType.DMA((2,))],
        compiler_params=_cparams(("arbitrary",)),
        name="moe_combine",
    )(x1.reshape(B * S, D), mods, idx, idx, ys)
    return out.reshape(B, S, D)


def _tile_plan(counts, tm, n_steps, region_blocks):
    tiles = (counts + tm - 1) // tm
    ends = jnp.cumsum(tiles)
    total = ends[-1]
    step = jnp.minimum(jnp.arange(n_steps, dtype=jnp.int32), jnp.maximum(total - 1, 0))
    grp = jnp.sum((step[:, None] >= ends[None, :]).astype(jnp.int32), axis=1)
    grp = jnp.minimum(grp, N_GROUPS - 1)
    starts = ends - tiles
    blk = grp * region_blocks + (step - starts[grp])
    live = (jnp.arange(n_steps, dtype=jnp.int32) < total).astype(jnp.int32)
    return blk.astype(jnp.int32), grp.astype(jnp.int32), live


def _moe_sorted(rows, idx, counts, x1, mods, w_gu, w_down, ts, tm):
    B, S, D = x1.shape
    T = B * S
    region_rows = T + tm
    n_steps = T // tm + N_GROUPS
    cnt = counts[0, :N_GROUPS]
    hs = _dispatch(rows.reshape(T * SUBLANES, LANES), idx, cnt, ts, tm, region_rows)
    blk, grp, live = _tile_plan(cnt, tm, n_steps, region_rows // tm)
    ys = _moe_group(hs, blk, grp, live, w_gu, w_down, tm)
    return _combine(x1, mods, idx, ys, ts)


def _final_kernel(x_ref, sh_ref, sc_ref, g_ref, o_ref):
    o_ref[0] = _norm_mod(x_ref[0], g_ref[...], sc_ref[...], sh_ref[...])


def _final(x, mods_f, g, ts):
    B, S, D = x.shape
    return pl.pallas_call(
        _final_kernel,
        out_shape=jax.ShapeDtypeStruct((B, S, D), F32),
        grid=(B, S // ts),
        in_specs=[pl.BlockSpec((1, ts, D), lambda b, s: (b, s, 0)),
                  pl.BlockSpec((None, None, 1, D), lambda b, s: (b, 0, 0, 0)),
                  pl.BlockSpec((None, None, 1, D), lambda b, s: (b, 1, 0, 0)),
                  pl.BlockSpec((1, D), lambda b, s: (0, 0))],
        out_specs=pl.BlockSpec((1, ts, D), lambda b, s: (b, s, 0)),
        compiler_params=_cparams(("parallel", "parallel")),
        name="final_norm",
    )(x, mods_f, mods_f, g)


def _heads(t, n):
    B, S, _ = t.shape
    return t.reshape(B, S, n, -1).transpose(0, 2, 1, 3)


def _trunk(x, mods, mods_f, wts, attn_fn, state_fn, ts, keep, L, sort_tokens):
    B, S, D = x.shape
    states = []
    for l in range(mods.shape[0]):
        w = wts[l]
        m = mods[l].reshape(B, 6, 1, D)
        qa, ka, va, qm, km, vm, om, gc, kt, vt = _in_proj(
            x, m, w["g1"], w["w_main"], w["w_gate"], w["b_gate"], ts, keep)
        att = attn_fn(l, qa, ka, va, kt, vt)
        c0, n0, m0 = state_fn(l, B)
        hm, C, n, mm = _mlstm(qm, km, vm, om, gc, w["g_mlstm"], c0, n0, m0, L)
        if sort_tokens:
            x1, rows, idx, counts = _out_proj(att, hm, x, m, w["g2"], w["w_out"], w["w_router"], w["b_router"], ts,
                                              region_rows=B * S + MOE_TILE)
            x = _moe_sorted(rows, idx, counts, x1, m, w["w_gu4"], w["w_down4"], ts, MOE_TILE)
        else:
            x1, h2, gates = _out_proj(att, hm, x, m, w["g2"], w["w_out"], w["w_router"], w["b_router"], ts)
            x = _moe(h2, gates, x1, m, w["w_gu"], w["w_down_e"], ts)
        states.append((_heads(kt, N_HEADS_A), _heads(vt, N_HEADS_A), C, n[:, :, 0, :], mm[:, :, 0, 0]))
    y = _final(x, mods_f.reshape(B, 2, 1, D), wts[0]["g_final"], ts)
    return y, states


def kernel(x_prompt, x_sample, c_prompt, c_sample, cache_attn_k, cache_attn_v, state_mlstm_C, state_mlstm_n,
           state_mlstm_m, w_ada, b_ada, g_norm, w_in, b_igate, b_fgate, rel_bias, g_mlstm, w_out, w_router_g,
           b_router_g, w_router_e, b_router_e, w_gate, w_up, w_down, w_ada_f, b_ada_f, g_final):
    depth = w_ada.shape[0]
    Bp, Sp, D = x_prompt.shape
    Bs, Ss, _ = x_sample.shape

    c_all = jnp.concatenate([c_prompt, c_sample], axis=0)
    pad = (-c_all.shape[0]) % 16
    c_all = jnp.pad(c_all, ((0, pad), (0, 0)))
    mods_all = _ada(c_all, w_ada, b_ada, 1536)
    mods_fin = _ada(c_all, w_ada_f[None], b_ada_f[None], 1024)[0]

    wts = []
    for l in range(depth):
        wl = w_in[l]
        n_main = N_SLABS * SLAB
        w_gate_cols = jnp.pad(wl[:, n_main:], ((0, 0), (0, LANES - 2 * N_HEADS_M)))
        b_gate_cols = jnp.pad(jnp.concatenate([b_igate[l], b_fgate[l]]), (0, LANES - 2 * N_HEADS_M))
        w_r = jnp.concatenate([w_router_g[l], w_router_e[l]], axis=1)
        b_r = jnp.concatenate([b_router_g[l], b_router_e[l]])
        npad = LANES - N_GROUPS - N_EXPERTS
        w_gu = jnp.concatenate([w_gate[l], w_up[l]], axis=-1).astype(BF16)
        w_dn = w_down[l].astype(BF16)
        wts.append(dict(
            g1=g_norm[l, 0][None], g2=g_norm[l, 1][None], g_final=g_final[None],
            w_main=wl[:, :n_main].astype(BF16), w_gate=w_gate_cols.astype(BF16), b_gate=b_gate_cols[None],
            g_mlstm=g_mlstm[l][None], w_out=w_out[l].astype(BF16),
            w_router=jnp.pad(w_r, ((0, 0), (0, npad))).astype(BF16), b_router=jnp.pad(b_r, (0, npad))[None],
            w_gu=w_gu, w_down_e=w_dn,
            w_gu4=w_gu.reshape(N_GROUPS, EXP_PER_GROUP, D_MODEL, 2 * D_EXPERT),
            w_down4=w_dn.reshape(N_GROUPS, EXP_PER_GROUP, D_EXPERT, D_MODEL)))

    bias_tabs = [_band_bias_table(rel_bias[l]) for l in range(depth)]

    def attn_p(l, qa, ka, va, kt, vt):
        return _attn_prompt(qa, ka, va, bias_tabs[l])

    def state_p(l, B):
        return (jnp.zeros((B, N_HEADS_M, HD_M, HD_M), F32), jnp.zeros((B, N_HEADS_M, 1, HD_M), F32),
                jnp.zeros((B, N_HEADS_M, 1, HD_M), F32))

    keep_p = min(BAND_PAST, Sp)
    y_p, sp = _trunk(x_prompt, mods_all[:, :Bp], mods_fin[:Bp], wts, attn_p, state_p,
                     ts=512, keep=keep_p, L=256, sort_tokens=True)

    Wc = cache_attn_k.shape[3]
    tq = jnp.arange(Ss)[:, None]
    dist_c = Wc + tq - jnp.arange(Wc)[None, :]
    dist_n = tq - jnp.arange(Ss)[None, :]
    lookup = lambda l, dist: rel_bias[l][:, jnp.clip(dist, -MAX_REL, MAX_REL) + MAX_REL].astype(F32)

    def attn_s(l, qa, ka, va, kt, vt):
        o = _attn_sample(_heads(qa.astype(F32), N_HEADS_A), _heads(kt, N_HEADS_A), _heads(vt, N_HEADS_A),
                         cache_attn_k[l], cache_attn_v[l], lookup(l, dist_c), lookup(l, dist_n))
        return o.transpose(0, 2, 1, 3).reshape(Bs, Ss, D_ATT).astype(BF16)

    def state_s(l, B):
        return (state_mlstm_C[l], state_mlstm_n[l][:, :, None, :],
                jnp.broadcast_to(state_mlstm_m[l][:, :, None, None], (B, N_HEADS_M, 1, HD_M)))

    y_s, ss = _trunk(x_sample, mods_all[:, Bp:Bp + Bs], mods_fin[Bp:Bp + Bs], wts, attn_s, state_s,
                     ts=Ss, keep=Ss, L=Ss, sort_tokens=False)

    stack = lambda sts, i: jnp.stack([s[i] for s in sts])
    return (y_p, y_s,
            stack(sp, 0), stack(sp, 1), stack(sp, 2), stack(sp, 3), stack(sp, 4),
            stack(ss, 0), stack(ss, 1), stack(ss, 2), stack(ss, 3), stack(ss, 4))
```

```python
import functools

import jax
import jax.numpy as jnp
from jax import lax
from jax.experimental import pallas as pl
from jax.experimental.pallas import tpu as pltpu

F32 = jnp.float32
BF16 = jnp.bfloat16

D_MODEL = 1024
CHUNK = 64
BAND_CHUNKS = 8
BAND_PAST = BAND_CHUNKS * CHUNK
D_ATT = 512
N_HEADS_A = 8
HD_A = 64
MAX_REL = 128
D_MLSTM = 512
N_HEADS_M = 4
HD_M = 128
N_GROUPS = 4
EXP_PER_GROUP = 4
N_EXPERTS = 16
D_EXPERT = 256
EPS = 1e-6
NEG = -1e30

LANES = 128
SUBLANES = 8
N_SLABS = 7
SLAB = 512
ATTN_HEADS = 4
Q_BLOCK = 256
K_WINDOW = BAND_PAST + Q_BLOCK
ROUTER_E0 = N_GROUPS
PACKED = D_MODEL // 2
GATE_GROUP = PACKED // LANES
DMA_UNROLL = 8
MOE_TILE = 512
VMEM_LIMIT = 56 * 1024 * 1024


def _cparams(sem):
    return pltpu.CompilerParams(dimension_semantics=sem, vmem_limit_bytes=VMEM_LIMIT)


def _ada_kernel(c_ref, w_ref, b_ref, o_ref):
    c = c_ref[...]
    a = (c * jax.nn.sigmoid(c)).astype(BF16)
    o_ref[0] = jnp.dot(a, w_ref[0].astype(BF16), preferred_element_type=F32) + b_ref[0]


def _ada(c, w, b, tn):
    nl, d, n = w.shape
    bc = c.shape[0]
    return pl.pallas_call(
        _ada_kernel,
        out_shape=jax.ShapeDtypeStruct((nl, bc, n), F32),
        grid=(nl, n // tn),
        in_specs=[pl.BlockSpec((bc, d), lambda l, j: (0, 0)),
                  pl.BlockSpec((1, d, tn), lambda l, j: (l, 0, j)),
                  pl.BlockSpec((1, 1, tn), lambda l, j: (l, 0, j))],
        out_specs=pl.BlockSpec((1, bc, tn), lambda l, j: (l, 0, j)),
        compiler_params=_cparams(("parallel", "parallel")),
        name="ada",
    )(c, w, b.reshape(nl, 1, n))


def _norm_mod(x, g, scale, shift):
    r = lax.rsqrt(jnp.mean(x * x, axis=-1, keepdims=True) + EPS)
    return (x * r * g) * (1.0 + scale) + shift


def _inproj_kernel(x_ref, sh_ref, sc_ref, g_ref, w_ref, wg_ref, bg_ref,
                   qa_ref, ka_ref, va_ref, qm_ref, km_ref, vm_ref, om_ref, gc_ref, kt_ref, vt_ref,
                   *, first_tail_tile):
    hb = _norm_mod(x_ref[0], g_ref[...], sc_ref[...], sh_ref[...]).astype(BF16)
    outs = (qa_ref, ka_ref, va_ref, qm_ref, km_ref, vm_ref, om_ref)
    in_tail = pl.program_id(1) >= first_tail_tile
    for j, o_ref in enumerate(outs):
        p = jnp.dot(hb, w_ref[:, j * SLAB:(j + 1) * SLAB], preferred_element_type=F32)
        if j == 4:
            p = p * (HD_M ** -0.5)
        o_ref[0] = p.astype(BF16)
        if j in (1, 2):
            t_ref = kt_ref if j == 1 else vt_ref

            @pl.when(in_tail)
            def _():
                t_ref[0] = p
    pg = jnp.dot(hb, wg_ref[...], preferred_element_type=F32) + bg_ref[...]
    lsig = jnp.minimum(pg, 0.0) - jnp.log1p(jnp.exp(-jnp.abs(pg)))
    lane = lax.broadcasted_iota(jnp.int32, pg.shape, 1)
    gc = jnp.where(lane < N_HEADS_M, pg, lsig)
    gc_ref[0] = gc[:, :2 * N_HEADS_M]


def _in_proj(x, mods, g, w_main, w_gate, b_gate, ts, keep):
    B, S, D = x.shape
    n_tiles = S // ts
    n_tail = keep // ts
    first_tail = n_tiles - n_tail
    slab = jax.ShapeDtypeStruct((B, S, SLAB), BF16)
    tail = jax.ShapeDtypeStruct((B, keep, SLAB), F32)
    slab_spec = pl.BlockSpec((1, ts, SLAB), lambda b, s: (b, s, 0))
    tail_spec = pl.BlockSpec((1, ts, SLAB), lambda b, s: (b, jnp.maximum(s - first_tail, 0), 0))
    return pl.pallas_call(
        functools.partial(_inproj_kernel, first_tail_tile=first_tail),
        out_shape=(slab,) * N_SLABS + (jax.ShapeDtypeStruct((B, S, 2 * N_HEADS_M), F32), tail, tail),
        grid=(B, n_tiles),
        in_specs=[pl.BlockSpec((1, ts, D), lambda b, s: (b, s, 0)),
                  pl.BlockSpec((None, None, 1, D), lambda b, s: (b, 0, 0, 0)),
                  pl.BlockSpec((None, None, 1, D), lambda b, s: (b, 1, 0, 0)),
                  pl.BlockSpec((1, D), lambda b, s: (0, 0)),
                  pl.BlockSpec((D, N_SLABS * SLAB), lambda b, s: (0, 0)),
                  pl.BlockSpec((D, LANES), lambda b, s: (0, 0)),
                  pl.BlockSpec((1, LANES), lambda b, s: (0, 0))],
        out_specs=(slab_spec,) * N_SLABS + (pl.BlockSpec((1, ts, 2 * N_HEADS_M), lambda b, s: (b, s, 0)),
                                            tail_spec, tail_spec),
        compiler_params=_cparams(("parallel", "arbitrary")),
        name="in_proj",
    )(x, mods, mods, g, w_main, w_gate, b_gate)


def _attn_prompt_kernel(q_ref, k_ref, v_ref, bias_ref, o_ref):
    i = pl.program_id(2)
    ws = pl.multiple_of(jnp.maximum(i * Q_BLOCK - BAND_PAST, 0), Q_BLOCK)
    variant = jnp.minimum(i, BAND_PAST // Q_BLOCK)
    q = q_ref[0]
    kw = k_ref[0, pl.ds(ws, K_WINDOW), :]
    vw = v_ref[0, pl.ds(ws, K_WINDOW), :]
    lane = lax.broadcasted_iota(jnp.int32, q.shape, 1)
    out = None
    for hh in range(ATTN_HEADS):
        in_head = (lane >= hh * HD_A) & (lane < (hh + 1) * HD_A)
        qh = jnp.where(in_head, q * (HD_A ** -0.5), 0.0).astype(BF16)
        s = lax.dot_general(qh, kw, (((1,), (1,)), ((), ())), preferred_element_type=F32)
        s = s + bias_ref[variant, hh]
        mx = jnp.max(s, axis=-1, keepdims=True)
        p = jnp.exp(s - mx)
        l = jnp.sum(p, axis=-1, keepdims=True)
        o = jnp.dot(p.astype(BF16), vw, preferred_element_type=F32) / l
        out = o if out is None else jnp.where(in_head, o, out)
    o_ref[0] = out.astype(BF16)


def _band_bias_table(table):
    H = table.shape[0]
    a = jnp.arange(Q_BLOCK)[:, None]
    j = jnp.arange(K_WINDOW)[None, :]
    period = Q_BLOCK + K_WINDOW
    u = jnp.arange(period)
    v = jnp.where(u < K_WINDOW, u, u - period)
    tabs = []
    for off in range(0, BAND_PAST + 1, Q_BLOCK):
        prof = table[:, jnp.clip(off - v, -MAX_REL, MAX_REL) + MAX_REL].astype(F32)
        skew = jnp.tile(prof, (1, Q_BLOCK))[:, :Q_BLOCK * (period - 1)].reshape(H, Q_BLOCK, period - 1)
        bias = skew[:, :, :K_WINDOW]
        qc = (a + off) // CHUNK
        kc = j // CHUNK
        valid = (kc <= qc) & (kc >= qc - BAND_CHUNKS)
        tabs.append(jnp.where(valid[None], bias, NEG))
    return jnp.stack(tabs)


def _attn_prompt(q, k, v, bias_tab):
    B, S, _ = q.shape
    nv = bias_tab.shape[0]
    width = ATTN_HEADS * HD_A
    return pl.pallas_call(
        _attn_prompt_kernel,
        out_shape=jax.ShapeDtypeStruct((B, S, D_ATT), BF16),
        grid=(N_HEADS_A // ATTN_HEADS, B, S // Q_BLOCK),
        in_specs=[pl.BlockSpec((1, Q_BLOCK, width), lambda hp, b, i: (b, i, hp)),
                  pl.BlockSpec((1, S, width), lambda hp, b, i: (b, 0, hp)),
                  pl.BlockSpec((1, S, width), lambda hp, b, i: (b, 0, hp)),
                  pl.BlockSpec((nv, ATTN_HEADS, Q_BLOCK, K_WINDOW), lambda hp, b, i: (0, hp, 0, 0))],
        out_specs=pl.BlockSpec((1, Q_BLOCK, width), lambda hp, b, i: (b, i, hp)),
        compiler_params=_cparams(("parallel", "parallel", "arbitrary")),
        name="attn_prompt",
    )(q, k, v, bias_tab)


def _attn_sample_kernel(q_ref, kn_ref, vn_ref, kc_ref, vc_ref, bc_ref, bn_ref, o_ref):
    for h in range(N_HEADS_A):
        q = (q_ref[0, h] * (HD_A ** -0.5)).astype(BF16)
        kn = kn_ref[0, h].astype(BF16)
        vn = vn_ref[0, h].astype(BF16)
        kc = kc_ref[0, h].astype(BF16)
        vc = vc_ref[0, h].astype(BF16)
        nt = (((1,), (1,)), ((), ()))
        sc = lax.dot_general(q, kc, nt, preferred_element_type=F32) + bc_ref[h]
        sn = lax.dot_general(q, kn, nt, preferred_element_type=F32) + bn_ref[h]
        mx = jnp.maximum(jnp.max(sc, axis=-1, keepdims=True), jnp.max(sn, axis=-1, keepdims=True))
        pc = jnp.exp(sc - mx)
        pn = jnp.exp(sn - mx)
        l = jnp.sum(pc, axis=-1, keepdims=True) + jnp.sum(pn, axis=-1, keepdims=True)
        o = (jnp.dot(pc.astype(BF16), vc, preferred_element_type=F32)
             + jnp.dot(pn.astype(BF16), vn, preferred_element_type=F32))
        o_ref[0, h] = o / l


def _attn_sample(q, kn, vn, kc, vc, bias_c, bias_n):
    B, H, T, d = q.shape
    W = kc.shape[2]
    new = pl.BlockSpec((1, H, T, d), lambda b: (b, 0, 0, 0))
    old = pl.BlockSpec((1, H, W, d), lambda b: (b, 0, 0, 0))
    return pl.pallas_call(
        _attn_sample_kernel,
        out_shape=jax.ShapeDtypeStruct((B, H, T, d), F32),
        grid=(B,),
        in_specs=[new, new, new, old, old,
                  pl.BlockSpec((H, T, W), lambda b: (0, 0, 0)),
                  pl.BlockSpec((H, T, T), lambda b: (0, 0, 0))],
        out_specs=new,
        compiler_params=_cparams(("parallel",)),
        name="attn_sample",
    )(q, kn, vn, kc, vc, bias_c, bias_n)


def _mlstm_kernel(q_ref, k_ref, v_ref, om_ref, gc_ref, gml_ref, c0_ref, n0_ref, m0_ref,
                  h_ref, c_ref, n_ref, m_ref, *, L, nc):
    c_ref[0] = c0_ref[0]
    n_ref[0] = n0_ref[0]
    m_ref[0] = m0_ref[0]
    tt = lax.broadcasted_iota(jnp.int32, (L, L), 0)
    ss = lax.broadcasted_iota(jnp.int32, (L, L), 1)
    causal = ss <= tt
    eye = ss == tt
    nt = (((1,), (1,)), ((), ()))
    tn = (((0,), (0,)), ((), ()))

    def chunk(ci, carry):
        off = pl.multiple_of(ci * L, L)
        g = gc_ref[0, pl.ds(off, L), :]
        for hh in range(N_HEADS_M):
            cols = slice(hh * HD_M, (hh + 1) * HD_M)
            ig_c = g[:, hh:hh + 1]
            lf_c = g[:, N_HEADS_M + hh:N_HEADS_M + hh + 1]
            ig_r = jnp.sum(jnp.where(eye, ig_c, 0.0), axis=0, keepdims=True)
            lf_r = jnp.sum(jnp.where(eye, lf_c, 0.0), axis=0, keepdims=True)
            b_r = jnp.sum(jnp.where(tt <= ss, lf_c, 0.0), axis=0, keepdims=True)
            b_c = jnp.sum(jnp.where(causal, lf_r, 0.0), axis=1, keepdims=True)
            a_r = ig_r - b_r
            a_c = ig_c - b_c
            m_prev = m_ref[0, hh][:, 0:1]
            Mt = jnp.maximum(jnp.max(jnp.where(causal, a_r, -jnp.inf), axis=1, keepdims=True), m_prev)
            P = jnp.where(causal, jnp.exp(a_r - Mt), 0.0)
            decay = jnp.exp(m_prev - Mt)
            q = q_ref[0, pl.ds(off, L), cols]
            k = k_ref[0, pl.ds(off, L), cols]
            v = v_ref[0, pl.ds(off, L), cols]
            C = c_ref[0, hh]
            n = n_ref[0, hh]
            S = P * lax.dot_general(q, k, nt, preferred_element_type=F32)
            num = decay * jnp.dot(q, C.astype(BF16), preferred_element_type=F32) \
                + jnp.dot(S.astype(BF16), v, preferred_element_type=F32)
            den = decay * jnp.sum(q.astype(F32) * n, axis=1, keepdims=True) + jnp.sum(S, axis=1, keepdims=True)
            hv = num / jnp.maximum(jnp.abs(den), jnp.exp(-(b_c + Mt)))
            hn = hv * lax.rsqrt(jnp.mean(hv * hv, axis=-1, keepdims=True) + EPS) * gml_ref[:, cols]
            om = om_ref[0, pl.ds(off, L), cols].astype(F32)
            h_ref[0, pl.ds(off, L), cols] = (hn * jax.nn.sigmoid(om)).astype(BF16)
            M_last = Mt[L - 1:L, :]
            b_last = b_c[L - 1:L, :]
            kw = k.astype(F32) * jnp.exp(a_c - M_last)
            cd = jnp.exp(m_prev - M_last)
            c_ref[0, hh] = cd * C + lax.dot_general(kw.astype(BF16), v, tn, preferred_element_type=F32)
            n_ref[0, hh] = cd * n + jnp.sum(kw, axis=0, keepdims=True)
            m_ref[0, hh] = jnp.broadcast_to(b_last + M_last, (1, HD_M))
        return carry

    lax.fori_loop(0, nc, chunk, 0)


def _mlstm(q, k, v, om, gc, gml, c0, n0, m0, L):
    B, S, _ = q.shape
    H = N_HEADS_M
    seq = pl.BlockSpec((1, S, D_MLSTM), lambda b: (b, 0, 0))
    cst = pl.BlockSpec((1, H, HD_M, HD_M), lambda b: (b, 0, 0, 0))
    vec = pl.BlockSpec((1, H, 1, HD_M), lambda b: (b, 0, 0, 0))
    return pl.pallas_call(
        functools.partial(_mlstm_kernel, L=L, nc=S // L),
        out_shape=(jax.ShapeDtypeStruct((B, S, D_MLSTM), BF16),
                   jax.ShapeDtypeStruct((B, H, HD_M, HD_M), F32),
                   jax.ShapeDtypeStruct((B, H, 1, HD_M), F32),
                   jax.ShapeDtypeStruct((B, H, 1, HD_M), F32)),
        grid=(B,),
        in_specs=[seq, seq, seq, seq,
                  pl.BlockSpec((1, S, 2 * H), lambda b: (b, 0, 0)),
                  pl.BlockSpec((1, D_MLSTM), lambda b: (0, 0)),
                  cst, vec, vec],
        out_specs=(seq, cst, vec, vec),
        compiler_params=_cparams(("parallel",)),
        name="mlstm",
    )(q, k, v, om, gc, gml, c0, n0, m0)


def _mix_residual_norm(att_ref, hm_ref, x_ref, gt_ref, sh_ref, sc_ref, g_ref, wo_ref):
    mix = (jnp.dot(att_ref[0], wo_ref[:D_ATT, :], preferred_element_type=F32)
           + jnp.dot(hm_ref[0], wo_ref[D_ATT:, :], preferred_element_type=F32))
    x1 = x_ref[0] + gt_ref[...] * mix
    return x1, _norm_mod(x1, g_ref[...], sc_ref[...], sh_ref[...]).astype(BF16)


def _router(hb, wr_ref, br_ref):
    logit = jnp.dot(hb, wr_ref[...], preferred_element_type=F32) + br_ref[...]
    lane = lax.broadcasted_iota(jnp.int32, logit.shape, 1)
    big = jnp.int32(LANES)
    is_g = lane < N_GROUPS
    mg = jnp.max(jnp.where(is_g, logit, -jnp.inf), axis=-1, keepdims=True)
    eg = jnp.where(is_g, jnp.exp(logit - mg), 0.0)
    pg = eg / jnp.sum(eg, axis=-1, keepdims=True)
    p_top = jnp.max(pg, axis=-1, keepdims=True)
    g_idx = jnp.min(jnp.where(is_g & (pg == p_top), lane, big), axis=-1, keepdims=True)
    e_lo = ROUTER_E0 + g_idx * EXP_PER_GROUP
    sel = (lane >= e_lo) & (lane < e_lo + EXP_PER_GROUP)
    me = jnp.max(jnp.where(sel, logit, -jnp.inf), axis=-1, keepdims=True)
    ee = jnp.where(sel, jnp.exp(logit - me), 0.0)
    pe = ee / jnp.sum(ee, axis=-1, keepdims=True)
    v1 = jnp.max(jnp.where(sel, pe, -1.0), axis=-1, keepdims=True)
    i1 = jnp.min(jnp.where(sel & (pe == v1), lane, big), axis=-1, keepdims=True)
    sel2 = sel & (lane != i1)
    v2 = jnp.max(jnp.where(sel2, pe, -1.0), axis=-1, keepdims=True)
    i2 = jnp.min(jnp.where(sel2 & (pe == v2), lane, big), axis=-1, keepdims=True)
    tot = v1 + v2
    w1 = v1 / tot * p_top
    w2 = v2 / tot * p_top
    return jnp.where(lane == i1, w1, jnp.where(lane == i2, w2, 0.0)), g_idx


def _outproj_kernel(att_ref, hm_ref, x_ref, gt_ref, sh_ref, sc_ref, g_ref, wo_ref, wr_ref, br_ref,
                    x1_ref, h2_ref, gates_ref):
    x1, hb = _mix_residual_norm(att_ref, hm_ref, x_ref, gt_ref, sh_ref, sc_ref, g_ref, wo_ref)
    x1_ref[0] = x1
    h2_ref[0] = hb
    gates_ref[0] = _router(hb, wr_ref, br_ref)[0]


def _pack_bf16_pairs(lo, hi):
    lo_bits = lax.bitcast_convert_type(lo.astype(BF16).astype(F32), jnp.uint32)
    hi_bits = lax.bitcast_convert_type(hi.astype(BF16).astype(F32), jnp.uint32)
    return (hi_bits & jnp.uint32(0xFFFF0000)) | (lo_bits >> 16)


def _unpack_bf16_pairs(packed):
    lo = lax.bitcast_convert_type(packed << 16, F32)
    hi = lax.bitcast_convert_type(packed & jnp.uint32(0xFFFF0000), F32)
    return lo, hi


def _outproj_sort_kernel(att_ref, hm_ref, x_ref, gt_ref, sh_ref, sc_ref, g_ref, wo_ref, wr_ref, br_ref,
                         x1_ref, row_ref, idx_ref, cnt_ref, fill_ref, *, region_rows):
    first = (pl.program_id(0) == 0) & (pl.program_id(1) == 0)

    @pl.when(first)
    def _():
        fill_ref[...] = jnp.zeros_like(fill_ref)

    x1, hb = _mix_residual_norm(att_ref, hm_ref, x_ref, gt_ref, sh_ref, sc_ref, g_ref, wo_ref)
    x1_ref[0] = x1
    gates, g_idx = _router(hb, wr_ref, br_ref)
    hf = hb.astype(F32)
    half = hf.shape[1] // 2
    ts = gates.shape[0]
    packed = _pack_bf16_pairs(hf[:, :half], hf[:, half:])
    for k in range(SUBLANES):
        if k < GATE_GROUP:
            piece = packed[:, k * LANES:(k + 1) * LANES]
        elif k == GATE_GROUP:
            piece = lax.bitcast_convert_type(gates, jnp.uint32)
        else:
            piece = jnp.zeros((ts, LANES), jnp.uint32)
        row_ref[_lane_group(k, ts), :] = piece
    lane = lax.broadcasted_iota(jnp.int32, gates.shape, 1)
    onehot = lane == g_idx
    tt = lax.broadcasted_iota(jnp.int32, (ts, ts), 0)
    ss = lax.broadcasted_iota(jnp.int32, (ts, ts), 1)
    before = jnp.where(ss < tt, 1.0, 0.0).astype(BF16)
    rank = jnp.dot(before, jnp.where(onehot, 1.0, 0.0).astype(BF16), preferred_element_type=F32)
    fill = fill_ref[...]
    slot = jnp.sum(jnp.where(onehot, rank + fill, 0.0), axis=-1, keepdims=True)
    slot = slot + g_idx.astype(F32) * float(region_rows)
    slot_row = jnp.sum(jnp.where(ss == tt, slot, 0.0), axis=0, keepdims=True)
    idx_ref[...] = slot_row.astype(jnp.int32)
    fill = fill + jnp.sum(jnp.where(onehot, 1.0, 0.0), axis=0, keepdims=True)
    fill_ref[...] = fill
    cnt_ref[...] = fill.astype(jnp.int32)


def _out_proj(att, hm, x, mods, g, w_out, w_router, b_router, ts, region_rows=None):
    B, S, D = x.shape
    nt = S // ts
    tile = lambda w: pl.BlockSpec((1, ts, w), lambda b, s: (b, s, 0))
    mod = lambda i: pl.BlockSpec((None, None, 1, D), lambda b, s: (b, i, 0, 0))
    const = lambda shape: pl.BlockSpec(shape, lambda b, s: (0, 0))
    in_specs = [tile(D_ATT), tile(D_MLSTM), tile(D), mod(2), mod(3), mod(4), const((1, D)),
                const((D, D)), const((D, LANES)), const((1, LANES))]
    args = (att, hm, x, mods, mods, mods, g, w_out, w_router, b_router)
    if region_rows is None:
        return pl.pallas_call(
            _outproj_kernel,
            out_shape=(jax.ShapeDtypeStruct((B, S, D), F32),
                       jax.ShapeDtypeStruct((B, S, D), BF16),
                       jax.ShapeDtypeStruct((B, S, LANES), F32)),
            grid=(B, nt), in_specs=in_specs,
            out_specs=(tile(D), tile(D), tile(LANES)),
            compiler_params=_cparams(("parallel", "parallel")),
            name="out_proj",
        )(*args)
    return pl.pallas_call(
        functools.partial(_outproj_sort_kernel, region_rows=region_rows),
        out_shape=(jax.ShapeDtypeStruct((B, S, D), F32),
                   jax.ShapeDtypeStruct((B, S * SUBLANES, LANES), jnp.uint32),
                   jax.ShapeDtypeStruct((B * nt, 1, ts), jnp.int32),
                   jax.ShapeDtypeStruct((1, LANES), jnp.int32)),
        grid=(B, nt), in_specs=in_specs,
        out_specs=(tile(D), pl.BlockSpec((None, ts * SUBLANES, LANES), lambda b, s: (b, s, 0)),
                   pl.BlockSpec((None, 1, ts), lambda b, s: (b * nt + s, 0, 0)),
                   const((1, LANES))),
        scratch_shapes=[pltpu.VMEM((1, LANES), F32)],
        compiler_params=_cparams(("arbitrary", "arbitrary")),
        name="out_proj_sort",
    )(*args)


def _moe_kernel(h_ref, gates_ref, x_ref, gt_ref, wgu_ref, wd_ref, o_ref, acc_ref):
    e = pl.program_id(2)

    @pl.when(e == 0)
    def _():
        acc_ref[...] = jnp.zeros_like(acc_ref)

    h = h_ref[0]
    au = jnp.dot(h, wgu_ref[0], preferred_element_type=F32)
    a = au[:, :D_EXPERT]
    u = au[:, D_EXPERT:]
    act = (a * jax.nn.sigmoid(a) * u).astype(BF16)
    y = jnp.dot(act, wd_ref[0], preferred_element_type=F32)
    gates = gates_ref[0]
    lane = lax.broadcasted_iota(jnp.int32, gates.shape, 1)
    ge = jnp.sum(jnp.where(lane == ROUTER_E0 + e, gates, 0.0), axis=-1, keepdims=True)
    acc_ref[...] += ge * y

    @pl.when(e == N_EXPERTS - 1)
    def _():
        o_ref[0] = x_ref[0] + gt_ref[0] * acc_ref[...]


def _moe(h2, gates, x1, mods, w_gu, w_down, ts):
    B, S, D = x1.shape
    tile = lambda w: pl.BlockSpec((1, ts, w), lambda b, s, e: (b, s, 0))
    return pl.pallas_call(
        _moe_kernel,
        out_shape=jax.ShapeDtypeStruct((B, S, D), F32),
        grid=(B, S // ts, N_EXPERTS),
        in_specs=[tile(D), tile(LANES), tile(D), tile(D),
                  pl.BlockSpec((1, D, 2 * D_EXPERT), lambda b, s, e: (e, 0, 0)),
                  pl.BlockSpec((1, D_EXPERT, D), lambda b, s, e: (e, 0, 0))],
        out_specs=tile(D),
        scratch_shapes=[pltpu.VMEM((ts, D), F32)],
        compiler_params=_cparams(("parallel", "parallel", "arbitrary")),
        name="moe",
    )(h2, gates, x1, mods, w_gu, w_down)


def _token_rows(token):
    return pl.ds(pl.multiple_of(token * SUBLANES, SUBLANES), SUBLANES)


def _lane_group(k, n, first_token=0):
    return pl.ds(first_token * SUBLANES + k, n, stride=SUBLANES)


def _for_each_token(ts, fn):
    def body(j, c):
        for u in range(DMA_UNROLL):
            fn(j * DMA_UNROLL + u, u % 2)
        return c
    lax.fori_loop(0, ts // DMA_UNROLL, body, 0)


def _dispatch_kernel(cnt_ref, row_ref, idx_ref, hs_ref, buf_ref, zero_ref, sem_ref, zsem_ref, *, region_rows, tm):
    i = pl.program_id(0)
    n = pl.num_programs(0)
    ts = idx_ref.shape[1]
    slot = i % 2

    def wait_slot(s):
        pltpu.make_async_copy(buf_ref.at[s], hs_ref.at[pl.ds(0, ts * SUBLANES), :], sem_ref.at[s]).wait()

    @pl.when(i >= 2)
    def _():
        wait_slot(slot)

    buf_ref[slot] = row_ref[...]

    def issue(r, priority):
        pltpu.make_async_copy(buf_ref.at[slot, _token_rows(r), :], hs_ref.at[_token_rows(idx_ref[0, r]), :],
                              sem_ref.at[slot]).start(priority=priority)
    _for_each_token(ts, issue)

    @pl.when(i == n - 1)
    def _():
        wait_slot(slot)

        @pl.when(n >= 2)
        def _():
            wait_slot(1 - slot)

        zero_ref[...] = jnp.zeros_like(zero_ref)

        def tail_copy(g):
            first = g * region_rows + cnt_ref[g]
            dst = hs_ref.at[pl.ds(pl.multiple_of(first * SUBLANES, SUBLANES), tm * SUBLANES), :]
            return pltpu.make_async_copy(zero_ref, dst, zsem_ref.at[g])

        for g in range(N_GROUPS):
            tail_copy(g).start()
        for g in range(N_GROUPS):
            tail_copy(g).wait()


def _dispatch(rows, idx, counts, ts, tm, region_rows):
    T = rows.shape[0] // SUBLANES
    grid_spec = pltpu.PrefetchScalarGridSpec(
        num_scalar_prefetch=1,
        grid=(T // ts,),
        in_specs=[pl.BlockSpec((ts * SUBLANES, LANES), lambda i, cnt: (i, 0)),
                  pl.BlockSpec((None, 1, ts), lambda i, cnt: (i, 0, 0), memory_space=pltpu.SMEM)],
        out_specs=pl.BlockSpec(memory_space=pl.ANY),
        scratch_shapes=[pltpu.VMEM((2, ts * SUBLANES, LANES), jnp.uint32),
                        pltpu.VMEM((tm * SUBLANES, LANES), jnp.uint32),
                        pltpu.SemaphoreType.DMA((2,)),
                        pltpu.SemaphoreType.DMA((N_GROUPS,))])
    return pl.pallas_call(
        functools.partial(_dispatch_kernel, region_rows=region_rows, tm=tm),
        out_shape=jax.ShapeDtypeStruct((N_GROUPS * region_rows * SUBLANES, LANES), jnp.uint32),
        grid_spec=grid_spec,
        compiler_params=_cparams(("arbitrary",)),
        name="moe_dispatch",
    )(counts, rows, idx)


def _moe_group_kernel(blk_ref, grp_ref, live_ref, hs_ref, wgu_ref, wd_ref, o_ref):
    i = pl.program_id(0)

    @pl.when(live_ref[i] == 0)
    def _():
        o_ref[...] = jnp.zeros_like(o_ref)

    @pl.when(live_ref[i] != 0)
    def _():
        tm = hs_ref.shape[0] // SUBLANES
        halves = [_unpack_bf16_pairs(hs_ref[_lane_group(k, tm), :]) for k in range(PACKED // LANES)]
        lo = jnp.concatenate([h[0] for h in halves], axis=1).astype(BF16)
        hi = jnp.concatenate([h[1] for h in halves], axis=1).astype(BF16)
        gates = lax.bitcast_convert_type(hs_ref[_lane_group(GATE_GROUP, tm), :], F32)
        lane = lax.broadcasted_iota(jnp.int32, gates.shape, 1)
        e0 = ROUTER_E0 + grp_ref[i] * EXP_PER_GROUP
        y = jnp.zeros((tm, D_MODEL), F32)
        for e in range(EXP_PER_GROUP):
            au = (jnp.dot(lo, wgu_ref[e, :PACKED, :], preferred_element_type=F32)
                  + jnp.dot(hi, wgu_ref[e, PACKED:, :], preferred_element_type=F32))
            a = au[:, :D_EXPERT]
            u = au[:, D_EXPERT:]
            act = (a * jax.nn.sigmoid(a) * u).astype(BF16)
            ge = jnp.sum(jnp.where(lane == e0 + e, gates, 0.0), axis=-1, keepdims=True)
            y = y + ge * jnp.dot(act, wd_ref[e], preferred_element_type=F32)
        for k in range(D_MODEL // LANES):
            o_ref[_lane_group(k, tm), :] = y[:, k * LANES:(k + 1) * LANES]


def _moe_group(hs, blk, grp, live, w_gu, w_down, tm):
    n_steps = blk.shape[0]
    rows = tm * SUBLANES
    n_blocks = hs.shape[0] // rows
    grid_spec = pltpu.PrefetchScalarGridSpec(
        num_scalar_prefetch=3,
        grid=(n_steps,),
        in_specs=[pl.BlockSpec((rows, LANES), lambda i, blk, grp, live: (blk[i], 0)),
                  pl.BlockSpec((None, EXP_PER_GROUP, D_MODEL, 2 * D_EXPERT), lambda i, blk, grp, live: (grp[i], 0, 0, 0)),
                  pl.BlockSpec((None, EXP_PER_GROUP, D_EXPERT, D_MODEL), lambda i, blk, grp, live: (grp[i], 0, 0, 0))],
        out_specs=pl.BlockSpec((rows, LANES),
                               lambda i, blk, grp, live: (jnp.where(live[i] != 0, blk[i], n_blocks), 0)))
    return pl.pallas_call(
        _moe_group_kernel,
        out_shape=jax.ShapeDtypeStruct(((n_blocks + 1) * rows, LANES), F32),
        grid_spec=grid_spec,
        compiler_params=_cparams(("arbitrary",)),
        name="moe_group",
    )(blk, grp, live, hs, w_gu, w_down)


def _combine_kernel(x1_ref, gt_ref, idx0_ref, idx1_ref, ys_ref, *rest, final):
    if final:
        shf_ref, scf_ref, gf_ref, o_ref, buf_ref, sem_ref = rest
    else:
        o_ref, buf_ref, sem_ref = rest
    i = pl.program_id(0)
    n = pl.num_programs(0)
    ts = idx0_ref.shape[1]
    slot = i % 2

    def issue(idx_ref, s):
        def one(r, priority):
            pltpu.make_async_copy(ys_ref.at[_token_rows(idx_ref[0, r]), :], buf_ref.at[s, _token_rows(r), :],
                                  sem_ref.at[s]).start(priority=priority)
        _for_each_token(ts, one)

    @pl.when(i == 0)
    def _():
        issue(idx0_ref, 0)

    @pl.when(i + 1 < n)
    def _():
        issue(idx1_ref, 1 - slot)

    pltpu.make_async_copy(ys_ref.at[pl.ds(0, ts * SUBLANES), :], buf_ref.at[slot], sem_ref.at[slot]).wait()

    g = gt_ref[...]
    for k in range(D_MODEL // LANES):
        cols = slice(k * LANES, (k + 1) * LANES)
        y = buf_ref[slot, _lane_group(k, ts), :]
        o_ref[:, cols] = x1_ref[:, cols] + g[:, cols] * y
    if final:
        o_ref[...] = _norm_mod(o_ref[...], gf_ref[...], scf_ref[...], shf_ref[...])


def _combine(x1, mods, idx, ys, ts, final=None):
    B, S, D = x1.shape
    nt = S // ts
    n = B * nt
    in_specs = [pl.BlockSpec((ts, D), lambda i: (i, 0)),
                pl.BlockSpec((None, None, 1, D), lambda i: (i // nt, 5, 0, 0)),
                pl.BlockSpec((None, 1, ts), lambda i: (i, 0, 0), memory_space=pltpu.SMEM),
                pl.BlockSpec((None, 1, ts), lambda i: (jnp.minimum(i + 1, n - 1), 0, 0), memory_space=pltpu.SMEM),
                pl.BlockSpec(memory_space=pl.ANY)]
    args = [x1.reshape(B * S, D), mods, idx, idx, ys]
    if final is not None:
        mods_f, g_final = final
        in_specs += [pl.BlockSpec((None, None, 1, D), lambda i: (i // nt, 0, 0, 0)),
                     pl.BlockSpec((None, None, 1, D), lambda i: (i // nt, 1, 0, 0)),
                     pl.BlockSpec((1, D), lambda i: (0, 0))]
        args += [mods_f, mods_f, g_final]
    out = pl.pallas_call(
        functools.partial(_combine_kernel, final=final is not None),
        out_shape=jax.ShapeDtypeStruct((B * S, D), F32),
        grid=(n,),
        in_specs=in_specs,
        out_specs=pl.BlockSpec((ts, D), lambda i: (i, 0)),
        scratch_shapes=[pltpu.VMEM((2, ts * SUBLANES, LANES), F32), pltpu.SemaphoreType.DMA((2,))],
        compiler_params=_cparams(("arbitrary",)),
        name="moe_combine",
    )(*args)
    return out.reshape(B, S, D)


def _tile_plan(counts, tm, n_steps, region_blocks):
    tiles = (counts + tm - 1) // tm
    ends = jnp.cumsum(tiles)
    total = ends[-1]
    step = jnp.minimum(jnp.arange(n_steps, dtype=jnp.int32), jnp.maximum(total - 1, 0))
    grp = jnp.sum((step[:, None] >= ends[None, :]).astype(jnp.int32), axis=1)
    grp = jnp.minimum(grp, N_GROUPS - 1)
    starts = ends - tiles
    blk = grp * region_blocks + (step - starts[grp])
    live = (jnp.arange(n_steps, dtype=jnp.int32) < total).astype(jnp.int32)
    return blk.astype(jnp.int32), grp.astype(jnp.int32), live


def _moe_sorted(rows, idx, counts, x1, mods, w_gu, w_down, ts, tm, final):
    B, S, D = x1.shape
    T = B * S
    region_rows = T + tm
    n_steps = T // tm + N_GROUPS
    cnt = counts[0, :N_GROUPS]
    hs = _dispatch(rows.reshape(T * SUBLANES, LANES), idx, cnt, ts, tm, region_rows)
    blk, grp, live = _tile_plan(cnt, tm, n_steps, region_rows // tm)
    ys = _moe_group(hs, blk, grp, live, w_gu, w_down, tm)
    return _combine(x1, mods, idx, ys, ts, final)


def _final_kernel(x_ref, sh_ref, sc_ref, g_ref, o_ref):
    o_ref[0] = _norm_mod(x_ref[0], g_ref[...], sc_ref[...], sh_ref[...])


def _final(x, mods_f, g, ts):
    B, S, D = x.shape
    return pl.pallas_call(
        _final_kernel,
        out_shape=jax.ShapeDtypeStruct((B, S, D), F32),
        grid=(B, S // ts),
        in_specs=[pl.BlockSpec((1, ts, D), lambda b, s: (b, s, 0)),
                  pl.BlockSpec((None, None, 1, D), lambda b, s: (b, 0, 0, 0)),
                  pl.BlockSpec((None, None, 1, D), lambda b, s: (b, 1, 0, 0)),
                  pl.BlockSpec((1, D), lambda b, s: (0, 0))],
        out_specs=pl.BlockSpec((1, ts, D), lambda b, s: (b, s, 0)),
        compiler_params=_cparams(("parallel", "parallel")),
        name="final_norm",
    )(x, mods_f, mods_f, g)


def _heads(t, n):
    B, S, _ = t.shape
    return t.reshape(B, S, n, -1).transpose(0, 2, 1, 3)


def _trunk(x, mods, mods_f, wts, attn_fn, state_fn, ts, keep, L, sort_tokens):
    B, S, D = x.shape
    states = []
    depth = mods.shape[0]
    final = (mods_f.reshape(B, 2, 1, D), wts[0]["g_final"])
    for l in range(depth):
        w = wts[l]
        m = mods[l].reshape(B, 6, 1, D)
        qa, ka, va, qm, km, vm, om, gc, kt, vt = _in_proj(
            x, m, w["g1"], w["w_main"], w["w_gate"], w["b_gate"], ts, keep)
        att = attn_fn(l, qa, ka, va, kt, vt)
        c0, n0, m0 = state_fn(l, B)
        hm, C, n, mm = _mlstm(qm, km, vm, om, gc, w["g_mlstm"], c0, n0, m0, L)
        if sort_tokens:
            x1, rows, idx, counts = _out_proj(att, hm, x, m, w["g2"], w["w_out"], w["w_router"], w["b_router"], ts,
                                              region_rows=B * S + MOE_TILE)
            x = _moe_sorted(rows, idx, counts, x1, m, w["w_gu4"], w["w_down4"], ts, MOE_TILE,
                            final if l == depth - 1 else None)
        else:
            x1, h2, gates = _out_proj(att, hm, x, m, w["g2"], w["w_out"], w["w_router"], w["b_router"], ts)
            flat = lambda t: t.reshape(1, B * S, t.shape[-1])
            gate = jnp.broadcast_to(m[:, 5], (B, S, D))
            x = _moe(flat(h2), flat(gates), flat(x1), flat(gate), w["w_gu"], w["w_down_e"], B * S).reshape(B, S, D)
        states.append((_heads(kt, N_HEADS_A), _heads(vt, N_HEADS_A), C, n[:, :, 0, :], mm[:, :, 0, 0]))
    if not sort_tokens:
        x = _final(x, *final, ts)
    return x, states


def kernel(x_prompt, x_sample, c_prompt, c_sample, cache_attn_k, cache_attn_v, state_mlstm_C, state_mlstm_n,
           state_mlstm_m, w_ada, b_ada, g_norm, w_in, b_igate, b_fgate, rel_bias, g_mlstm, w_out, w_router_g,
           b_router_g, w_router_e, b_router_e, w_gate, w_up, w_down, w_ada_f, b_ada_f, g_final):
    depth = w_ada.shape[0]
    Bp, Sp, D = x_prompt.shape
    Bs, Ss, _ = x_sample.shape

    c_all = jnp.concatenate([c_prompt, c_sample], axis=0)
    pad = (-c_all.shape[0]) % 16
    c_all = jnp.pad(c_all, ((0, pad), (0, 0)))
    mods_all = _ada(c_all, w_ada, b_ada, 1536)
    mods_fin = _ada(c_all, w_ada_f[None], b_ada_f[None], 1024)[0]

    wts = []
    for l in range(depth):
        wl = w_in[l]
        n_main = N_SLABS * SLAB
        w_gate_cols = jnp.pad(wl[:, n_main:], ((0, 0), (0, LANES - 2 * N_HEADS_M)))
        b_gate_cols = jnp.pad(jnp.concatenate([b_igate[l], b_fgate[l]]), (0, LANES - 2 * N_HEADS_M))
        w_r = jnp.concatenate([w_router_g[l], w_router_e[l]], axis=1)
        b_r = jnp.concatenate([b_router_g[l], b_router_e[l]])
        npad = LANES - N_GROUPS - N_EXPERTS
        w_gu = jnp.concatenate([w_gate[l], w_up[l]], axis=-1).astype(BF16)
        w_dn = w_down[l].astype(BF16)
        wts.append(dict(
            g1=g_norm[l, 0][None], g2=g_norm[l, 1][None], g_final=g_final[None],
            w_main=wl[:, :n_main].astype(BF16), w_gate=w_gate_cols.astype(BF16), b_gate=b_gate_cols[None],
            g_mlstm=g_mlstm[l][None], w_out=w_out[l].astype(BF16),
            w_router=jnp.pad(w_r, ((0, 0), (0, npad))).astype(BF16), b_router=jnp.pad(b_r, (0, npad))[None],
            w_gu=w_gu, w_down_e=w_dn,
            w_gu4=w_gu.reshape(N_GROUPS, EXP_PER_GROUP, D_MODEL, 2 * D_EXPERT),
            w_down4=w_dn.reshape(N_GROUPS, EXP_PER_GROUP, D_EXPERT, D_MODEL)))

    bias_tabs = [_band_bias_table(rel_bias[l]) for l in range(depth)]

    def attn_p(l, qa, ka, va, kt, vt):
        return _attn_prompt(qa, ka, va, bias_tabs[l])

    def state_p(l, B):
        return (jnp.zeros((B, N_HEADS_M, HD_M, HD_M), F32), jnp.zeros((B, N_HEADS_M, 1, HD_M), F32),
                jnp.zeros((B, N_HEADS_M, 1, HD_M), F32))

    keep_p = min(BAND_PAST, Sp)
    y_p, sp = _trunk(x_prompt, mods_all[:, :Bp], mods_fin[:Bp], wts, attn_p, state_p,
                     ts=512, keep=keep_p, L=512, sort_tokens=True)

    Wc = cache_attn_k.shape[3]
    tq = jnp.arange(Ss)[:, None]
    dist_c = Wc + tq - jnp.arange(Wc)[None, :]
    dist_n = tq - jnp.arange(Ss)[None, :]
    lookup = lambda l, dist: rel_bias[l][:, jnp.clip(dist, -MAX_REL, MAX_REL) + MAX_REL].astype(F32)

    def attn_s(l, qa, ka, va, kt, vt):
        o = _attn_sample(_heads(qa.astype(F32), N_HEADS_A), _heads(kt, N_HEADS_A), _heads(vt, N_HEADS_A),
                         cache_attn_k[l], cache_attn_v[l], lookup(l, dist_c), lookup(l, dist_n))
        return o.transpose(0, 2, 1, 3).reshape(Bs, Ss, D_ATT).astype(BF16)

    def state_s(l, B):
        return (state_mlstm_C[l], state_mlstm_n[l][:, :, None, :],
                jnp.broadcast_to(state_mlstm_m[l][:, :, None, None], (B, N_HEADS_M, 1, HD_M)))

    y_s, ss = _trunk(x_sample, mods_all[:, Bp:Bp + Bs], mods_fin[Bp:Bp + Bs], wts, attn_s, state_s,
                     ts=Ss, keep=Ss, L=Ss, sort_tokens=False)

    stack = lambda sts, i: jnp.stack([s[i] for s in sts])
    return (y_p, y_s,
            stack(sp, 0), stack(sp, 1), stack(sp, 2), stack(sp, 3), stack(sp, 4),
            stack(ss, 0), stack(ss, 1), stack(ss, 2), stack(ss, 3), stack(ss, 4))
```

```python
import functools

import jax
import jax.numpy as jnp
from jax import lax
from jax.experimental import pallas as pl
from jax.experimental.pallas import tpu as pltpu

F32 = jnp.float32
BF16 = jnp.bfloat16

D_MODEL = 1024
CHUNK = 64
BAND_CHUNKS = 8
BAND_PAST = BAND_CHUNKS * CHUNK
D_ATT = 512
N_HEADS_A = 8
HD_A = 64
MAX_REL = 128
D_MLSTM = 512
N_HEADS_M = 4
HD_M = 128
N_GROUPS = 4
EXP_PER_GROUP = 4
N_EXPERTS = 16
D_EXPERT = 256
EPS = 1e-6
NEG = -1e30

LANES = 128
SUBLANES = 8
N_SLABS = 7
SLAB = 512
ATTN_HEADS = 4
Q_BLOCK = 256
K_WINDOW = BAND_PAST + Q_BLOCK
ROUTER_E0 = N_GROUPS
PACKED = D_MODEL // 2
GATE_GROUP = PACKED // LANES
DMA_UNROLL = 8
MOE_TILE = 512
VMEM_LIMIT = 56 * 1024 * 1024


def _cparams(sem):
    return pltpu.CompilerParams(dimension_semantics=sem, vmem_limit_bytes=VMEM_LIMIT)


def _mm(a, b, dims=None):
    precision = lax.Precision.HIGHEST if a.dtype == F32 and b.dtype == F32 else None
    if dims is None:
        return jnp.dot(a, b, precision=precision, preferred_element_type=F32)
    return lax.dot_general(a, b, dims, precision=precision, preferred_element_type=F32)


def _ada_kernel(c_ref, w_ref, b_ref, o_ref):
    c = c_ref[...]
    o_ref[0] = _mm(c * jax.nn.sigmoid(c), w_ref[0]) + b_ref[0]


def _ada(c, w, b, tn):
    nl, d, n = w.shape
    bc = c.shape[0]
    return pl.pallas_call(
        _ada_kernel,
        out_shape=jax.ShapeDtypeStruct((nl, bc, n), F32),
        grid=(nl, n // tn),
        in_specs=[pl.BlockSpec((bc, d), lambda l, j: (0, 0)),
                  pl.BlockSpec((1, d, tn), lambda l, j: (l, 0, j)),
                  pl.BlockSpec((1, 1, tn), lambda l, j: (l, 0, j))],
        out_specs=pl.BlockSpec((1, bc, tn), lambda l, j: (l, 0, j)),
        compiler_params=_cparams(("parallel", "parallel")),
        name="ada",
    )(c, w, b.reshape(nl, 1, n))


def _norm_mod(x, g, scale, shift):
    r = lax.rsqrt(jnp.mean(x * x, axis=-1, keepdims=True) + EPS)
    return (x * r * g) * (1.0 + scale) + shift


def _inproj_kernel(x_ref, sh_ref, sc_ref, g_ref, w_ref, wg_ref, bg_ref, *rest, first_tail_tile, km_block):
    if km_block:
        wkt_ref, rest = rest[0], rest[1:]
    qa_ref, ka_ref, va_ref, qm_ref, km_ref, vm_ref, om_ref, gc_ref, kt_ref, vt_ref = rest
    hb = _norm_mod(x_ref[0], g_ref[...], sc_ref[...], sh_ref[...]).astype(w_ref.dtype)
    outs = (qa_ref, ka_ref, va_ref, qm_ref, km_ref, vm_ref, om_ref)
    in_tail = pl.program_id(1) >= first_tail_tile
    for j, o_ref in enumerate(outs):
        cols = slice(j * SLAB, (j + 1) * SLAB)
        if j == 4 and km_block:
            t = lax.dot_general(wkt_ref[...], hb, (((1,), (1,)), ((), ())), preferred_element_type=F32)
            t = t * (HD_M ** -0.5)
            for c in range(hb.shape[0] // km_block):
                o_ref[0, c] = t[:, c * km_block:(c + 1) * km_block].astype(BF16)
            continue
        p = _mm(hb, w_ref[:, cols])
        if j == 4:
            p = p * (HD_M ** -0.5)
        o_ref[0] = p.astype(o_ref.dtype)
        if j in (1, 2):
            t_ref = kt_ref if j == 1 else vt_ref

            @pl.when(in_tail)
            def _():
                t_ref[0] = p
    pg = _mm(hb, wg_ref[...]) + bg_ref[...]
    lsig = jnp.minimum(pg, 0.0) - jnp.log1p(jnp.exp(-jnp.abs(pg)))
    lane = lax.broadcasted_iota(jnp.int32, pg.shape, 1)
    gc_ref[0] = jnp.where(lane < N_HEADS_M, pg, jnp.where(lane < 2 * N_HEADS_M, lsig, 0.0))


def _in_proj(x, mods, g, w_main, w_gate, b_gate, ts, keep, w_kt=None, km_block=None):
    B, S, D = x.shape
    n_tiles = S // ts
    n_tail = keep // ts
    first_tail = n_tiles - n_tail
    slab = jax.ShapeDtypeStruct((B, S, SLAB), w_main.dtype)
    tail = jax.ShapeDtypeStruct((B, keep, SLAB), F32)
    slab_spec = pl.BlockSpec((1, ts, SLAB), lambda b, s: (b, s, 0))
    tail_spec = pl.BlockSpec((1, ts, SLAB), lambda b, s: (b, jnp.maximum(s - first_tail, 0), 0))
    slabs = [slab] * N_SLABS
    slab_specs = [slab_spec] * N_SLABS
    in_specs = [pl.BlockSpec((1, ts, D), lambda b, s: (b, s, 0)),
                pl.BlockSpec((None, None, 1, D), lambda b, s: (b, 0, 0, 0)),
                pl.BlockSpec((None, None, 1, D), lambda b, s: (b, 1, 0, 0)),
                pl.BlockSpec((1, D), lambda b, s: (0, 0)),
                pl.BlockSpec((D, N_SLABS * SLAB), lambda b, s: (0, 0)),
                pl.BlockSpec((D, LANES), lambda b, s: (0, 0)),
                pl.BlockSpec((1, LANES), lambda b, s: (0, 0))]
    args = [x, mods, mods, g, w_main, w_gate, b_gate]
    if km_block:
        slabs[4] = jax.ShapeDtypeStruct((B, S // km_block, SLAB, km_block), BF16)
        slab_specs[4] = pl.BlockSpec((1, ts // km_block, SLAB, km_block), lambda b, s: (b, s, 0, 0))
        in_specs.append(pl.BlockSpec((SLAB, D), lambda b, s: (0, 0)))
        args.append(w_kt)
    return pl.pallas_call(
        functools.partial(_inproj_kernel, first_tail_tile=first_tail, km_block=km_block),
        out_shape=tuple(slabs) + (jax.ShapeDtypeStruct((B, S, LANES), F32), tail, tail),
        grid=(B, n_tiles),
        in_specs=in_specs,
        out_specs=tuple(slab_specs) + (pl.BlockSpec((1, ts, LANES), lambda b, s: (b, s, 0)),
                                       tail_spec, tail_spec),
        compiler_params=_cparams(("parallel", "arbitrary")),
        name="in_proj",
    )(*args)


def _attn_prompt_kernel(q_ref, k_ref, v_ref, bias_ref, o_ref):
    i = pl.program_id(2)
    ws = pl.multiple_of(jnp.maximum(i * Q_BLOCK - BAND_PAST, 0), Q_BLOCK)
    variant = jnp.minimum(i, BAND_PAST // Q_BLOCK)
    q = q_ref[0]
    kw = k_ref[0, pl.ds(ws, K_WINDOW), :]
    vw = v_ref[0, pl.ds(ws, K_WINDOW), :]
    lane = lax.broadcasted_iota(jnp.int32, q.shape, 1)
    out = None
    for hh in range(ATTN_HEADS):
        in_head = (lane >= hh * HD_A) & (lane < (hh + 1) * HD_A)
        qh = jnp.where(in_head, q * (HD_A ** -0.5), 0.0).astype(BF16)
        s = lax.dot_general(qh, kw, (((1,), (1,)), ((), ())), preferred_element_type=F32)
        s = s + bias_ref[variant, hh]
        mx = jnp.max(s, axis=-1, keepdims=True)
        p = jnp.exp(s - mx)
        l = jnp.sum(p, axis=-1, keepdims=True)
        o = jnp.dot(p.astype(BF16), vw, preferred_element_type=F32) / l
        out = o if out is None else jnp.where(in_head, o, out)
    o_ref[0] = out.astype(BF16)


def _band_bias_table(table):
    H = table.shape[0]
    a = jnp.arange(Q_BLOCK)[:, None]
    j = jnp.arange(K_WINDOW)[None, :]
    period = Q_BLOCK + K_WINDOW
    u = jnp.arange(period)
    v = jnp.where(u < K_WINDOW, u, u - period)
    tabs = []
    for off in range(0, BAND_PAST + 1, Q_BLOCK):
        prof = table[:, jnp.clip(off - v, -MAX_REL, MAX_REL) + MAX_REL].astype(F32)
        skew = jnp.tile(prof, (1, Q_BLOCK))[:, :Q_BLOCK * (period - 1)].reshape(H, Q_BLOCK, period - 1)
        bias = skew[:, :, :K_WINDOW]
        qc = (a + off) // CHUNK
        kc = j // CHUNK
        valid = (kc <= qc) & (kc >= qc - BAND_CHUNKS)
        tabs.append(jnp.where(valid[None], bias, NEG))
    return jnp.stack(tabs)


def _attn_prompt(q, k, v, bias_tab):
    B, S, _ = q.shape
    nv = bias_tab.shape[0]
    width = ATTN_HEADS * HD_A
    return pl.pallas_call(
        _attn_prompt_kernel,
        out_shape=jax.ShapeDtypeStruct((B, S, D_ATT), BF16),
        grid=(N_HEADS_A // ATTN_HEADS, B, S // Q_BLOCK),
        in_specs=[pl.BlockSpec((1, Q_BLOCK, width), lambda hp, b, i: (b, i, hp)),
                  pl.BlockSpec((1, S, width), lambda hp, b, i: (b, 0, hp)),
                  pl.BlockSpec((1, S, width), lambda hp, b, i: (b, 0, hp)),
                  pl.BlockSpec((nv, ATTN_HEADS, Q_BLOCK, K_WINDOW), lambda hp, b, i: (0, hp, 0, 0))],
        out_specs=pl.BlockSpec((1, Q_BLOCK, width), lambda hp, b, i: (b, i, hp)),
        compiler_params=_cparams(("parallel", "parallel", "arbitrary")),
        name="attn_prompt",
    )(q, k, v, bias_tab)


def _attn_sample_kernel(q_ref, kn_ref, vn_ref, kc_ref, vc_ref, bc_ref, bn_ref, o_ref):
    for h in range(N_HEADS_A):
        q = q_ref[0, h] * (HD_A ** -0.5)
        kn = kn_ref[0, h]
        vn = vn_ref[0, h]
        kc = kc_ref[0, h]
        vc = vc_ref[0, h]
        nt = (((1,), (1,)), ((), ()))
        sc = _mm(q, kc, nt) + bc_ref[h]
        sn = _mm(q, kn, nt) + bn_ref[h]
        mx = jnp.maximum(jnp.max(sc, axis=-1, keepdims=True), jnp.max(sn, axis=-1, keepdims=True))
        pc = jnp.exp(sc - mx)
        pn = jnp.exp(sn - mx)
        l = jnp.sum(pc, axis=-1, keepdims=True) + jnp.sum(pn, axis=-1, keepdims=True)
        o = _mm(pc, vc) + _mm(pn, vn)
        o_ref[0, h] = o / l


def _attn_sample(q, kn, vn, kc, vc, bias_c, bias_n):
    B, H, T, d = q.shape
    W = kc.shape[2]
    new = pl.BlockSpec((1, H, T, d), lambda b: (b, 0, 0, 0))
    old = pl.BlockSpec((1, H, W, d), lambda b: (b, 0, 0, 0))
    return pl.pallas_call(
        _attn_sample_kernel,
        out_shape=jax.ShapeDtypeStruct((B, H, T, d), F32),
        grid=(B,),
        in_specs=[new, new, new, old, old,
                  pl.BlockSpec((H, T, W), lambda b: (0, 0, 0)),
                  pl.BlockSpec((H, T, T), lambda b: (0, 0, 0))],
        out_specs=new,
        compiler_params=_cparams(("parallel",)),
        name="attn_sample",
    )(q, kn, vn, kc, vc, bias_c, bias_n)


def _mlstm_kernel(q_ref, k_ref, v_ref, om_ref, gc_ref, gml_ref, c0_ref, n0_ref, m0_ref,
                  h_ref, c_ref, n_ref, m_ref, *, L, nc):
    c_ref[0] = c0_ref[0]
    n_ref[0] = n0_ref[0]
    m_ref[0] = m0_ref[0]
    tt = lax.broadcasted_iota(jnp.int32, (L, L), 0)
    ss = lax.broadcasted_iota(jnp.int32, (L, L), 1)
    causal = ss <= tt
    eye = ss == tt
    nt = (((1,), (1,)), ((), ()))
    tn = (((0,), (0,)), ((), ()))

    def chunk(ci, carry):
        off = pl.multiple_of(ci * L, L)
        g = gc_ref[0, pl.ds(off, L), :]
        for hh in range(N_HEADS_M):
            cols = slice(hh * HD_M, (hh + 1) * HD_M)
            ig_c = g[:, hh:hh + 1]
            lf_c = g[:, N_HEADS_M + hh:N_HEADS_M + hh + 1]
            ig_r = jnp.sum(jnp.where(eye, ig_c, 0.0), axis=0, keepdims=True)
            lf_r = jnp.sum(jnp.where(eye, lf_c, 0.0), axis=0, keepdims=True)
            b_r = jnp.sum(jnp.where(tt <= ss, lf_c, 0.0), axis=0, keepdims=True)
            b_c = jnp.sum(jnp.where(causal, lf_r, 0.0), axis=1, keepdims=True)
            a_r = ig_r - b_r
            a_c = ig_c - b_c
            m_prev = m_ref[0, hh][:, 0:1]
            Mt = jnp.maximum(jnp.max(jnp.where(causal, a_r, -jnp.inf), axis=1, keepdims=True), m_prev)
            P = jnp.where(causal, jnp.exp(a_r - Mt), 0.0)
            decay = jnp.exp(m_prev - Mt)
            q = q_ref[0, pl.ds(off, L), cols]
            k = k_ref[0, pl.ds(off, L), cols]
            v = v_ref[0, pl.ds(off, L), cols]
            C = c_ref[0, hh]
            n = n_ref[0, hh]
            S = P * _mm(q, k, nt)
            num = decay * _mm(q, C.astype(q.dtype)) + _mm(S.astype(q.dtype), v)
            den = decay * jnp.sum(q.astype(F32) * n, axis=1, keepdims=True) + jnp.sum(S, axis=1, keepdims=True)
            hv = num / jnp.maximum(jnp.abs(den), jnp.exp(-(b_c + Mt)))
            hn = hv * lax.rsqrt(jnp.mean(hv * hv, axis=-1, keepdims=True) + EPS) * gml_ref[:, cols]
            om = om_ref[0, pl.ds(off, L), cols].astype(F32)
            h_ref[0, pl.ds(off, L), cols] = (hn * jax.nn.sigmoid(om)).astype(h_ref.dtype)
            M_last = Mt[L - 1:L, :]
            b_last = b_c[L - 1:L, :]
            kw = k.astype(F32) * jnp.exp(a_c - M_last)
            cd = jnp.exp(m_prev - M_last)
            c_ref[0, hh] = cd * C + _mm(kw.astype(q.dtype), v, tn)
            n_ref[0, hh] = cd * n + jnp.sum(kw, axis=0, keepdims=True)
            m_ref[0, hh] = jnp.broadcast_to(b_last + M_last, (1, HD_M))
        return carry

    lax.fori_loop(0, nc, chunk, 0)


def _mlstm(q, k, v, om, gc, gml, c0, n0, m0, L):
    B, S, _ = q.shape
    H = N_HEADS_M
    seq = pl.BlockSpec((1, S, D_MLSTM), lambda b: (b, 0, 0))
    cst = pl.BlockSpec((1, H, HD_M, HD_M), lambda b: (b, 0, 0, 0))
    vec = pl.BlockSpec((1, H, 1, HD_M), lambda b: (b, 0, 0, 0))
    return pl.pallas_call(
        functools.partial(_mlstm_kernel, L=L, nc=S // L),
        out_shape=(jax.ShapeDtypeStruct((B, S, D_MLSTM), q.dtype),
                   jax.ShapeDtypeStruct((B, H, HD_M, HD_M), F32),
                   jax.ShapeDtypeStruct((B, H, 1, HD_M), F32),
                   jax.ShapeDtypeStruct((B, H, 1, HD_M), F32)),
        grid=(B,),
        in_specs=[seq, seq, seq, seq,
                  pl.BlockSpec((1, S, LANES), lambda b: (b, 0, 0)),
                  pl.BlockSpec((1, D_MLSTM), lambda b: (0, 0)),
                  cst, vec, vec],
        out_specs=(seq, cst, vec, vec),
        compiler_params=_cparams(("parallel",)),
        name="mlstm",
    )(q, k, v, om, gc, gml, c0, n0, m0)


def _split3(x):
    hi = x.astype(BF16)
    r1 = x - hi.astype(F32)
    mid = r1.astype(BF16)
    lo = (r1 - mid.astype(F32)).astype(BF16)
    return hi, mid, lo


def _mlstm_long_kernel(q_ref, kt_ref, v_ref, om_ref, gc_ref, gml_ref, ca0_ref, m0_ref,
                       h_ref, ca_ref, m_ref, *, L, nc):
    ca_ref[0] = ca0_ref[0]
    m_ref[0] = m0_ref[0]
    tt = lax.broadcasted_iota(jnp.int32, (L, L), 0)
    ss = lax.broadcasted_iota(jnp.int32, (L, L), 1)
    causal = ss <= tt
    lower = jnp.where(causal, 1.0, 0.0).astype(BF16)
    upper = jnp.where(tt <= ss, 1.0, 0.0).astype(BF16)
    ones_blk = jnp.ones((L, HD_M), BF16)
    mean_mat = jnp.full((HD_M, HD_M), 1.0 / HD_M, BF16)
    lane = lax.broadcasted_iota(jnp.int32, (L, LANES), 1)
    H = N_HEADS_M

    def chunk(ci, carry):
        off = pl.multiple_of(ci * L, L)
        g = gc_ref[0, pl.ds(off, L), :]
        g_rows = g.T[:2 * H, :]
        lf_c = jnp.where(lane < H, pltpu.roll(g, LANES - H, axis=1), 0.0)
        bc3 = jnp.dot(lower, jnp.concatenate(_split3(lf_c), axis=1), preferred_element_type=F32)
        b_c = bc3[:, :LANES] + bc3[:, LANES:2 * LANES] + bc3[:, 2 * LANES:]
        terms = [t.astype(F32) for t in _split3(g_rows)] + [jnp.zeros((SUBLANES, L), F32)]
        rows3 = jnp.concatenate(terms, axis=0).astype(BF16)
        br3 = jnp.dot(rows3, upper, preferred_element_type=F32)
        b_r = br3[:2 * H] + br3[2 * H:4 * H] + br3[4 * H:6 * H]
        a_r4 = g_rows[:H] - b_r[H:]
        for hh in range(H):
            cols = slice(hh * HD_M, (hh + 1) * HD_M)
            m_prev = m_ref[0, hh][:, 0:1]
            a_r = a_r4[hh:hh + 1, :]
            Mt = jnp.maximum(jnp.max(jnp.where(causal, a_r, -jnp.inf), axis=1, keepdims=True), m_prev)
            Mb = jnp.broadcast_to(Mt, (L, LANES))
            P = jnp.where(causal, jnp.exp(a_r - jnp.concatenate([Mb] * (L // LANES), axis=1)), 0.0)
            q = q_ref[0, pl.ds(off, L), cols]
            kt = kt_ref[0, ci, cols, :]
            v = v_ref[0, pl.ds(off, L), cols]
            V = jnp.concatenate([v, ones_blk], axis=1)
            CA = ca_ref[0, hh]
            S = (P * jnp.dot(q, kt, preferred_element_type=F32)).astype(BF16)
            inter = jnp.dot(q, CA.astype(BF16), preferred_element_type=F32)
            intra = jnp.dot(S, V, preferred_element_type=F32)
            decay = jnp.exp(m_prev - Mb)
            num = decay * inter[:, :HD_M] + intra[:, :HD_M]
            den = decay * inter[:, HD_M:] + intra[:, HD_M:]
            b_cb = jnp.broadcast_to(b_c[:, hh:hh + 1], (L, LANES))
            hv = num / jnp.maximum(jnp.abs(den), jnp.exp(-(b_cb + Mb)))
            sq = hv * hv
            sq_hi = sq.astype(BF16)
            sq_lo = (sq - sq_hi.astype(F32)).astype(BF16)
            msq = (jnp.dot(sq_hi, mean_mat, preferred_element_type=F32)
                   + jnp.dot(sq_lo, mean_mat, preferred_element_type=F32))
            hn = hv * lax.rsqrt(msq + EPS) * gml_ref[:, cols]
            om = om_ref[0, pl.ds(off, L), cols].astype(F32)
            h_ref[0, pl.ds(off, L), cols] = (hn * jax.nn.sigmoid(om)).astype(BF16)
            M_last = jnp.maximum(jnp.max(a_r, axis=1, keepdims=True), m_prev)
            kw = (kt.astype(F32) * jnp.exp(a_r - M_last)).astype(BF16)
            cd = jnp.exp(m_prev - M_last)
            ca_ref[0, hh] = cd * CA + jnp.dot(kw, V, preferred_element_type=F32)
            m_ref[0, hh] = jnp.broadcast_to(b_r[H + hh:H + hh + 1, L - 1:L] + M_last, (1, HD_M))
        return carry

    lax.fori_loop(0, nc, chunk, 0)


def _mlstm_long(q, kt, v, om, gc, gml, ca0, m0, L):
    B, S, _ = q.shape
    H = N_HEADS_M
    seq = pl.BlockSpec((1, S, D_MLSTM), lambda b: (b, 0, 0))
    cst = pl.BlockSpec((1, H, HD_M, 2 * HD_M), lambda b: (b, 0, 0, 0))
    vec = pl.BlockSpec((1, H, 1, HD_M), lambda b: (b, 0, 0, 0))
    return pl.pallas_call(
        functools.partial(_mlstm_long_kernel, L=L, nc=S // L),
        out_shape=(jax.ShapeDtypeStruct((B, S, D_MLSTM), BF16),
                   jax.ShapeDtypeStruct((B, H, HD_M, 2 * HD_M), F32),
                   jax.ShapeDtypeStruct((B, H, 1, HD_M), F32)),
        grid=(B,),
        in_specs=[seq,
                  pl.BlockSpec((1, S // L, D_MLSTM, L), lambda b: (b, 0, 0, 0)),
                  seq, seq,
                  pl.BlockSpec((1, S, LANES), lambda b: (b, 0, 0)),
                  pl.BlockSpec((1, D_MLSTM), lambda b: (0, 0)),
                  cst, vec],
        out_specs=(seq, cst, vec),
        compiler_params=_cparams(("parallel",)),
        name="mlstm_long",
    )(q, kt, v, om, gc, gml, ca0, m0)


def _mix_residual_norm(att_ref, hm_ref, x_ref, gt_ref, sh_ref, sc_ref, g_ref, wo_ref):
    mix = _mm(att_ref[0], wo_ref[:D_ATT, :]) + _mm(hm_ref[0], wo_ref[D_ATT:, :])
    x1 = x_ref[0] + gt_ref[...] * mix
    return x1, _norm_mod(x1, g_ref[...], sc_ref[...], sh_ref[...]).astype(wo_ref.dtype)


def _router(hb, wr_ref, br_ref):
    logit = _mm(hb, wr_ref[...]) + br_ref[...]
    lane = lax.broadcasted_iota(jnp.int32, logit.shape, 1)
    big = jnp.int32(LANES)
    is_g = lane < N_GROUPS
    mg = jnp.max(jnp.where(is_g, logit, -jnp.inf), axis=-1, keepdims=True)
    eg = jnp.where(is_g, jnp.exp(logit - mg), 0.0)
    pg = eg / jnp.sum(eg, axis=-1, keepdims=True)
    p_top = jnp.max(pg, axis=-1, keepdims=True)
    g_idx = jnp.min(jnp.where(is_g & (pg == p_top), lane, big), axis=-1, keepdims=True)
    e_lo = ROUTER_E0 + g_idx * EXP_PER_GROUP
    sel = (lane >= e_lo) & (lane < e_lo + EXP_PER_GROUP)
    me = jnp.max(jnp.where(sel, logit, -jnp.inf), axis=-1, keepdims=True)
    ee = jnp.where(sel, jnp.exp(logit - me), 0.0)
    pe = ee / jnp.sum(ee, axis=-1, keepdims=True)
    v1 = jnp.max(jnp.where(sel, pe, -1.0), axis=-1, keepdims=True)
    i1 = jnp.min(jnp.where(sel & (pe == v1), lane, big), axis=-1, keepdims=True)
    sel2 = sel & (lane != i1)
    v2 = jnp.max(jnp.where(sel2, pe, -1.0), axis=-1, keepdims=True)
    i2 = jnp.min(jnp.where(sel2 & (pe == v2), lane, big), axis=-1, keepdims=True)
    tot = v1 + v2
    w1 = v1 / tot * p_top
    w2 = v2 / tot * p_top
    return jnp.where(lane == i1, w1, jnp.where(lane == i2, w2, 0.0)), g_idx


def _outproj_kernel(att_ref, hm_ref, x_ref, gt_ref, sh_ref, sc_ref, g_ref, wo_ref, wr_ref, br_ref,
                    x1_ref, h2_ref, gates_ref):
    x1, hb = _mix_residual_norm(att_ref, hm_ref, x_ref, gt_ref, sh_ref, sc_ref, g_ref, wo_ref)
    x1_ref[0] = x1
    h2_ref[0] = hb
    gates_ref[0] = _router(hb, wr_ref, br_ref)[0]


def _pack_bf16_pairs(lo, hi):
    lo_bits = lax.bitcast_convert_type(lo.astype(BF16).astype(F32), jnp.uint32)
    hi_bits = lax.bitcast_convert_type(hi.astype(BF16).astype(F32), jnp.uint32)
    return (hi_bits & jnp.uint32(0xFFFF0000)) | (lo_bits >> 16)


def _unpack_bf16_pairs(packed):
    lo = lax.bitcast_convert_type(packed << 16, F32)
    hi = lax.bitcast_convert_type(packed & jnp.uint32(0xFFFF0000), F32)
    return lo, hi


def _outproj_sort_kernel(att_ref, hm_ref, x_ref, gt_ref, sh_ref, sc_ref, g_ref, wo_ref, wr_ref, br_ref,
                         x1_ref, row_ref, idx_ref, cnt_ref, fill_ref, *, region_rows):
    first = (pl.program_id(0) == 0) & (pl.program_id(1) == 0)

    @pl.when(first)
    def _():
        fill_ref[...] = jnp.zeros_like(fill_ref)

    x1, hb = _mix_residual_norm(att_ref, hm_ref, x_ref, gt_ref, sh_ref, sc_ref, g_ref, wo_ref)
    x1_ref[0] = x1
    gates, g_idx = _router(hb, wr_ref, br_ref)
    hf = hb.astype(F32)
    half = hf.shape[1] // 2
    ts = gates.shape[0]
    packed = _pack_bf16_pairs(hf[:, :half], hf[:, half:])
    for k in range(SUBLANES):
        if k < GATE_GROUP:
            piece = packed[:, k * LANES:(k + 1) * LANES]
        elif k == GATE_GROUP:
            piece = lax.bitcast_convert_type(gates, jnp.uint32)
        else:
            piece = jnp.zeros((ts, LANES), jnp.uint32)
        row_ref[_lane_group(k, ts), :] = piece
    lane = lax.broadcasted_iota(jnp.int32, gates.shape, 1)
    onehot = lane == g_idx
    tt = lax.broadcasted_iota(jnp.int32, (ts, ts), 0)
    ss = lax.broadcasted_iota(jnp.int32, (ts, ts), 1)
    before = jnp.where(ss < tt, 1.0, 0.0).astype(BF16)
    rank = jnp.dot(before, jnp.where(onehot, 1.0, 0.0).astype(BF16), preferred_element_type=F32)
    fill = fill_ref[...]
    slot = jnp.sum(jnp.where(onehot, rank + fill, 0.0), axis=-1, keepdims=True)
    slot = slot + g_idx.astype(F32) * float(region_rows)
    slot_row = jnp.sum(jnp.where(ss == tt, slot, 0.0), axis=0, keepdims=True)
    idx_ref[...] = slot_row.astype(jnp.int32)
    fill = fill + jnp.sum(jnp.where(onehot, 1.0, 0.0), axis=0, keepdims=True)
    fill_ref[...] = fill
    cnt_ref[...] = fill.astype(jnp.int32)


def _out_proj(att, hm, x, mods, g, w_out, w_router, b_router, ts, region_rows=None):
    B, S, D = x.shape
    nt = S // ts
    tile = lambda w: pl.BlockSpec((1, ts, w), lambda b, s: (b, s, 0))
    mod = lambda i: pl.BlockSpec((None, None, 1, D), lambda b, s: (b, i, 0, 0))
    const = lambda shape: pl.BlockSpec(shape, lambda b, s: (0, 0))
    in_specs = [tile(D_ATT), tile(D_MLSTM), tile(D), mod(2), mod(3), mod(4), const((1, D)),
                const((D, D)), const((D, LANES)), const((1, LANES))]
    args = (att, hm, x, mods, mods, mods, g, w_out, w_router, b_router)
    if region_rows is None:
        return pl.pallas_call(
            _outproj_kernel,
            out_shape=(jax.ShapeDtypeStruct((B, S, D), F32),
                       jax.ShapeDtypeStruct((B, S, D), w_out.dtype),
                       jax.ShapeDtypeStruct((B, S, LANES), F32)),
            grid=(B, nt), in_specs=in_specs,
            out_specs=(tile(D), tile(D), tile(LANES)),
            compiler_params=_cparams(("parallel", "parallel")),
            name="out_proj",
        )(*args)
    return pl.pallas_call(
        functools.partial(_outproj_sort_kernel, region_rows=region_rows),
        out_shape=(jax.ShapeDtypeStruct((B, S, D), F32),
                   jax.ShapeDtypeStruct((B, S * SUBLANES, LANES), jnp.uint32),
                   jax.ShapeDtypeStruct((B * nt, 1, ts), jnp.int32),
                   jax.ShapeDtypeStruct((1, LANES), jnp.int32)),
        grid=(B, nt), in_specs=in_specs,
        out_specs=(tile(D), pl.BlockSpec((None, ts * SUBLANES, LANES), lambda b, s: (b, s, 0)),
                   pl.BlockSpec((None, 1, ts), lambda b, s: (b * nt + s, 0, 0)),
                   const((1, LANES))),
        scratch_shapes=[pltpu.VMEM((1, LANES), F32)],
        compiler_params=_cparams(("arbitrary", "arbitrary")),
        name="out_proj_sort",
    )(*args)


def _moe_kernel(h_ref, gates_ref, x_ref, gt_ref, wgu_ref, wd_ref, o_ref, acc_ref):
    e = pl.program_id(2)

    @pl.when(e == 0)
    def _():
        acc_ref[...] = jnp.zeros_like(acc_ref)

    h = h_ref[0]
    au = _mm(h, wgu_ref[0])
    a = au[:, :D_EXPERT]
    u = au[:, D_EXPERT:]
    act = (a * jax.nn.sigmoid(a) * u).astype(h.dtype)
    y = _mm(act, wd_ref[0])
    gates = gates_ref[0]
    lane = lax.broadcasted_iota(jnp.int32, gates.shape, 1)
    ge = jnp.sum(jnp.where(lane == ROUTER_E0 + e, gates, 0.0), axis=-1, keepdims=True)
    acc_ref[...] += ge * y

    @pl.when(e == N_EXPERTS - 1)
    def _():
        o_ref[0] = x_ref[0] + gt_ref[0] * acc_ref[...]


def _moe(h2, gates, x1, mods, w_gu, w_down, ts):
    B, S, D = x1.shape
    tile = lambda w: pl.BlockSpec((1, ts, w), lambda b, s, e: (b, s, 0))
    return pl.pallas_call(
        _moe_kernel,
        out_shape=jax.ShapeDtypeStruct((B, S, D), F32),
        grid=(B, S // ts, N_EXPERTS),
        in_specs=[tile(D), tile(LANES), tile(D), tile(D),
                  pl.BlockSpec((1, D, 2 * D_EXPERT), lambda b, s, e: (e, 0, 0)),
                  pl.BlockSpec((1, D_EXPERT, D), lambda b, s, e: (e, 0, 0))],
        out_specs=tile(D),
        scratch_shapes=[pltpu.VMEM((ts, D), F32)],
        compiler_params=_cparams(("parallel", "parallel", "arbitrary")),
        name="moe",
    )(h2, gates, x1, mods, w_gu, w_down)


def _token_rows(token):
    return pl.ds(pl.multiple_of(token * SUBLANES, SUBLANES), SUBLANES)


def _lane_group(k, n, first_token=0):
    return pl.ds(first_token * SUBLANES + k, n, stride=SUBLANES)


def _for_each_token(ts, fn):
    def body(j, c):
        for u in range(DMA_UNROLL):
            fn(j * DMA_UNROLL + u, u % 2)
        return c
    lax.fori_loop(0, ts // DMA_UNROLL, body, 0)


def _dispatch_kernel(cnt_ref, row_ref, idx_ref, hs_ref, buf_ref, zero_ref, sem_ref, zsem_ref, *, region_rows, tm):
    i = pl.program_id(0)
    n = pl.num_programs(0)
    ts = idx_ref.shape[1]
    slot = i % 2

    def wait_slot(s):
        pltpu.make_async_copy(buf_ref.at[s], hs_ref.at[pl.ds(0, ts * SUBLANES), :], sem_ref.at[s]).wait()

    @pl.when(i >= 2)
    def _():
        wait_slot(slot)

    buf_ref[slot] = row_ref[...]

    def issue(r, priority):
        pltpu.make_async_copy(buf_ref.at[slot, _token_rows(r), :], hs_ref.at[_token_rows(idx_ref[0, r]), :],
                              sem_ref.at[slot]).start(priority=priority)
    _for_each_token(ts, issue)

    @pl.when(i == n - 1)
    def _():
        wait_slot(slot)

        @pl.when(n >= 2)
        def _():
            wait_slot(1 - slot)

        zero_ref[...] = jnp.zeros_like(zero_ref)

        def tail_copy(g):
            first = g * region_rows + cnt_ref[g]
            dst = hs_ref.at[pl.ds(pl.multiple_of(first * SUBLANES, SUBLANES), tm * SUBLANES), :]
            return pltpu.make_async_copy(zero_ref, dst, zsem_ref.at[g])

        for g in range(N_GROUPS):
            tail_copy(g).start()
        for g in range(N_GROUPS):
            tail_copy(g).wait()


def _dispatch(rows, idx, counts, ts, tm, region_rows):
    T = rows.shape[0] // SUBLANES
    grid_spec = pltpu.PrefetchScalarGridSpec(
        num_scalar_prefetch=1,
        grid=(T // ts,),
        in_specs=[pl.BlockSpec((ts * SUBLANES, LANES), lambda i, cnt: (i, 0)),
                  pl.BlockSpec((None, 1, ts), lambda i, cnt: (i, 0, 0), memory_space=pltpu.SMEM)],
        out_specs=pl.BlockSpec(memory_space=pl.ANY),
        scratch_shapes=[pltpu.VMEM((2, ts * SUBLANES, LANES), jnp.uint32),
                        pltpu.VMEM((tm * SUBLANES, LANES), jnp.uint32),
                        pltpu.SemaphoreType.DMA((2,)),
                        pltpu.SemaphoreType.DMA((N_GROUPS,))])
    return pl.pallas_call(
        functools.partial(_dispatch_kernel, region_rows=region_rows, tm=tm),
        out_shape=jax.ShapeDtypeStruct((N_GROUPS * region_rows * SUBLANES, LANES), jnp.uint32),
        grid_spec=grid_spec,
        compiler_params=_cparams(("arbitrary",)),
        name="moe_dispatch",
    )(counts, rows, idx)


def _moe_group_kernel(blk_ref, grp_ref, live_ref, hs_ref, wgu_ref, wd_ref, o_ref):
    i = pl.program_id(0)

    @pl.when(live_ref[i] == 0)
    def _():
        o_ref[...] = jnp.zeros_like(o_ref)

    @pl.when(live_ref[i] != 0)
    def _():
        tm = hs_ref.shape[0] // SUBLANES
        halves = [_unpack_bf16_pairs(hs_ref[_lane_group(k, tm), :]) for k in range(PACKED // LANES)]
        lo = jnp.concatenate([h[0] for h in halves], axis=1).astype(BF16)
        hi = jnp.concatenate([h[1] for h in halves], axis=1).astype(BF16)
        gates = lax.bitcast_convert_type(hs_ref[_lane_group(GATE_GROUP, tm), :], F32)
        lane = lax.broadcasted_iota(jnp.int32, gates.shape, 1)
        e0 = ROUTER_E0 + grp_ref[i] * EXP_PER_GROUP
        y = jnp.zeros((tm, D_MODEL), F32)
        for e in range(EXP_PER_GROUP):
            au = (jnp.dot(lo, wgu_ref[e, :PACKED, :], preferred_element_type=F32)
                  + jnp.dot(hi, wgu_ref[e, PACKED:, :], preferred_element_type=F32))
            a = au[:, :D_EXPERT]
            u = au[:, D_EXPERT:]
            act = (a * jax.nn.sigmoid(a) * u).astype(BF16)
            ge = jnp.sum(jnp.where(lane == e0 + e, gates, 0.0), axis=-1, keepdims=True)
            y = y + ge * jnp.dot(act, wd_ref[e], preferred_element_type=F32)
        for k in range(D_MODEL // LANES):
            o_ref[_lane_group(k, tm), :] = y[:, k * LANES:(k + 1) * LANES]


def _moe_group(hs, blk, grp, live, w_gu, w_down, tm):
    n_steps = blk.shape[0]
    rows = tm * SUBLANES
    n_blocks = hs.shape[0] // rows
    grid_spec = pltpu.PrefetchScalarGridSpec(
        num_scalar_prefetch=3,
        grid=(n_steps,),
        in_specs=[pl.BlockSpec((rows, LANES), lambda i, blk, grp, live: (blk[i], 0)),
                  pl.BlockSpec((None, EXP_PER_GROUP, D_MODEL, 2 * D_EXPERT), lambda i, blk, grp, live: (grp[i], 0, 0, 0)),
                  pl.BlockSpec((None, EXP_PER_GROUP, D_EXPERT, D_MODEL), lambda i, blk, grp, live: (grp[i], 0, 0, 0))],
        out_specs=pl.BlockSpec((rows, LANES),
                               lambda i, blk, grp, live: (jnp.where(live[i] != 0, blk[i], n_blocks), 0)))
    return pl.pallas_call(
        _moe_group_kernel,
        out_shape=jax.ShapeDtypeStruct(((n_blocks + 1) * rows, LANES), F32),
        grid_spec=grid_spec,
        compiler_params=_cparams(("arbitrary",)),
        name="moe_group",
    )(blk, grp, live, hs, w_gu, w_down)


def _combine_kernel(x1_ref, gt_ref, idx0_ref, idx1_ref, ys_ref, *rest, final):
    if final:
        shf_ref, scf_ref, gf_ref, o_ref, buf_ref, sem_ref = rest
    else:
        o_ref, buf_ref, sem_ref = rest
    i = pl.program_id(0)
    n = pl.num_programs(0)
    ts = idx0_ref.shape[1]
    slot = i % 2

    def issue(idx_ref, s):
        def one(r, priority):
            pltpu.make_async_copy(ys_ref.at[_token_rows(idx_ref[0, r]), :], buf_ref.at[s, _token_rows(r), :],
                                  sem_ref.at[s]).start(priority=priority)
        _for_each_token(ts, one)

    @pl.when(i == 0)
    def _():
        issue(idx0_ref, 0)

    @pl.when(i + 1 < n)
    def _():
        issue(idx1_ref, 1 - slot)

    pltpu.make_async_copy(ys_ref.at[pl.ds(0, ts * SUBLANES), :], buf_ref.at[slot], sem_ref.at[slot]).wait()

    g = gt_ref[...]
    for k in range(D_MODEL // LANES):
        cols = slice(k * LANES, (k + 1) * LANES)
        y = buf_ref[slot, _lane_group(k, ts), :]
        o_ref[:, cols] = x1_ref[:, cols] + g[:, cols] * y
    if final:
        o_ref[...] = _norm_mod(o_ref[...], gf_ref[...], scf_ref[...], shf_ref[...])


def _combine(x1, mods, idx, ys, ts, final=None):
    B, S, D = x1.shape
    nt = S // ts
    n = B * nt
    in_specs = [pl.BlockSpec((ts, D), lambda i: (i, 0)),
                pl.BlockSpec((None, None, 1, D), lambda i: (i // nt, 5, 0, 0)),
                pl.BlockSpec((None, 1, ts), lambda i: (i, 0, 0), memory_space=pltpu.SMEM),
                pl.BlockSpec((None, 1, ts), lambda i: (jnp.minimum(i + 1, n - 1), 0, 0), memory_space=pltpu.SMEM),
                pl.BlockSpec(memory_space=pl.ANY)]
    args = [x1.reshape(B * S, D), mods, idx, idx, ys]
    if final is not None:
        mods_f, g_final = final
        in_specs += [pl.BlockSpec((None, None, 1, D), lambda i: (i // nt, 0, 0, 0)),
                     pl.BlockSpec((None, None, 1, D), lambda i: (i // nt, 1, 0, 0)),
                     pl.BlockSpec((1, D), lambda i: (0, 0))]
        args += [mods_f, mods_f, g_final]
    out = pl.pallas_call(
        functools.partial(_combine_kernel, final=final is not None),
        out_shape=jax.ShapeDtypeStruct((B * S, D), F32),
        grid=(n,),
        in_specs=in_specs,
        out_specs=pl.BlockSpec((ts, D), lambda i: (i, 0)),
        scratch_shapes=[pltpu.VMEM((2, ts * SUBLANES, LANES), F32), pltpu.SemaphoreType.DMA((2,))],
        compiler_params=_cparams(("arbitrary",)),
        name="moe_combine",
    )(*args)
    return out.reshape(B, S, D)


def _tile_plan(counts, tm, n_steps, region_blocks):
    tiles = (counts + tm - 1) // tm
    ends = jnp.cumsum(tiles)
    total = ends[-1]
    step = jnp.minimum(jnp.arange(n_steps, dtype=jnp.int32), jnp.maximum(total - 1, 0))
    grp = jnp.sum((step[:, None] >= ends[None, :]).astype(jnp.int32), axis=1)
    grp = jnp.minimum(grp, N_GROUPS - 1)
    starts = ends - tiles
    blk = grp * region_blocks + (step - starts[grp])
    live = (jnp.arange(n_steps, dtype=jnp.int32) < total).astype(jnp.int32)
    return blk.astype(jnp.int32), grp.astype(jnp.int32), live


def _moe_sorted(rows, idx, counts, x1, mods, w_gu, w_down, ts, tm, final):
    B, S, D = x1.shape
    T = B * S
    region_rows = T + tm
    n_steps = T // tm + N_GROUPS
    cnt = counts[0, :N_GROUPS]
    hs = _dispatch(rows.reshape(T * SUBLANES, LANES), idx, cnt, ts, tm, region_rows)
    blk, grp, live = _tile_plan(cnt, tm, n_steps, region_rows // tm)
    ys = _moe_group(hs, blk, grp, live, w_gu, w_down, tm)
    return _combine(x1, mods, idx, ys, ts, final)


def _final_kernel(x_ref, sh_ref, sc_ref, g_ref, o_ref):
    o_ref[0] = _norm_mod(x_ref[0], g_ref[...], sc_ref[...], sh_ref[...])


def _final(x, mods_f, g, ts):
    B, S, D = x.shape
    return pl.pallas_call(
        _final_kernel,
        out_shape=jax.ShapeDtypeStruct((B, S, D), F32),
        grid=(B, S // ts),
        in_specs=[pl.BlockSpec((1, ts, D), lambda b, s: (b, s, 0)),
                  pl.BlockSpec((None, None, 1, D), lambda b, s: (b, 0, 0, 0)),
                  pl.BlockSpec((None, None, 1, D), lambda b, s: (b, 1, 0, 0)),
                  pl.BlockSpec((1, D), lambda b, s: (0, 0))],
        out_specs=pl.BlockSpec((1, ts, D), lambda b, s: (b, s, 0)),
        compiler_params=_cparams(("parallel", "parallel")),
        name="final_norm",
    )(x, mods_f, mods_f, g)


def _heads(t, n):
    B, S, _ = t.shape
    return t.reshape(B, S, n, -1).transpose(0, 2, 1, 3)


def _trunk(x, mods, mods_f, wts, attn_fn, state_fn, ts, keep, L, long_sequence):
    B, S, D = x.shape
    sort_tokens = long_sequence
    states = []
    depth = mods.shape[0]
    final = (mods_f.reshape(B, 2, 1, D), wts[0]["g_final"])
    for l in range(depth):
        w = wts[l]
        m = mods[l].reshape(B, 6, 1, D)
        qa, ka, va, qm, km, vm, om, gc, kt, vt = _in_proj(
            x, m, w["g1"], w["w_main"], w["w_gate"], w["b_gate"], ts, keep,
            w["w_kt"] if long_sequence else None, L if long_sequence else None)
        att = attn_fn(l, qa, ka, va, kt, vt)
        c0, n0, m0 = state_fn(l, B)
        if long_sequence:
            n_cols = jnp.broadcast_to(jnp.swapaxes(n0, 2, 3), (B, N_HEADS_M, HD_M, HD_M))
            hm, ca, mm = _mlstm_long(qm, km, vm, om, gc, w["g_mlstm"], jnp.concatenate([c0, n_cols], axis=-1), m0, L)
            C, n = ca[..., :HD_M], ca[:, :, None, :, HD_M]
        else:
            hm, C, n, mm = _mlstm(qm, km, vm, om, gc, w["g_mlstm"], c0, n0, m0, L)
        if sort_tokens:
            x1, rows, idx, counts = _out_proj(att, hm, x, m, w["g2"], w["w_out"], w["w_router"], w["b_router"], ts,
                                              region_rows=B * S + MOE_TILE)
            x = _moe_sorted(rows, idx, counts, x1, m, w["w_gu4"], w["w_down4"], ts, MOE_TILE,
                            final if l == depth - 1 else None)
        else:
            x1, h2, gates = _out_proj(att, hm, x, m, w["g2"], w["w_out"], w["w_router"], w["b_router"], ts)
            flat = lambda t: t.reshape(1, B * S, t.shape[-1])
            gate = jnp.broadcast_to(m[:, 5], (B, S, D))
            x = _moe(flat(h2), flat(gates), flat(x1), flat(gate), w["w_gu"], w["w_down_e"], B * S).reshape(B, S, D)
        states.append((_heads(kt, N_HEADS_A), _heads(vt, N_HEADS_A), C, n[:, :, 0, :], mm[:, :, 0, 0]))
    if not sort_tokens:
        x = _final(x, *final, ts)
    return x, states


def kernel(x_prompt, x_sample, c_prompt, c_sample, cache_attn_k, cache_attn_v, state_mlstm_C, state_mlstm_n,
           state_mlstm_m, w_ada, b_ada, g_norm, w_in, b_igate, b_fgate, rel_bias, g_mlstm, w_out, w_router_g,
           b_router_g, w_router_e, b_router_e, w_gate, w_up, w_down, w_ada_f, b_ada_f, g_final):
    depth = w_ada.shape[0]
    Bp, Sp, D = x_prompt.shape
    Bs, Ss, _ = x_sample.shape

    c_all = jnp.concatenate([c_prompt, c_sample], axis=0)
    pad = (-c_all.shape[0]) % 16
    c_all = jnp.pad(c_all, ((0, pad), (0, 0)))
    mods_all = _ada(c_all, w_ada, b_ada, 1536)
    mods_fin = _ada(c_all, w_ada_f[None], b_ada_f[None], 1024)[0]

    def layer_weights(l, dtype):
        wl = w_in[l]
        n_main = N_SLABS * SLAB
        w_gate_cols = jnp.pad(wl[:, n_main:], ((0, 0), (0, LANES - 2 * N_HEADS_M)))
        b_gate_cols = jnp.pad(jnp.concatenate([b_igate[l], b_fgate[l]]), (0, LANES - 2 * N_HEADS_M))
        w_r = jnp.concatenate([w_router_g[l], w_router_e[l]], axis=1)
        b_r = jnp.concatenate([b_router_g[l], b_router_e[l]])
        npad = LANES - N_GROUPS - N_EXPERTS
        w_gu = jnp.concatenate([w_gate[l], w_up[l]], axis=-1).astype(dtype)
        w_dn = w_down[l].astype(dtype)
        return dict(
            g1=g_norm[l, 0][None], g2=g_norm[l, 1][None], g_final=g_final[None],
            w_main=wl[:, :n_main].astype(dtype), w_gate=w_gate_cols.astype(dtype), b_gate=b_gate_cols[None],
            w_kt=wl[:, 4 * SLAB:5 * SLAB].T.astype(dtype),
            g_mlstm=g_mlstm[l][None], w_out=w_out[l].astype(dtype),
            w_router=jnp.pad(w_r, ((0, 0), (0, npad))).astype(dtype), b_router=jnp.pad(b_r, (0, npad))[None],
            w_gu=w_gu, w_down_e=w_dn,
            w_gu4=w_gu.reshape(N_GROUPS, EXP_PER_GROUP, D_MODEL, 2 * D_EXPERT),
            w_down4=w_dn.reshape(N_GROUPS, EXP_PER_GROUP, D_EXPERT, D_MODEL))

    wts = [layer_weights(l, BF16) for l in range(depth)]
    wts_full = [layer_weights(l, F32) for l in range(depth)]

    bias_tabs = [_band_bias_table(rel_bias[l]) for l in range(depth)]

    def attn_p(l, qa, ka, va, kt, vt):
        return _attn_prompt(qa, ka, va, bias_tabs[l])

    def state_p(l, B):
        return (jnp.zeros((B, N_HEADS_M, HD_M, HD_M), F32), jnp.zeros((B, N_HEADS_M, 1, HD_M), F32),
                jnp.zeros((B, N_HEADS_M, 1, HD_M), F32))

    keep_p = min(BAND_PAST, Sp)
    y_p, sp = _trunk(x_prompt, mods_all[:, :Bp], mods_fin[:Bp], wts, attn_p, state_p,
                     ts=512, keep=keep_p, L=512, long_sequence=True)

    Wc = cache_attn_k.shape[3]
    tq = jnp.arange(Ss)[:, None]
    dist_c = Wc + tq - jnp.arange(Wc)[None, :]
    dist_n = tq - jnp.arange(Ss)[None, :]
    lookup = lambda l, dist: rel_bias[l][:, jnp.clip(dist, -MAX_REL, MAX_REL) + MAX_REL].astype(F32)

    def attn_s(l, qa, ka, va, kt, vt):
        o = _attn_sample(_heads(qa, N_HEADS_A), _heads(kt, N_HEADS_A), _heads(vt, N_HEADS_A),
                         cache_attn_k[l], cache_attn_v[l], lookup(l, dist_c), lookup(l, dist_n))
        return o.transpose(0, 2, 1, 3).reshape(Bs, Ss, D_ATT)

    def state_s(l, B):
        return (state_mlstm_C[l], state_mlstm_n[l][:, :, None, :],
                jnp.broadcast_to(state_mlstm_m[l][:, :, None, None], (B, N_HEADS_M, 1, HD_M)))

    y_s, ss = _trunk(x_sample, mods_all[:, Bp:Bp + Bs], mods_fin[Bp:Bp + Bs], wts_full, attn_s, state_s,
                     ts=Ss, keep=Ss, L=Ss, long_sequence=False)

    stack = lambda sts, i: jnp.stack([s[i] for s in sts])
    return (y_p, y_s,
            stack(sp, 0), stack(sp, 1), stack(sp, 2), stack(sp, 3), stack(sp, 4),
            stack(ss, 0), stack(ss, 1), stack(ss, 2), stack(ss, 3), stack(ss, 4))
```

```python
import functools

import jax
import jax.numpy as jnp
from jax import lax
from jax.experimental import pallas as pl
from jax.experimental.pallas import tpu as pltpu

F32 = jnp.float32
BF16 = jnp.bfloat16

D_MODEL = 1024
CHUNK = 64
BAND_CHUNKS = 8
BAND_PAST = BAND_CHUNKS * CHUNK
D_ATT = 512
N_HEADS_A = 8
HD_A = 64
MAX_REL = 128
D_MLSTM = 512
N_HEADS_M = 4
HD_M = 128
N_GROUPS = 4
EXP_PER_GROUP = 4
N_EXPERTS = 16
D_EXPERT = 256
EPS = 1e-6
NEG = -1e30

LANES = 128
SUBLANES = 8
N_SLABS = 7
SLAB = 512
ATTN_HEADS = 4
Q_BLOCK = 256
K_WINDOW = BAND_PAST + Q_BLOCK
ROUTER_E0 = N_GROUPS
PACKED = D_MODEL // 2
GATE_GROUP = PACKED // LANES
DMA_UNROLL = 8
MOE_TILE = 512
VMEM_LIMIT = 56 * 1024 * 1024


def _cparams(sem):
    return pltpu.CompilerParams(dimension_semantics=sem, vmem_limit_bytes=VMEM_LIMIT)


def _mm(a, b, dims=None):
    precision = lax.Precision.HIGHEST if a.dtype == F32 and b.dtype == F32 else None
    if dims is None:
        return jnp.dot(a, b, precision=precision, preferred_element_type=F32)
    return lax.dot_general(a, b, dims, precision=precision, preferred_element_type=F32)


def _ada_kernel(c_ref, w_ref, b_ref, o_ref):
    c = c_ref[...]
    o_ref[0] = _mm(c * jax.nn.sigmoid(c), w_ref[0]) + b_ref[0]


def _ada(c, w, b, tn):
    nl, d, n = w.shape
    bc = c.shape[0]
    return pl.pallas_call(
        _ada_kernel,
        out_shape=jax.ShapeDtypeStruct((nl, bc, n), F32),
        grid=(nl, n // tn),
        in_specs=[pl.BlockSpec((bc, d), lambda l, j: (0, 0)),
                  pl.BlockSpec((1, d, tn), lambda l, j: (l, 0, j)),
                  pl.BlockSpec((1, 1, tn), lambda l, j: (l, 0, j))],
        out_specs=pl.BlockSpec((1, bc, tn), lambda l, j: (l, 0, j)),
        compiler_params=_cparams(("parallel", "parallel")),
        name="ada",
    )(c, w, b.reshape(nl, 1, n))


def _norm_mod(x, g, scale, shift):
    r = lax.rsqrt(jnp.mean(x * x, axis=-1, keepdims=True) + EPS)
    return (x * r * g) * (1.0 + scale) + shift


def _inproj_kernel(x_ref, sh_ref, sc_ref, g_ref, w_ref, wg_ref, bg_ref, *rest, km_block):
    if km_block:
        wkt_ref, rest = rest[0], rest[1:]
    qa_ref, ka_ref, va_ref, qm_ref, km_ref, vm_ref, om_ref, gc_ref, kt_ref, vt_ref = rest
    hb = _norm_mod(x_ref[0], g_ref[...], sc_ref[...], sh_ref[...]).astype(w_ref.dtype)
    outs = (qa_ref, ka_ref, va_ref, qm_ref, km_ref, vm_ref, om_ref)
    for j, o_ref in enumerate(outs):
        cols = slice(j * SLAB, (j + 1) * SLAB)
        if j == 4 and km_block:
            t = lax.dot_general(wkt_ref[...], hb, (((1,), (1,)), ((), ())), preferred_element_type=F32)
            t = t * (HD_M ** -0.5)
            for c in range(hb.shape[0] // km_block):
                o_ref[0, c] = t[:, c * km_block:(c + 1) * km_block].astype(BF16)
            continue
        p = _mm(hb, w_ref[:, cols])
        if j == 4:
            p = p * (HD_M ** -0.5)
        o_ref[0] = p.astype(o_ref.dtype)
        if j in (1, 2):
            (kt_ref if j == 1 else vt_ref)[0] = p
    pg = _mm(hb, wg_ref[...]) + bg_ref[...]
    lsig = jnp.minimum(pg, 0.0) - jnp.log1p(jnp.exp(-jnp.abs(pg)))
    lane = lax.broadcasted_iota(jnp.int32, pg.shape, 1)
    gc_ref[0] = jnp.where(lane < N_HEADS_M, pg, jnp.where(lane < 2 * N_HEADS_M, lsig, 0.0))


def _in_proj(x, mods, g, w_main, w_gate, b_gate, ts, keep, w_kt=None, km_block=None):
    B, S, D = x.shape
    n_tiles = S // ts
    n_tail = keep // ts
    first_tail = n_tiles - n_tail
    slab = jax.ShapeDtypeStruct((B, S, SLAB), w_main.dtype)
    tail = jax.ShapeDtypeStruct((B, keep, SLAB), F32)
    slab_spec = pl.BlockSpec((1, ts, SLAB), lambda b, s: (b, s, 0))
    tail_spec = pl.BlockSpec((1, ts, SLAB), lambda b, s: (b, jnp.maximum(s - first_tail, 0), 0))
    slabs = [slab] * N_SLABS
    slab_specs = [slab_spec] * N_SLABS
    if mods.shape[2] == 1:
        mod_spec = lambda i: pl.BlockSpec((None, None, 1, D), lambda b, s: (b, i, 0, 0))
    else:
        mod_spec = lambda i: pl.BlockSpec((None, None, ts, D), lambda b, s: (i, b, s, 0))
    in_specs = [pl.BlockSpec((1, ts, D), lambda b, s: (b, s, 0)),
                mod_spec(0), mod_spec(1),
                pl.BlockSpec((1, D), lambda b, s: (0, 0)),
                pl.BlockSpec(w_main.shape, lambda b, s: (0, 0)),
                pl.BlockSpec((D, LANES), lambda b, s: (0, 0)),
                pl.BlockSpec((1, LANES), lambda b, s: (0, 0))]
    args = [x, mods, mods, g, w_main, w_gate, b_gate]
    if km_block:
        slabs[4] = jax.ShapeDtypeStruct((B, S // km_block, SLAB, km_block), BF16)
        slab_specs[4] = pl.BlockSpec((1, ts // km_block, SLAB, km_block), lambda b, s: (b, s, 0, 0))
        in_specs.append(pl.BlockSpec((SLAB, D), lambda b, s: (0, 0)))
        args.append(w_kt)
    return pl.pallas_call(
        functools.partial(_inproj_kernel, km_block=km_block),
        out_shape=tuple(slabs) + (jax.ShapeDtypeStruct((B, S, LANES), F32), tail, tail),
        grid=(B, n_tiles),
        in_specs=in_specs,
        out_specs=tuple(slab_specs) + (pl.BlockSpec((1, ts, LANES), lambda b, s: (b, s, 0)),
                                       tail_spec, tail_spec),
        compiler_params=_cparams(("parallel", "arbitrary")),
        name="in_proj",
    )(*args)


def _attn_prompt_kernel(q_ref, k_ref, v_ref, bias_ref, o_ref):
    i = pl.program_id(2)
    ws = pl.multiple_of(jnp.maximum(i * Q_BLOCK - BAND_PAST, 0), Q_BLOCK)
    variant = jnp.minimum(i, BAND_PAST // Q_BLOCK)
    q = q_ref[0]
    kw = k_ref[0, pl.ds(ws, K_WINDOW), :]
    vw = v_ref[0, pl.ds(ws, K_WINDOW), :]
    lane = lax.broadcasted_iota(jnp.int32, q.shape, 1)
    out = None
    for hh in range(ATTN_HEADS):
        in_head = (lane >= hh * HD_A) & (lane < (hh + 1) * HD_A)
        qh = jnp.where(in_head, q * (HD_A ** -0.5), 0.0).astype(BF16)
        s = lax.dot_general(qh, kw, (((1,), (1,)), ((), ())), preferred_element_type=F32)
        s = s + bias_ref[variant, hh]
        mx = jnp.max(s, axis=-1, keepdims=True)
        p = jnp.exp(s - mx)
        l = jnp.sum(p, axis=-1, keepdims=True)
        o = jnp.dot(p.astype(BF16), vw, preferred_element_type=F32) / l
        out = o if out is None else jnp.where(in_head, o, out)
    o_ref[0] = out.astype(BF16)


def _band_bias_table(table):
    H = table.shape[0]
    a = jnp.arange(Q_BLOCK)[:, None]
    j = jnp.arange(K_WINDOW)[None, :]
    period = Q_BLOCK + K_WINDOW
    u = jnp.arange(period)
    v = jnp.where(u < K_WINDOW, u, u - period)
    tabs = []
    for off in range(0, BAND_PAST + 1, Q_BLOCK):
        prof = table[:, jnp.clip(off - v, -MAX_REL, MAX_REL) + MAX_REL].astype(F32)
        skew = jnp.tile(prof, (1, Q_BLOCK))[:, :Q_BLOCK * (period - 1)].reshape(H, Q_BLOCK, period - 1)
        bias = skew[:, :, :K_WINDOW]
        qc = (a + off) // CHUNK
        kc = j // CHUNK
        valid = (kc <= qc) & (kc >= qc - BAND_CHUNKS)
        tabs.append(jnp.where(valid[None], bias, NEG))
    return jnp.stack(tabs)


def _attn_prompt(q, k, v, bias_tab):
    B, S, _ = q.shape
    nv = bias_tab.shape[0]
    width = ATTN_HEADS * HD_A
    return pl.pallas_call(
        _attn_prompt_kernel,
        out_shape=jax.ShapeDtypeStruct((B, S, D_ATT), BF16),
        grid=(N_HEADS_A // ATTN_HEADS, B, S // Q_BLOCK),
        in_specs=[pl.BlockSpec((1, Q_BLOCK, width), lambda hp, b, i: (b, i, hp)),
                  pl.BlockSpec((1, S, width), lambda hp, b, i: (b, 0, hp)),
                  pl.BlockSpec((1, S, width), lambda hp, b, i: (b, 0, hp)),
                  pl.BlockSpec((nv, ATTN_HEADS, Q_BLOCK, K_WINDOW), lambda hp, b, i: (0, hp, 0, 0))],
        out_specs=pl.BlockSpec((1, Q_BLOCK, width), lambda hp, b, i: (b, i, hp)),
        compiler_params=_cparams(("parallel", "parallel", "arbitrary")),
        name="attn_prompt",
    )(q, k, v, bias_tab)


def _attn_sample_kernel(q_ref, kn_ref, vn_ref, kc_ref, vc_ref, bc_ref, bn_ref, o_ref):
    for h in range(N_HEADS_A):
        q = q_ref[0, h] * (HD_A ** -0.5)
        kn = kn_ref[0, h]
        vn = vn_ref[0, h]
        kc = kc_ref[0, h]
        vc = vc_ref[0, h]
        nt = (((1,), (1,)), ((), ()))
        sc = _mm(q, kc, nt) + bc_ref[h]
        sn = _mm(q, kn, nt) + bn_ref[h]
        mx = jnp.maximum(jnp.max(sc, axis=-1, keepdims=True), jnp.max(sn, axis=-1, keepdims=True))
        pc = jnp.exp(sc - mx)
        pn = jnp.exp(sn - mx)
        l = jnp.sum(pc, axis=-1, keepdims=True) + jnp.sum(pn, axis=-1, keepdims=True)
        o = _mm(pc, vc) + _mm(pn, vn)
        o_ref[0, h] = o / l


def _attn_sample(q, kn, vn, kc, vc, bias_c, bias_n):
    B, H, T, d = q.shape
    W = kc.shape[2]
    new = pl.BlockSpec((1, H, T, d), lambda b: (b, 0, 0, 0))
    old = pl.BlockSpec((1, H, W, d), lambda b: (b, 0, 0, 0))
    return pl.pallas_call(
        _attn_sample_kernel,
        out_shape=jax.ShapeDtypeStruct((B, H, T, d), F32),
        grid=(B,),
        in_specs=[new, new, new, old, old,
                  pl.BlockSpec((H, T, W), lambda b: (0, 0, 0)),
                  pl.BlockSpec((H, T, T), lambda b: (0, 0, 0))],
        out_specs=new,
        compiler_params=_cparams(("parallel",)),
        name="attn_sample",
    )(q, kn, vn, kc, vc, bias_c, bias_n)


def _mlstm_kernel(q_ref, k_ref, v_ref, om_ref, gc_ref, gml_ref, c0_ref, n0_ref, m0_ref,
                  h_ref, c_ref, n_ref, m_ref, *, L, nc):
    c_ref[0] = c0_ref[0]
    n_ref[0] = n0_ref[0]
    m_ref[0] = m0_ref[0]
    tt = lax.broadcasted_iota(jnp.int32, (L, L), 0)
    ss = lax.broadcasted_iota(jnp.int32, (L, L), 1)
    causal = ss <= tt
    eye = ss == tt
    nt = (((1,), (1,)), ((), ()))
    tn = (((0,), (0,)), ((), ()))

    def chunk(ci, carry):
        off = pl.multiple_of(ci * L, L)
        g = gc_ref[0, pl.ds(off, L), :]
        for hh in range(N_HEADS_M):
            cols = slice(hh * HD_M, (hh + 1) * HD_M)
            ig_c = g[:, hh:hh + 1]
            lf_c = g[:, N_HEADS_M + hh:N_HEADS_M + hh + 1]
            ig_r = jnp.sum(jnp.where(eye, ig_c, 0.0), axis=0, keepdims=True)
            lf_r = jnp.sum(jnp.where(eye, lf_c, 0.0), axis=0, keepdims=True)
            b_r = jnp.sum(jnp.where(tt <= ss, lf_c, 0.0), axis=0, keepdims=True)
            b_c = jnp.sum(jnp.where(causal, lf_r, 0.0), axis=1, keepdims=True)
            a_r = ig_r - b_r
            a_c = ig_c - b_c
            m_prev = m_ref[0, hh][:, 0:1]
            Mt = jnp.maximum(jnp.max(jnp.where(causal, a_r, -jnp.inf), axis=1, keepdims=True), m_prev)
            P = jnp.where(causal, jnp.exp(a_r - Mt), 0.0)
            decay = jnp.exp(m_prev - Mt)
            q = q_ref[0, pl.ds(off, L), cols]
            k = k_ref[0, pl.ds(off, L), cols]
            v = v_ref[0, pl.ds(off, L), cols]
            C = c_ref[0, hh]
            n = n_ref[0, hh]
            S = P * _mm(q, k, nt)
            num = decay * _mm(q, C.astype(q.dtype)) + _mm(S.astype(q.dtype), v)
            den = decay * jnp.sum(q.astype(F32) * n, axis=1, keepdims=True) + jnp.sum(S, axis=1, keepdims=True)
            hv = num / jnp.maximum(jnp.abs(den), jnp.exp(-(b_c + Mt)))
            hn = hv * lax.rsqrt(jnp.mean(hv * hv, axis=-1, keepdims=True) + EPS) * gml_ref[:, cols]
            om = om_ref[0, pl.ds(off, L), cols].astype(F32)
            h_ref[0, pl.ds(off, L), cols] = (hn * jax.nn.sigmoid(om)).astype(h_ref.dtype)
            M_last = Mt[L - 1:L, :]
            b_last = b_c[L - 1:L, :]
            kw = k.astype(F32) * jnp.exp(a_c - M_last)
            cd = jnp.exp(m_prev - M_last)
            c_ref[0, hh] = cd * C + _mm(kw.astype(q.dtype), v, tn)
            n_ref[0, hh] = cd * n + jnp.sum(kw, axis=0, keepdims=True)
            m_ref[0, hh] = jnp.broadcast_to(b_last + M_last, (1, HD_M))
        return carry

    lax.fori_loop(0, nc, chunk, 0)


def _mlstm(q, k, v, om, gc, gml, c0, n0, m0, L):
    B, S, _ = q.shape
    H = N_HEADS_M
    seq = pl.BlockSpec((1, S, D_MLSTM), lambda b: (b, 0, 0))
    cst = pl.BlockSpec((1, H, HD_M, HD_M), lambda b: (b, 0, 0, 0))
    vec = pl.BlockSpec((1, H, 1, HD_M), lambda b: (b, 0, 0, 0))
    return pl.pallas_call(
        functools.partial(_mlstm_kernel, L=L, nc=S // L),
        out_shape=(jax.ShapeDtypeStruct((B, S, D_MLSTM), q.dtype),
                   jax.ShapeDtypeStruct((B, H, HD_M, HD_M), F32),
                   jax.ShapeDtypeStruct((B, H, 1, HD_M), F32),
                   jax.ShapeDtypeStruct((B, H, 1, HD_M), F32)),
        grid=(B,),
        in_specs=[seq, seq, seq, seq,
                  pl.BlockSpec((1, S, LANES), lambda b: (b, 0, 0)),
                  pl.BlockSpec((1, D_MLSTM), lambda b: (0, 0)),
                  cst, vec, vec],
        out_specs=(seq, cst, vec, vec),
        compiler_params=_cparams(("parallel",)),
        name="mlstm",
    )(q, k, v, om, gc, gml, c0, n0, m0)


def _split3(x):
    hi = x.astype(BF16)
    r1 = x - hi.astype(F32)
    mid = r1.astype(BF16)
    lo = (r1 - mid.astype(F32)).astype(BF16)
    return hi, mid, lo


def _mlstm_long_kernel(q_ref, kt_ref, v_ref, om_ref, gc_ref, gml_ref, ca0_ref, m0_ref,
                       h_ref, ca_ref, m_ref, *, L, nc):
    ca_ref[0] = ca0_ref[0]
    m_ref[0] = m0_ref[0]
    tt = lax.broadcasted_iota(jnp.int32, (L, L), 0)
    ss = lax.broadcasted_iota(jnp.int32, (L, L), 1)
    causal = ss <= tt
    lower = jnp.where(causal, 1.0, 0.0).astype(BF16)
    upper = jnp.where(tt <= ss, 1.0, 0.0).astype(BF16)
    ones_blk = jnp.ones((L, HD_M), BF16)
    mean_mat = jnp.full((HD_M, HD_M), 1.0 / HD_M, BF16)
    lane = lax.broadcasted_iota(jnp.int32, (L, LANES), 1)
    H = N_HEADS_M

    def chunk(ci, carry):
        off = pl.multiple_of(ci * L, L)
        g = gc_ref[0, pl.ds(off, L), :]
        g_rows = g.T[:2 * H, :]
        lf_c = jnp.where(lane < H, pltpu.roll(g, LANES - H, axis=1), 0.0)
        bc3 = jnp.dot(lower, jnp.concatenate(_split3(lf_c), axis=1), preferred_element_type=F32)
        b_c = bc3[:, :LANES] + bc3[:, LANES:2 * LANES] + bc3[:, 2 * LANES:]
        terms = [t.astype(F32) for t in _split3(g_rows)] + [jnp.zeros((SUBLANES, L), F32)]
        rows3 = jnp.concatenate(terms, axis=0).astype(BF16)
        br3 = jnp.dot(rows3, upper, preferred_element_type=F32)
        b_r = br3[:2 * H] + br3[2 * H:4 * H] + br3[4 * H:6 * H]
        a_r4 = g_rows[:H] - b_r[H:]
        for hh in range(H):
            cols = slice(hh * HD_M, (hh + 1) * HD_M)
            m_prev = m_ref[0, hh][:, 0:1]
            a_r = a_r4[hh:hh + 1, :]
            Mt = jnp.maximum(jnp.max(jnp.where(causal, a_r, -jnp.inf), axis=1, keepdims=True), m_prev)
            Mb = jnp.broadcast_to(Mt, (L, LANES))
            P = jnp.where(causal, jnp.exp(a_r - jnp.concatenate([Mb] * (L // LANES), axis=1)), 0.0)
            q = q_ref[0, pl.ds(off, L), cols]
            kt = kt_ref[0, ci, cols, :]
            v = v_ref[0, pl.ds(off, L), cols]
            V = jnp.concatenate([v, ones_blk], axis=1)
            CA = ca_ref[0, hh]
            S = (P * jnp.dot(q, kt, preferred_element_type=F32)).astype(BF16)
            inter = jnp.dot(q, CA.astype(BF16), preferred_element_type=F32)
            intra = jnp.dot(S, V, preferred_element_type=F32)
            decay = jnp.exp(m_prev - Mb)
            num = decay * inter[:, :HD_M] + intra[:, :HD_M]
            den = decay * inter[:, HD_M:] + intra[:, HD_M:]
            b_cb = jnp.broadcast_to(b_c[:, hh:hh + 1], (L, LANES))
            hv = num / jnp.maximum(jnp.abs(den), jnp.exp(-(b_cb + Mb)))
            sq = hv * hv
            sq_hi = sq.astype(BF16)
            sq_lo = (sq - sq_hi.astype(F32)).astype(BF16)
            msq = (jnp.dot(sq_hi, mean_mat, preferred_element_type=F32)
                   + jnp.dot(sq_lo, mean_mat, preferred_element_type=F32))
            hn = hv * lax.rsqrt(msq + EPS) * gml_ref[:, cols]
            om = om_ref[0, pl.ds(off, L), cols].astype(F32)
            h_ref[0, pl.ds(off, L), cols] = (hn * jax.nn.sigmoid(om)).astype(BF16)
            M_last = jnp.maximum(jnp.max(a_r, axis=1, keepdims=True), m_prev)
            kw = (kt.astype(F32) * jnp.exp(a_r - M_last)).astype(BF16)
            cd = jnp.exp(m_prev - M_last)
            ca_ref[0, hh] = cd * CA + jnp.dot(kw, V, preferred_element_type=F32)
            m_ref[0, hh] = jnp.broadcast_to(b_r[H + hh:H + hh + 1, L - 1:L] + M_last, (1, HD_M))
        return carry

    lax.fori_loop(0, nc, chunk, 0)


def _mlstm_long(q, kt, v, om, gc, gml, ca0, m0, L):
    B, S, _ = q.shape
    H = N_HEADS_M
    seq = pl.BlockSpec((1, S, D_MLSTM), lambda b: (b, 0, 0))
    cst = pl.BlockSpec((1, H, HD_M, 2 * HD_M), lambda b: (b, 0, 0, 0))
    vec = pl.BlockSpec((1, H, 1, HD_M), lambda b: (b, 0, 0, 0))
    return pl.pallas_call(
        functools.partial(_mlstm_long_kernel, L=L, nc=S // L),
        out_shape=(jax.ShapeDtypeStruct((B, S, D_MLSTM), BF16),
                   jax.ShapeDtypeStruct((B, H, HD_M, 2 * HD_M), F32),
                   jax.ShapeDtypeStruct((B, H, 1, HD_M), F32)),
        grid=(B,),
        in_specs=[seq,
                  pl.BlockSpec((1, S // L, D_MLSTM, L), lambda b: (b, 0, 0, 0)),
                  seq, seq,
                  pl.BlockSpec((1, S, LANES), lambda b: (b, 0, 0)),
                  pl.BlockSpec((1, D_MLSTM), lambda b: (0, 0)),
                  cst, vec],
        out_specs=(seq, cst, vec),
        compiler_params=_cparams(("parallel",)),
        name="mlstm_long",
    )(q, kt, v, om, gc, gml, ca0, m0)


def _mix_residual_norm(att_ref, hm_ref, x_ref, gt_ref, sh_ref, sc_ref, g_ref, wo_ref):
    mix = _mm(att_ref[0], wo_ref[:D_ATT, :]) + _mm(hm_ref[0], wo_ref[D_ATT:, :])
    x1 = x_ref[0] + gt_ref[...] * mix
    return x1, _norm_mod(x1, g_ref[...], sc_ref[...], sh_ref[...]).astype(wo_ref.dtype)


def _router(hb, wr_ref, br_ref):
    logit = _mm(hb, wr_ref[...]) + br_ref[...]
    lane = lax.broadcasted_iota(jnp.int32, logit.shape, 1)
    big = jnp.int32(LANES)
    is_g = lane < N_GROUPS
    mg = jnp.max(jnp.where(is_g, logit, -jnp.inf), axis=-1, keepdims=True)
    eg = jnp.where(is_g, jnp.exp(logit - mg), 0.0)
    pg = eg / jnp.sum(eg, axis=-1, keepdims=True)
    p_top = jnp.max(pg, axis=-1, keepdims=True)
    g_idx = jnp.min(jnp.where(is_g & (pg == p_top), lane, big), axis=-1, keepdims=True)
    e_lo = ROUTER_E0 + g_idx * EXP_PER_GROUP
    sel = (lane >= e_lo) & (lane < e_lo + EXP_PER_GROUP)
    me = jnp.max(jnp.where(sel, logit, -jnp.inf), axis=-1, keepdims=True)
    ee = jnp.where(sel, jnp.exp(logit - me), 0.0)
    pe = ee / jnp.sum(ee, axis=-1, keepdims=True)
    v1 = jnp.max(jnp.where(sel, pe, -1.0), axis=-1, keepdims=True)
    i1 = jnp.min(jnp.where(sel & (pe == v1), lane, big), axis=-1, keepdims=True)
    sel2 = sel & (lane != i1)
    v2 = jnp.max(jnp.where(sel2, pe, -1.0), axis=-1, keepdims=True)
    i2 = jnp.min(jnp.where(sel2 & (pe == v2), lane, big), axis=-1, keepdims=True)
    tot = v1 + v2
    w1 = v1 / tot * p_top
    w2 = v2 / tot * p_top
    return jnp.where(lane == i1, w1, jnp.where(lane == i2, w2, 0.0)), g_idx


def _outproj_kernel(att_ref, hm_ref, x_ref, gt_ref, sh_ref, sc_ref, g_ref, wo_ref, wr_ref, br_ref,
                    x1_ref, h2_ref, gates_ref):
    x1, hb = _mix_residual_norm(att_ref, hm_ref, x_ref, gt_ref, sh_ref, sc_ref, g_ref, wo_ref)
    x1_ref[0] = x1
    h2_ref[0] = hb
    gates_ref[0] = _router(hb, wr_ref, br_ref)[0]


def _pack_bf16_pairs(lo, hi):
    lo_bits = lax.bitcast_convert_type(lo.astype(BF16).astype(F32), jnp.uint32)
    hi_bits = lax.bitcast_convert_type(hi.astype(BF16).astype(F32), jnp.uint32)
    return (hi_bits & jnp.uint32(0xFFFF0000)) | (lo_bits >> 16)


def _unpack_bf16_pairs(packed):
    lo = lax.bitcast_convert_type(packed << 16, F32)
    hi = lax.bitcast_convert_type(packed & jnp.uint32(0xFFFF0000), F32)
    return lo, hi


def _outproj_sort_kernel(att_ref, hm_ref, x_ref, gt_ref, sh_ref, sc_ref, g_ref, wo_ref, wr_ref, br_ref,
                         x1_ref, row_ref, idx_ref, cnt_ref, fill_ref, *, region_rows):
    first = (pl.program_id(0) == 0) & (pl.program_id(1) == 0)

    @pl.when(first)
    def _():
        fill_ref[...] = jnp.zeros_like(fill_ref)

    x1, hb = _mix_residual_norm(att_ref, hm_ref, x_ref, gt_ref, sh_ref, sc_ref, g_ref, wo_ref)
    x1_ref[0] = x1
    gates, g_idx = _router(hb, wr_ref, br_ref)
    hf = hb.astype(F32)
    half = hf.shape[1] // 2
    ts = gates.shape[0]
    packed = _pack_bf16_pairs(hf[:, :half], hf[:, half:])
    for k in range(SUBLANES):
        if k < GATE_GROUP:
            piece = packed[:, k * LANES:(k + 1) * LANES]
        elif k == GATE_GROUP:
            piece = lax.bitcast_convert_type(gates, jnp.uint32)
        else:
            piece = jnp.zeros((ts, LANES), jnp.uint32)
        row_ref[_lane_group(k, ts), :] = piece
    lane = lax.broadcasted_iota(jnp.int32, gates.shape, 1)
    onehot = lane == g_idx
    tt = lax.broadcasted_iota(jnp.int32, (ts, ts), 0)
    ss = lax.broadcasted_iota(jnp.int32, (ts, ts), 1)
    before = jnp.where(ss < tt, 1.0, 0.0).astype(BF16)
    rank = jnp.dot(before, jnp.where(onehot, 1.0, 0.0).astype(BF16), preferred_element_type=F32)
    fill = fill_ref[...]
    slot = jnp.sum(jnp.where(onehot, rank + fill, 0.0), axis=-1, keepdims=True)
    slot = slot + g_idx.astype(F32) * float(region_rows)
    slot_row = jnp.sum(jnp.where(ss == tt, slot, 0.0), axis=0, keepdims=True)
    idx_ref[...] = slot_row.astype(jnp.int32)
    fill = fill + jnp.sum(jnp.where(onehot, 1.0, 0.0), axis=0, keepdims=True)
    fill_ref[...] = fill
    cnt_ref[...] = fill.astype(jnp.int32)


def _out_proj(att, hm, x, mods, g, w_out, w_router, b_router, ts, region_rows=None):
    B, S, D = x.shape
    nt = S // ts
    tile = lambda w: pl.BlockSpec((1, ts, w), lambda b, s: (b, s, 0))
    mod = lambda i: pl.BlockSpec((None, None, 1, D), lambda b, s: (b, i, 0, 0))
    const = lambda shape: pl.BlockSpec(shape, lambda b, s: (0, 0))
    in_specs = [tile(D_ATT), tile(D_MLSTM), tile(D), mod(2), mod(3), mod(4), const((1, D)),
                const((D, D)), const((D, LANES)), const((1, LANES))]
    args = (att, hm, x, mods, mods, mods, g, w_out, w_router, b_router)
    if region_rows is None:
        return pl.pallas_call(
            _outproj_kernel,
            out_shape=(jax.ShapeDtypeStruct((B, S, D), F32),
                       jax.ShapeDtypeStruct((B, S, D), w_out.dtype),
                       jax.ShapeDtypeStruct((B, S, LANES), F32)),
            grid=(B, nt), in_specs=in_specs,
            out_specs=(tile(D), tile(D), tile(LANES)),
            compiler_params=_cparams(("parallel", "parallel")),
            name="out_proj",
        )(*args)
    return pl.pallas_call(
        functools.partial(_outproj_sort_kernel, region_rows=region_rows),
        out_shape=(jax.ShapeDtypeStruct((B, S, D), F32),
                   jax.ShapeDtypeStruct((B, S * SUBLANES, LANES), jnp.uint32),
                   jax.ShapeDtypeStruct((B * nt, 1, ts), jnp.int32),
                   jax.ShapeDtypeStruct((1, LANES), jnp.int32)),
        grid=(B, nt), in_specs=in_specs,
        out_specs=(tile(D), pl.BlockSpec((None, ts * SUBLANES, LANES), lambda b, s: (b, s, 0)),
                   pl.BlockSpec((None, 1, ts), lambda b, s: (b * nt + s, 0, 0)),
                   const((1, LANES))),
        scratch_shapes=[pltpu.VMEM((1, LANES), F32)],
        compiler_params=_cparams(("arbitrary", "arbitrary")),
        name="out_proj_sort",
    )(*args)


def _moe_kernel(h_ref, gates_ref, x_ref, gt_ref, wg_ref, wu_ref, wd_ref, o_ref, acc_ref):
    e = pl.program_id(2)

    @pl.when(e == 0)
    def _():
        acc_ref[...] = jnp.zeros_like(acc_ref)

    h = h_ref[0]
    a = _mm(h, wg_ref[0])
    u = _mm(h, wu_ref[0])
    act = (a * jax.nn.sigmoid(a) * u).astype(h.dtype)
    y = _mm(act, wd_ref[0])
    gates = gates_ref[0]
    lane = lax.broadcasted_iota(jnp.int32, gates.shape, 1)
    ge = jnp.sum(jnp.where(lane == ROUTER_E0 + e, gates, 0.0), axis=-1, keepdims=True)
    acc_ref[...] += ge * y

    @pl.when(e == N_EXPERTS - 1)
    def _():
        o_ref[0] = x_ref[0] + gt_ref[0] * acc_ref[...]


def _moe(h2, gates, x1, mods, w_gate, w_up, w_down, ts):
    B, S, D = x1.shape
    tile = lambda w: pl.BlockSpec((1, ts, w), lambda b, s, e: (b, s, 0))
    return pl.pallas_call(
        _moe_kernel,
        out_shape=jax.ShapeDtypeStruct((B, S, D), F32),
        grid=(B, S // ts, N_EXPERTS),
        in_specs=[tile(D), tile(LANES), tile(D), tile(D),
                  pl.BlockSpec((1, D, D_EXPERT), lambda b, s, e: (e, 0, 0)),
                  pl.BlockSpec((1, D, D_EXPERT), lambda b, s, e: (e, 0, 0)),
                  pl.BlockSpec((1, D_EXPERT, D), lambda b, s, e: (e, 0, 0))],
        out_specs=tile(D),
        scratch_shapes=[pltpu.VMEM((ts, D), F32)],
        compiler_params=_cparams(("parallel", "parallel", "arbitrary")),
        name="moe",
    )(h2, gates, x1, mods, w_gate, w_up, w_down)


def _token_rows(token):
    return pl.ds(pl.multiple_of(token * SUBLANES, SUBLANES), SUBLANES)


def _lane_group(k, n, first_token=0):
    return pl.ds(first_token * SUBLANES + k, n, stride=SUBLANES)


def _for_each_token(ts, fn):
    def body(j, c):
        for u in range(DMA_UNROLL):
            fn(j * DMA_UNROLL + u, u % 2)
        return c
    lax.fori_loop(0, ts // DMA_UNROLL, body, 0)


def _dispatch_kernel(cnt_ref, row_ref, idx_ref, hs_ref, buf_ref, zero_ref, sem_ref, zsem_ref, *, region_rows, tm):
    i = pl.program_id(0)
    n = pl.num_programs(0)
    ts = idx_ref.shape[1]
    slot = i % 2

    def wait_slot(s):
        pltpu.make_async_copy(buf_ref.at[s], hs_ref.at[pl.ds(0, ts * SUBLANES), :], sem_ref.at[s]).wait()

    @pl.when(i >= 2)
    def _():
        wait_slot(slot)

    buf_ref[slot] = row_ref[...]

    def issue(r, priority):
        pltpu.make_async_copy(buf_ref.at[slot, _token_rows(r), :], hs_ref.at[_token_rows(idx_ref[0, r]), :],
                              sem_ref.at[slot]).start(priority=priority)
    _for_each_token(ts, issue)

    @pl.when(i == n - 1)
    def _():
        wait_slot(slot)

        @pl.when(n >= 2)
        def _():
            wait_slot(1 - slot)

        zero_ref[...] = jnp.zeros_like(zero_ref)

        def tail_copy(g):
            first = g * region_rows + cnt_ref[g]
            dst = hs_ref.at[pl.ds(pl.multiple_of(first * SUBLANES, SUBLANES), tm * SUBLANES), :]
            return pltpu.make_async_copy(zero_ref, dst, zsem_ref.at[g])

        for g in range(N_GROUPS):
            tail_copy(g).start()
        for g in range(N_GROUPS):
            tail_copy(g).wait()


def _dispatch(rows, idx, counts, ts, tm, region_rows):
    T = rows.shape[0] // SUBLANES
    grid_spec = pltpu.PrefetchScalarGridSpec(
        num_scalar_prefetch=1,
        grid=(T // ts,),
        in_specs=[pl.BlockSpec((ts * SUBLANES, LANES), lambda i, cnt: (i, 0)),
                  pl.BlockSpec((None, 1, ts), lambda i, cnt: (i, 0, 0), memory_space=pltpu.SMEM)],
        out_specs=pl.BlockSpec(memory_space=pl.ANY),
        scratch_shapes=[pltpu.VMEM((2, ts * SUBLANES, LANES), jnp.uint32),
                        pltpu.VMEM((tm * SUBLANES, LANES), jnp.uint32),
                        pltpu.SemaphoreType.DMA((2,)),
                        pltpu.SemaphoreType.DMA((N_GROUPS,))])
    return pl.pallas_call(
        functools.partial(_dispatch_kernel, region_rows=region_rows, tm=tm),
        out_shape=jax.ShapeDtypeStruct((N_GROUPS * region_rows * SUBLANES, LANES), jnp.uint32),
        grid_spec=grid_spec,
        compiler_params=_cparams(("arbitrary",)),
        name="moe_dispatch",
    )(counts, rows, idx)


def _moe_group_kernel(blk_ref, grp_ref, live_ref, hs_ref, wgu_ref, wd_ref, o_ref):
    i = pl.program_id(0)

    @pl.when(live_ref[i] == 0)
    def _():
        o_ref[...] = jnp.zeros_like(o_ref)

    @pl.when(live_ref[i] != 0)
    def _():
        tm = hs_ref.shape[0] // SUBLANES
        halves = [_unpack_bf16_pairs(hs_ref[_lane_group(k, tm), :]) for k in range(PACKED // LANES)]
        lo = jnp.concatenate([h[0] for h in halves], axis=1).astype(BF16)
        hi = jnp.concatenate([h[1] for h in halves], axis=1).astype(BF16)
        gates = lax.bitcast_convert_type(hs_ref[_lane_group(GATE_GROUP, tm), :], F32)
        lane = lax.broadcasted_iota(jnp.int32, gates.shape, 1)
        e0 = ROUTER_E0 + grp_ref[i] * EXP_PER_GROUP
        y = jnp.zeros((tm, D_MODEL), F32)
        for e in range(EXP_PER_GROUP):
            au = (jnp.dot(lo, wgu_ref[e, :PACKED, :], preferred_element_type=F32)
                  + jnp.dot(hi, wgu_ref[e, PACKED:, :], preferred_element_type=F32))
            a = au[:, :D_EXPERT]
            u = au[:, D_EXPERT:]
            act = (a * jax.nn.sigmoid(a) * u).astype(BF16)
            ge = jnp.sum(jnp.where(lane == e0 + e, gates, 0.0), axis=-1, keepdims=True)
            y = y + ge * jnp.dot(act, wd_ref[e], preferred_element_type=F32)
        for k in range(D_MODEL // LANES):
            o_ref[_lane_group(k, tm), :] = y[:, k * LANES:(k + 1) * LANES]


def _moe_group(hs, blk, grp, live, w_gu, w_down, tm):
    n_steps = blk.shape[0]
    rows = tm * SUBLANES
    n_blocks = hs.shape[0] // rows
    grid_spec = pltpu.PrefetchScalarGridSpec(
        num_scalar_prefetch=3,
        grid=(n_steps,),
        in_specs=[pl.BlockSpec((rows, LANES), lambda i, blk, grp, live: (blk[i], 0)),
                  pl.BlockSpec((None, EXP_PER_GROUP, D_MODEL, 2 * D_EXPERT), lambda i, blk, grp, live: (grp[i], 0, 0, 0)),
                  pl.BlockSpec((None, EXP_PER_GROUP, D_EXPERT, D_MODEL), lambda i, blk, grp, live: (grp[i], 0, 0, 0))],
        out_specs=pl.BlockSpec((rows, LANES),
                               lambda i, blk, grp, live: (jnp.where(live[i] != 0, blk[i], n_blocks), 0)))
    return pl.pallas_call(
        _moe_group_kernel,
        out_shape=jax.ShapeDtypeStruct(((n_blocks + 1) * rows, LANES), F32),
        grid_spec=grid_spec,
        compiler_params=_cparams(("arbitrary",)),
        name="moe_group",
    )(blk, grp, live, hs, w_gu, w_down)


def _combine_kernel(x1_ref, gt_ref, idx0_ref, idx1_ref, ys_ref, *rest, final):
    if final:
        shf_ref, scf_ref, gf_ref, o_ref, buf_ref, sem_ref = rest
    else:
        o_ref, buf_ref, sem_ref = rest
    i = pl.program_id(0)
    n = pl.num_programs(0)
    ts = idx0_ref.shape[1]
    slot = i % 2

    def issue(idx_ref, s):
        def one(r, priority):
            pltpu.make_async_copy(ys_ref.at[_token_rows(idx_ref[0, r]), :], buf_ref.at[s, _token_rows(r), :],
                                  sem_ref.at[s]).start(priority=priority)
        _for_each_token(ts, one)

    @pl.when(i == 0)
    def _():
        issue(idx0_ref, 0)

    @pl.when(i + 1 < n)
    def _():
        issue(idx1_ref, 1 - slot)

    pltpu.make_async_copy(ys_ref.at[pl.ds(0, ts * SUBLANES), :], buf_ref.at[slot], sem_ref.at[slot]).wait()

    g = gt_ref[...]
    for k in range(D_MODEL // LANES):
        cols = slice(k * LANES, (k + 1) * LANES)
        y = buf_ref[slot, _lane_group(k, ts), :]
        o_ref[:, cols] = x1_ref[:, cols] + g[:, cols] * y
    if final:
        o_ref[...] = _norm_mod(o_ref[...], gf_ref[...], scf_ref[...], shf_ref[...])


def _combine(x1, mods, idx, ys, ts, final=None):
    B, S, D = x1.shape
    nt = S // ts
    n = B * nt
    in_specs = [pl.BlockSpec((ts, D), lambda i: (i, 0)),
                pl.BlockSpec((None, None, 1, D), lambda i: (i // nt, 5, 0, 0)),
                pl.BlockSpec((None, 1, ts), lambda i: (i, 0, 0), memory_space=pltpu.SMEM),
                pl.BlockSpec((None, 1, ts), lambda i: (jnp.minimum(i + 1, n - 1), 0, 0), memory_space=pltpu.SMEM),
                pl.BlockSpec(memory_space=pl.ANY)]
    args = [x1.reshape(B * S, D), mods, idx, idx, ys]
    if final is not None:
        mods_f, g_final = final
        in_specs += [pl.BlockSpec((None, None, 1, D), lambda i: (i // nt, 0, 0, 0)),
                     pl.BlockSpec((None, None, 1, D), lambda i: (i // nt, 1, 0, 0)),
                     pl.BlockSpec((1, D), lambda i: (0, 0))]
        args += [mods_f, mods_f, g_final]
    out = pl.pallas_call(
        functools.partial(_combine_kernel, final=final is not None),
        out_shape=jax.ShapeDtypeStruct((B * S, D), F32),
        grid=(n,),
        in_specs=in_specs,
        out_specs=pl.BlockSpec((ts, D), lambda i: (i, 0)),
        scratch_shapes=[pltpu.VMEM((2, ts * SUBLANES, LANES), F32), pltpu.SemaphoreType.DMA((2,))],
        compiler_params=_cparams(("arbitrary",)),
        name="moe_combine",
    )(*args)
    return out.reshape(B, S, D)


def _tile_plan(counts, tm, n_steps, region_blocks):
    tiles = (counts + tm - 1) // tm
    ends = jnp.cumsum(tiles)
    total = ends[-1]
    step = jnp.minimum(jnp.arange(n_steps, dtype=jnp.int32), jnp.maximum(total - 1, 0))
    grp = jnp.sum((step[:, None] >= ends[None, :]).astype(jnp.int32), axis=1)
    grp = jnp.minimum(grp, N_GROUPS - 1)
    starts = ends - tiles
    blk = grp * region_blocks + (step - starts[grp])
    live = (jnp.arange(n_steps, dtype=jnp.int32) < total).astype(jnp.int32)
    return blk.astype(jnp.int32), grp.astype(jnp.int32), live


def _moe_sorted(rows, idx, counts, x1, mods, w_gu, w_down, ts, tm, final):
    B, S, D = x1.shape
    T = B * S
    region_rows = T + tm
    n_steps = T // tm + N_GROUPS
    cnt = counts[0, :N_GROUPS]
    hs = _dispatch(rows.reshape(T * SUBLANES, LANES), idx, cnt, ts, tm, region_rows)
    blk, grp, live = _tile_plan(cnt, tm, n_steps, region_rows // tm)
    ys = _moe_group(hs, blk, grp, live, w_gu, w_down, tm)
    return _combine(x1, mods, idx, ys, ts, final)


def _final_kernel(x_ref, sh_ref, sc_ref, g_ref, o_ref):
    o_ref[0] = _norm_mod(x_ref[0], g_ref[...], sc_ref[...], sh_ref[...])


def _final(x, mods_f, g, ts):
    B, S, D = x.shape
    return pl.pallas_call(
        _final_kernel,
        out_shape=jax.ShapeDtypeStruct((B, S, D), F32),
        grid=(B, S // ts),
        in_specs=[pl.BlockSpec((1, ts, D), lambda b, s: (b, s, 0)),
                  pl.BlockSpec((None, None, 1, D), lambda b, s: (b, 0, 0, 0)),
                  pl.BlockSpec((None, None, 1, D), lambda b, s: (b, 1, 0, 0)),
                  pl.BlockSpec((1, D), lambda b, s: (0, 0))],
        out_specs=pl.BlockSpec((1, ts, D), lambda b, s: (b, s, 0)),
        compiler_params=_cparams(("parallel", "parallel")),
        name="final_norm",
    )(x, mods_f, mods_f, g)


def _heads(t, n):
    B, S, _ = t.shape
    return t.reshape(B, S, n, -1).transpose(0, 2, 1, 3)


def _trunk(x, mods, mods_f, wts, attn_fn, state_fn, ts, keep, L, long_sequence):
    B, S, D = x.shape
    sort_tokens = long_sequence
    states = []
    depth = mods.shape[0]
    final = (mods_f.reshape(B, 2, 1, D), wts[0]["g_final"])
    for l in range(depth):
        w = wts[l]
        m = mods[l].reshape(B, 6, 1, D)
        if long_sequence:
            qa, ka, va, qm, km, vm, om, gc, kt, vt = _in_proj(
                x, m, w["g1"], w["w_main"], w["w_gate"], w["b_gate"], ts, keep, w["w_kt"], L)
        else:
            spread = jnp.stack([jnp.broadcast_to(m[:, 0], (B, S, D)), jnp.broadcast_to(m[:, 1], (B, S, D))])
            outs = _in_proj(x.reshape(1, B * S, D), spread.reshape(2, 1, B * S, D), w["g1"], w["w_main"],
                            w["w_gate"], w["b_gate"], B * S, B * S)
            qa, ka, va, qm, km, vm, om, gc, kt, vt = (o.reshape(B, S, o.shape[-1]) for o in outs)
        att = attn_fn(l, qa, ka, va, kt, vt)
        c0, n0, m0 = state_fn(l, B)
        if long_sequence:
            n_cols = jnp.broadcast_to(jnp.swapaxes(n0, 2, 3), (B, N_HEADS_M, HD_M, HD_M))
            hm, ca, mm = _mlstm_long(qm, km, vm, om, gc, w["g_mlstm"], jnp.concatenate([c0, n_cols], axis=-1), m0, L)
            C, n = ca[..., :HD_M], ca[:, :, None, :, HD_M]
        else:
            hm, C, n, mm = _mlstm(qm, km, vm, om, gc, w["g_mlstm"], c0, n0, m0, L)
        if sort_tokens:
            x1, rows, idx, counts = _out_proj(att, hm, x, m, w["g2"], w["w_out"], w["w_router"], w["b_router"], ts,
                                              region_rows=B * S + MOE_TILE)
            x = _moe_sorted(rows, idx, counts, x1, m, w["w_gu4"], w["w_down4"], ts, MOE_TILE,
                            final if l == depth - 1 else None)
        else:
            x1, h2, gates = _out_proj(att, hm, x, m, w["g2"], w["w_out"], w["w_router"], w["b_router"], ts)
            flat = lambda t: t.reshape(1, B * S, t.shape[-1])
            gate = jnp.broadcast_to(m[:, 5], (B, S, D))
            x = _moe(flat(h2), flat(gates), flat(x1), flat(gate), w["w_gate_e"], w["w_up_e"], w["w_down_e"],
                     B * S).reshape(B, S, D)
        states.append((_heads(kt, N_HEADS_A), _heads(vt, N_HEADS_A), C, n[:, :, 0, :], mm[:, :, 0, 0]))
    if not sort_tokens:
        x = _final(x, *final, ts)
    return x, states


def kernel(x_prompt, x_sample, c_prompt, c_sample, cache_attn_k, cache_attn_v, state_mlstm_C, state_mlstm_n,
           state_mlstm_m, w_ada, b_ada, g_norm, w_in, b_igate, b_fgate, rel_bias, g_mlstm, w_out, w_router_g,
           b_router_g, w_router_e, b_router_e, w_gate, w_up, w_down, w_ada_f, b_ada_f, g_final):
    depth = w_ada.shape[0]
    Bp, Sp, D = x_prompt.shape
    Bs, Ss, _ = x_sample.shape

    c_all = jnp.concatenate([c_prompt, c_sample], axis=0)
    pad = (-c_all.shape[0]) % 16
    c_all = jnp.pad(c_all, ((0, pad), (0, 0)))
    mods_all = _ada(c_all, w_ada, b_ada, 1536)
    mods_fin = _ada(c_all, w_ada_f[None], b_ada_f[None], 1024)[0]

    def layer_weights(l, full_precision):
        wl = w_in[l]
        n_main = N_SLABS * SLAB
        w_gate_cols = jnp.pad(wl[:, n_main:], ((0, 0), (0, LANES - 2 * N_HEADS_M)))
        b_gate_cols = jnp.pad(jnp.concatenate([b_igate[l], b_fgate[l]]), (0, LANES - 2 * N_HEADS_M))
        w_r = jnp.concatenate([w_router_g[l], w_router_e[l]], axis=1)
        b_r = jnp.concatenate([b_router_g[l], b_router_e[l]])
        npad = LANES - N_GROUPS - N_EXPERTS
        w_r = jnp.pad(w_r, ((0, 0), (0, npad)))
        shared = dict(g1=g_norm[l, 0][None], g2=g_norm[l, 1][None], g_final=g_final[None],
                      b_gate=b_gate_cols[None], g_mlstm=g_mlstm[l][None], b_router=jnp.pad(b_r, (0, npad))[None])
        if full_precision:
            return dict(shared, w_main=wl, w_gate=w_gate_cols, w_out=w_out[l], w_router=w_r,
                        w_gate_e=w_gate[l], w_up_e=w_up[l], w_down_e=w_down[l])
        w_gu = jnp.concatenate([w_gate[l], w_up[l]], axis=-1).astype(BF16)
        return dict(shared, w_main=wl[:, :n_main].astype(BF16), w_gate=w_gate_cols.astype(BF16),
                    w_kt=wl[:, 4 * SLAB:5 * SLAB].T.astype(BF16), w_out=w_out[l].astype(BF16),
                    w_router=w_r.astype(BF16),
                    w_gu4=w_gu.reshape(N_GROUPS, EXP_PER_GROUP, D_MODEL, 2 * D_EXPERT),
                    w_down4=w_down[l].astype(BF16).reshape(N_GROUPS, EXP_PER_GROUP, D_EXPERT, D_MODEL))

    wts = [layer_weights(l, False) for l in range(depth)]
    wts_full = [layer_weights(l, True) for l in range(depth)]

    bias_tabs = [_band_bias_table(rel_bias[l]) for l in range(depth)]

    def attn_p(l, qa, ka, va, kt, vt):
        return _attn_prompt(qa, ka, va, bias_tabs[l])

    def state_p(l, B):
        return (jnp.zeros((B, N_HEADS_M, HD_M, HD_M), F32), jnp.zeros((B, N_HEADS_M, 1, HD_M), F32),
                jnp.zeros((B, N_HEADS_M, 1, HD_M), F32))

    keep_p = min(BAND_PAST, Sp)
    y_p, sp = _trunk(x_prompt, mods_all[:, :Bp], mods_fin[:Bp], wts, attn_p, state_p,
                     ts=512, keep=keep_p, L=512, long_sequence=True)

    Wc = cache_attn_k.shape[3]
    tq = jnp.arange(Ss)[:, None]
    dist_c = Wc + tq - jnp.arange(Wc)[None, :]
    dist_n = tq - jnp.arange(Ss)[None, :]
    lookup = lambda l, dist: rel_bias[l][:, jnp.clip(dist, -MAX_REL, MAX_REL) + MAX_REL].astype(F32)

    def attn_s(l, qa, ka, va, kt, vt):
        o = _attn_sample(_heads(qa, N_HEADS_A), _heads(kt, N_HEADS_A), _heads(vt, N_HEADS_A),
                         cache_attn_k[l], cache_attn_v[l], lookup(l, dist_c), lookup(l, dist_n))
        return o.transpose(0, 2, 1, 3).reshape(Bs, Ss, D_ATT)

    def state_s(l, B):
        return (state_mlstm_C[l], state_mlstm_n[l][:, :, None, :],
                jnp.broadcast_to(state_mlstm_m[l][:, :, None, None], (B, N_HEADS_M, 1, HD_M)))

    y_s, ss = _trunk(x_sample, mods_all[:, Bp:Bp + Bs], mods_fin[Bp:Bp + Bs], wts_full, attn_s, state_s,
                     ts=Ss, keep=Ss, L=Ss, long_sequence=False)

    stack = lambda sts, i: jnp.stack([s[i] for s in sts])
    return (y_p, y_s,
            stack(sp, 0), stack(sp, 1), stack(sp, 2), stack(sp, 3), stack(sp, 4),
            stack(ss, 0), stack(ss, 1), stack(ss, 2), stack(ss, 3), stack(ss, 4))
```

```python
import functools

import jax
import jax.numpy as jnp
from jax import lax
from jax.experimental import pallas as pl
from jax.experimental.pallas import tpu as pltpu

F32 = jnp.float32
BF16 = jnp.bfloat16

D_MODEL = 1024
CHUNK = 64
BAND_CHUNKS = 8
BAND_PAST = BAND_CHUNKS * CHUNK
D_ATT = 512
N_HEADS_A = 8
HD_A = 64
MAX_REL = 128
D_MLSTM = 512
N_HEADS_M = 4
HD_M = 128
N_GROUPS = 4
EXP_PER_GROUP = 4
N_EXPERTS = 16
D_EXPERT = 256
EPS = 1e-6
NEG = -1e30

LANES = 128
SUBLANES = 8
N_SLABS = 7
SLAB = 512
ATTN_HEADS = 4
Q_BLOCK = 256
K_WINDOW = BAND_PAST + Q_BLOCK
ROUTER_E0 = N_GROUPS
PACKED = D_MODEL // 2
GATE_GROUP = PACKED // LANES
DMA_UNROLL = 8
MOE_TILE = 512
VMEM_LIMIT = 56 * 1024 * 1024


def _cparams(sem):
    return pltpu.CompilerParams(dimension_semantics=sem, vmem_limit_bytes=VMEM_LIMIT)


def _mm(a, b, dims=None):
    precision = lax.Precision.HIGHEST if a.dtype == F32 and b.dtype == F32 else None
    if dims is None:
        return jnp.dot(a, b, precision=precision, preferred_element_type=F32)
    return lax.dot_general(a, b, dims, precision=precision, preferred_element_type=F32)


def _ada_kernel(c_ref, w_ref, b_ref, o_ref):
    c = c_ref[...]
    o_ref[0] = _mm(c * jax.nn.sigmoid(c), w_ref[0]) + b_ref[0]


def _ada(c, w, b, tn):
    nl, d, n = w.shape
    bc = c.shape[0]
    return pl.pallas_call(
        _ada_kernel,
        out_shape=jax.ShapeDtypeStruct((nl, bc, n), F32),
        grid=(nl, n // tn),
        in_specs=[pl.BlockSpec((bc, d), lambda l, j: (0, 0)),
                  pl.BlockSpec((1, d, tn), lambda l, j: (l, 0, j)),
                  pl.BlockSpec((1, 1, tn), lambda l, j: (l, 0, j))],
        out_specs=pl.BlockSpec((1, bc, tn), lambda l, j: (l, 0, j)),
        compiler_params=_cparams(("parallel", "parallel")),
        name="ada",
    )(c, w, b.reshape(nl, 1, n))


def _norm_mod(x, g, scale, shift):
    r = lax.rsqrt(jnp.mean(x * x, axis=-1, keepdims=True) + EPS)
    return (x * r * g) * (1.0 + scale) + shift


def _inproj_kernel(x_ref, sh_ref, sc_ref, g_ref, w_ref, wg_ref, bg_ref, *rest, km_block):
    if km_block:
        wkt_ref, rest = rest[0], rest[1:]
    qa_ref, ka_ref, va_ref, qm_ref, km_ref, vm_ref, om_ref, gc_ref, kt_ref, vt_ref = rest
    hb = _norm_mod(x_ref[0], g_ref[...], sc_ref[...], sh_ref[...]).astype(w_ref.dtype)
    outs = (qa_ref, ka_ref, va_ref, qm_ref, km_ref, vm_ref, om_ref)
    for j, o_ref in enumerate(outs):
        cols = slice(j * SLAB, (j + 1) * SLAB)
        if j == 4 and km_block:
            t = lax.dot_general(wkt_ref[...], hb, (((1,), (1,)), ((), ())), preferred_element_type=F32)
            t = t * (HD_M ** -0.5)
            for c in range(hb.shape[0] // km_block):
                o_ref[0, c] = t[:, c * km_block:(c + 1) * km_block].astype(BF16)
            continue
        p = _mm(hb, w_ref[:, cols])
        if j == 4:
            p = p * (HD_M ** -0.5)
        o_ref[0] = p.astype(o_ref.dtype)
        if j in (1, 2):
            (kt_ref if j == 1 else vt_ref)[0] = p
    pg = _mm(hb, wg_ref[...]) + bg_ref[...]
    lsig = jnp.minimum(pg, 0.0) - jnp.log1p(jnp.exp(-jnp.abs(pg)))
    lane = lax.broadcasted_iota(jnp.int32, pg.shape, 1)
    gc_ref[0] = jnp.where(lane < N_HEADS_M, pg, jnp.where(lane < 2 * N_HEADS_M, lsig, 0.0))


def _in_proj(x, mods, g, w_main, w_gate, b_gate, ts, keep, w_kt=None, km_block=None):
    B, S, D = x.shape
    n_tiles = S // ts
    n_tail = keep // ts
    first_tail = n_tiles - n_tail
    slab = jax.ShapeDtypeStruct((B, S, SLAB), w_main.dtype)
    tail = jax.ShapeDtypeStruct((B, keep, SLAB), F32)
    slab_spec = pl.BlockSpec((1, ts, SLAB), lambda b, s: (b, s, 0))
    tail_spec = pl.BlockSpec((1, ts, SLAB), lambda b, s: (b, jnp.maximum(s - first_tail, 0), 0))
    slabs = [slab] * N_SLABS
    slab_specs = [slab_spec] * N_SLABS
    if mods.shape[2] == 1:
        mod_spec = lambda i: pl.BlockSpec((None, None, 1, D), lambda b, s: (b, i, 0, 0))
    else:
        mod_spec = lambda i: pl.BlockSpec((None, None, ts, D), lambda b, s: (i, b, s, 0))
    in_specs = [pl.BlockSpec((1, ts, D), lambda b, s: (b, s, 0)),
                mod_spec(0), mod_spec(1),
                pl.BlockSpec((1, D), lambda b, s: (0, 0)),
                pl.BlockSpec(w_main.shape, lambda b, s: (0, 0)),
                pl.BlockSpec((D, LANES), lambda b, s: (0, 0)),
                pl.BlockSpec((1, LANES), lambda b, s: (0, 0))]
    args = [x, mods, mods, g, w_main, w_gate, b_gate]
    if km_block:
        slabs[4] = jax.ShapeDtypeStruct((B, S // km_block, SLAB, km_block), BF16)
        slab_specs[4] = pl.BlockSpec((1, ts // km_block, SLAB, km_block), lambda b, s: (b, s, 0, 0))
        in_specs.append(pl.BlockSpec((SLAB, D), lambda b, s: (0, 0)))
        args.append(w_kt)
    return pl.pallas_call(
        functools.partial(_inproj_kernel, km_block=km_block),
        out_shape=tuple(slabs) + (jax.ShapeDtypeStruct((B, S, LANES), F32), tail, tail),
        grid=(B, n_tiles),
        in_specs=in_specs,
        out_specs=tuple(slab_specs) + (pl.BlockSpec((1, ts, LANES), lambda b, s: (b, s, 0)),
                                       tail_spec, tail_spec),
        compiler_params=_cparams(("parallel", "arbitrary")),
        name="in_proj",
    )(*args)


def _attn_prompt_kernel(q_ref, k_ref, v_ref, bias_ref, o_ref):
    i = pl.program_id(2)
    ws = pl.multiple_of(jnp.maximum(i * Q_BLOCK - BAND_PAST, 0), Q_BLOCK)
    variant = jnp.minimum(i, BAND_PAST // Q_BLOCK)
    q = q_ref[0]
    kw = k_ref[0, pl.ds(ws, K_WINDOW), :]
    vw = v_ref[0, pl.ds(ws, K_WINDOW), :]
    lane = lax.broadcasted_iota(jnp.int32, q.shape, 1)
    out = None
    for hh in range(ATTN_HEADS):
        in_head = (lane >= hh * HD_A) & (lane < (hh + 1) * HD_A)
        qh = jnp.where(in_head, q * (HD_A ** -0.5), 0.0).astype(BF16)
        s = lax.dot_general(qh, kw, (((1,), (1,)), ((), ())), preferred_element_type=F32)
        s = s + bias_ref[variant, hh]
        mx = jnp.max(s, axis=-1, keepdims=True)
        p = jnp.exp(s - mx)
        l = jnp.sum(p, axis=-1, keepdims=True)
        o = jnp.dot(p.astype(BF16), vw, preferred_element_type=F32) / l
        out = o if out is None else jnp.where(in_head, o, out)
    o_ref[0] = out.astype(BF16)


def _band_bias_table(table):
    H = table.shape[0]
    a = jnp.arange(Q_BLOCK)[:, None]
    j = jnp.arange(K_WINDOW)[None, :]
    period = Q_BLOCK + K_WINDOW
    u = jnp.arange(period)
    v = jnp.where(u < K_WINDOW, u, u - period)
    tabs = []
    for off in range(0, BAND_PAST + 1, Q_BLOCK):
        prof = table[:, jnp.clip(off - v, -MAX_REL, MAX_REL) + MAX_REL].astype(F32)
        skew = jnp.tile(prof, (1, Q_BLOCK))[:, :Q_BLOCK * (period - 1)].reshape(H, Q_BLOCK, period - 1)
        bias = skew[:, :, :K_WINDOW]
        qc = (a + off) // CHUNK
        kc = j // CHUNK
        valid = (kc <= qc) & (kc >= qc - BAND_CHUNKS)
        tabs.append(jnp.where(valid[None], bias, NEG))
    return jnp.stack(tabs)


def _attn_prompt(q, k, v, bias_tab):
    B, S, _ = q.shape
    nv = bias_tab.shape[0]
    width = ATTN_HEADS * HD_A
    return pl.pallas_call(
        _attn_prompt_kernel,
        out_shape=jax.ShapeDtypeStruct((B, S, D_ATT), BF16),
        grid=(N_HEADS_A // ATTN_HEADS, B, S // Q_BLOCK),
        in_specs=[pl.BlockSpec((1, Q_BLOCK, width), lambda hp, b, i: (b, i, hp)),
                  pl.BlockSpec((1, S, width), lambda hp, b, i: (b, 0, hp)),
                  pl.BlockSpec((1, S, width), lambda hp, b, i: (b, 0, hp)),
                  pl.BlockSpec((nv, ATTN_HEADS, Q_BLOCK, K_WINDOW), lambda hp, b, i: (0, hp, 0, 0))],
        out_specs=pl.BlockSpec((1, Q_BLOCK, width), lambda hp, b, i: (b, i, hp)),
        compiler_params=_cparams(("parallel", "parallel", "arbitrary")),
        name="attn_prompt",
    )(q, k, v, bias_tab)


def _attn_sample_kernel(q_ref, kn_ref, vn_ref, kc_ref, vc_ref, bc_ref, bn_ref, o_ref):
    for h in range(N_HEADS_A):
        q = q_ref[0, h] * (HD_A ** -0.5)
        kn = kn_ref[0, h]
        vn = vn_ref[0, h]
        kc = kc_ref[0, h]
        vc = vc_ref[0, h]
        nt = (((1,), (1,)), ((), ()))
        sc = _mm(q, kc, nt) + bc_ref[h]
        sn = _mm(q, kn, nt) + bn_ref[h]
        mx = jnp.maximum(jnp.max(sc, axis=-1, keepdims=True), jnp.max(sn, axis=-1, keepdims=True))
        pc = jnp.exp(sc - mx)
        pn = jnp.exp(sn - mx)
        l = jnp.sum(pc, axis=-1, keepdims=True) + jnp.sum(pn, axis=-1, keepdims=True)
        o = _mm(pc, vc) + _mm(pn, vn)
        o_ref[0, h] = o / l


def _attn_sample(q, kn, vn, kc, vc, bias_c, bias_n):
    B, H, T, d = q.shape
    W = kc.shape[2]
    new = pl.BlockSpec((1, H, T, d), lambda b: (b, 0, 0, 0))
    old = pl.BlockSpec((1, H, W, d), lambda b: (b, 0, 0, 0))
    return pl.pallas_call(
        _attn_sample_kernel,
        out_shape=jax.ShapeDtypeStruct((B, H, T, d), F32),
        grid=(B,),
        in_specs=[new, new, new, old, old,
                  pl.BlockSpec((H, T, W), lambda b: (0, 0, 0)),
                  pl.BlockSpec((H, T, T), lambda b: (0, 0, 0))],
        out_specs=new,
        compiler_params=_cparams(("parallel",)),
        name="attn_sample",
    )(q, kn, vn, kc, vc, bias_c, bias_n)


def _mlstm_kernel(q_ref, k_ref, v_ref, om_ref, gc_ref, gml_ref, c0_ref, n0_ref, m0_ref,
                  h_ref, c_ref, n_ref, m_ref, *, L, nc):
    c_ref[0] = c0_ref[0]
    n_ref[0] = n0_ref[0]
    m_ref[0] = m0_ref[0]
    tt = lax.broadcasted_iota(jnp.int32, (L, L), 0)
    ss = lax.broadcasted_iota(jnp.int32, (L, L), 1)
    causal = ss <= tt
    eye = ss == tt
    nt = (((1,), (1,)), ((), ()))
    tn = (((0,), (0,)), ((), ()))

    def chunk(ci, carry):
        off = pl.multiple_of(ci * L, L)
        g = gc_ref[0, pl.ds(off, L), :]
        for hh in range(N_HEADS_M):
            cols = slice(hh * HD_M, (hh + 1) * HD_M)
            ig_c = g[:, hh:hh + 1]
            lf_c = g[:, N_HEADS_M + hh:N_HEADS_M + hh + 1]
            ig_r = jnp.sum(jnp.where(eye, ig_c, 0.0), axis=0, keepdims=True)
            lf_r = jnp.sum(jnp.where(eye, lf_c, 0.0), axis=0, keepdims=True)
            b_r = jnp.sum(jnp.where(tt <= ss, lf_c, 0.0), axis=0, keepdims=True)
            b_c = jnp.sum(jnp.where(causal, lf_r, 0.0), axis=1, keepdims=True)
            a_r = ig_r - b_r
            a_c = ig_c - b_c
            m_prev = m_ref[0, hh][:, 0:1]
            Mt = jnp.maximum(jnp.max(jnp.where(causal, a_r, -jnp.inf), axis=1, keepdims=True), m_prev)
            P = jnp.where(causal, jnp.exp(a_r - Mt), 0.0)
            decay = jnp.exp(m_prev - Mt)
            q = q_ref[0, pl.ds(off, L), cols]
            k = k_ref[0, pl.ds(off, L), cols]
            v = v_ref[0, pl.ds(off, L), cols]
            C = c_ref[0, hh]
            n = n_ref[0, hh]
            S = P * _mm(q, k, nt)
            num = decay * _mm(q, C.astype(q.dtype)) + _mm(S.astype(q.dtype), v)
            den = decay * jnp.sum(q.astype(F32) * n, axis=1, keepdims=True) + jnp.sum(S, axis=1, keepdims=True)
            hv = num / jnp.maximum(jnp.abs(den), jnp.exp(-(b_c + Mt)))
            hn = hv * lax.rsqrt(jnp.mean(hv * hv, axis=-1, keepdims=True) + EPS) * gml_ref[:, cols]
            om = om_ref[0, pl.ds(off, L), cols].astype(F32)
            h_ref[0, pl.ds(off, L), cols] = (hn * jax.nn.sigmoid(om)).astype(h_ref.dtype)
            M_last = Mt[L - 1:L, :]
            b_last = b_c[L - 1:L, :]
            kw = k.astype(F32) * jnp.exp(a_c - M_last)
            cd = jnp.exp(m_prev - M_last)
            c_ref[0, hh] = cd * C + _mm(kw.astype(q.dtype), v, tn)
            n_ref[0, hh] = cd * n + jnp.sum(kw, axis=0, keepdims=True)
            m_ref[0, hh] = jnp.broadcast_to(b_last + M_last, (1, HD_M))
        return carry

    lax.fori_loop(0, nc, chunk, 0)


def _mlstm(q, k, v, om, gc, gml, c0, n0, m0, L):
    B, S, _ = q.shape
    H = N_HEADS_M
    seq = pl.BlockSpec((1, S, D_MLSTM), lambda b: (b, 0, 0))
    cst = pl.BlockSpec((1, H, HD_M, HD_M), lambda b: (b, 0, 0, 0))
    vec = pl.BlockSpec((1, H, 1, HD_M), lambda b: (b, 0, 0, 0))
    return pl.pallas_call(
        functools.partial(_mlstm_kernel, L=L, nc=S // L),
        out_shape=(jax.ShapeDtypeStruct((B, S, D_MLSTM), q.dtype),
                   jax.ShapeDtypeStruct((B, H, HD_M, HD_M), F32),
                   jax.ShapeDtypeStruct((B, H, 1, HD_M), F32),
                   jax.ShapeDtypeStruct((B, H, 1, HD_M), F32)),
        grid=(B,),
        in_specs=[seq, seq, seq, seq,
                  pl.BlockSpec((1, S, LANES), lambda b: (b, 0, 0)),
                  pl.BlockSpec((1, D_MLSTM), lambda b: (0, 0)),
                  cst, vec, vec],
        out_specs=(seq, cst, vec, vec),
        compiler_params=_cparams(("parallel",)),
        name="mlstm",
    )(q, k, v, om, gc, gml, c0, n0, m0)


def _split3(x):
    hi = x.astype(BF16)
    r1 = x - hi.astype(F32)
    mid = r1.astype(BF16)
    lo = (r1 - mid.astype(F32)).astype(BF16)
    return hi, mid, lo


def _mlstm_long_kernel(q_ref, kt_ref, v_ref, om_ref, gc_ref, gml_ref, ca0_ref, m0_ref,
                       h_ref, ca_ref, m_ref, *, L, nc):
    ca_ref[0] = ca0_ref[0]
    m_ref[0] = m0_ref[0]
    tt = lax.broadcasted_iota(jnp.int32, (L, L), 0)
    ss = lax.broadcasted_iota(jnp.int32, (L, L), 1)
    causal = ss <= tt
    lower = jnp.where(causal, 1.0, 0.0).astype(BF16)
    upper = jnp.where(tt <= ss, 1.0, 0.0).astype(BF16)
    ones_blk = jnp.ones((L, HD_M), BF16)
    mean_mat = jnp.full((HD_M, HD_M), 1.0 / HD_M, BF16)
    lane = lax.broadcasted_iota(jnp.int32, (L, LANES), 1)
    H = N_HEADS_M

    def chunk(ci, carry):
        off = pl.multiple_of(ci * L, L)
        g = gc_ref[0, pl.ds(off, L), :]
        g_rows = g.T[:2 * H, :]
        lf_c = jnp.where(lane < H, pltpu.roll(g, LANES - H, axis=1), 0.0)
        bc3 = jnp.dot(lower, jnp.concatenate(_split3(lf_c), axis=1), preferred_element_type=F32)
        b_c = bc3[:, :LANES] + bc3[:, LANES:2 * LANES] + bc3[:, 2 * LANES:]
        terms = [t.astype(F32) for t in _split3(g_rows)] + [jnp.zeros((SUBLANES, L), F32)]
        rows3 = jnp.concatenate(terms, axis=0).astype(BF16)
        br3 = jnp.dot(rows3, upper, preferred_element_type=F32)
        b_r = br3[:2 * H] + br3[2 * H:4 * H] + br3[4 * H:6 * H]
        a_r4 = g_rows[:H] - b_r[H:]
        for hh in range(H):
            cols = slice(hh * HD_M, (hh + 1) * HD_M)
            m_prev = m_ref[0, hh][:, 0:1]
            a_r = a_r4[hh:hh + 1, :]
            Mt = jnp.maximum(jnp.max(jnp.where(causal, a_r, -jnp.inf), axis=1, keepdims=True), m_prev)
            Mb = jnp.broadcast_to(Mt, (L, LANES))
            P = jnp.where(causal, jnp.exp(a_r - jnp.concatenate([Mb] * (L // LANES), axis=1)), 0.0)
            q = q_ref[0, pl.ds(off, L), cols]
            kt = kt_ref[0, ci, cols, :]
            v = v_ref[0, pl.ds(off, L), cols]
            V = jnp.concatenate([v, ones_blk], axis=1)
            CA = ca_ref[0, hh]
            S = (P * jnp.dot(q, kt, preferred_element_type=F32)).astype(BF16)
            inter = jnp.dot(q, CA.astype(BF16), preferred_element_type=F32)
            intra = jnp.dot(S, V, preferred_element_type=F32)
            decay = jnp.exp(m_prev - Mb)
            num = decay * inter[:, :HD_M] + intra[:, :HD_M]
            den = decay * inter[:, HD_M:] + intra[:, HD_M:]
            b_cb = jnp.broadcast_to(b_c[:, hh:hh + 1], (L, LANES))
            hv = num / jnp.maximum(jnp.abs(den), jnp.exp(-(b_cb + Mb)))
            sq = hv * hv
            sq_hi = sq.astype(BF16)
            sq_lo = (sq - sq_hi.astype(F32)).astype(BF16)
            msq = (jnp.dot(sq_hi, mean_mat, preferred_element_type=F32)
                   + jnp.dot(sq_lo, mean_mat, preferred_element_type=F32))
            hn = hv * lax.rsqrt(msq + EPS) * gml_ref[:, cols]
            om = om_ref[0, pl.ds(off, L), cols].astype(F32)
            h_ref[0, pl.ds(off, L), cols] = (hn * jax.nn.sigmoid(om)).astype(BF16)
            M_last = jnp.maximum(jnp.max(a_r, axis=1, keepdims=True), m_prev)
            kw = (kt.astype(F32) * jnp.exp(a_r - M_last)).astype(BF16)
            cd = jnp.exp(m_prev - M_last)
            ca_ref[0, hh] = cd * CA + jnp.dot(kw, V, preferred_element_type=F32)
            m_ref[0, hh] = jnp.broadcast_to(b_r[H + hh:H + hh + 1, L - 1:L] + M_last, (1, HD_M))
        return carry

    lax.fori_loop(0, nc, chunk, 0)


def _mlstm_long(q, kt, v, om, gc, gml, ca0, m0, L):
    B, S, _ = q.shape
    H = N_HEADS_M
    seq = pl.BlockSpec((1, S, D_MLSTM), lambda b: (b, 0, 0))
    cst = pl.BlockSpec((1, H, HD_M, 2 * HD_M), lambda b: (b, 0, 0, 0))
    vec = pl.BlockSpec((1, H, 1, HD_M), lambda b: (b, 0, 0, 0))
    return pl.pallas_call(
        functools.partial(_mlstm_long_kernel, L=L, nc=S // L),
        out_shape=(jax.ShapeDtypeStruct((B, S, D_MLSTM), BF16),
                   jax.ShapeDtypeStruct((B, H, HD_M, 2 * HD_M), F32),
                   jax.ShapeDtypeStruct((B, H, 1, HD_M), F32)),
        grid=(B,),
        in_specs=[seq,
                  pl.BlockSpec((1, S // L, D_MLSTM, L), lambda b: (b, 0, 0, 0)),
                  seq, seq,
                  pl.BlockSpec((1, S, LANES), lambda b: (b, 0, 0)),
                  pl.BlockSpec((1, D_MLSTM), lambda b: (0, 0)),
                  cst, vec],
        out_specs=(seq, cst, vec),
        compiler_params=_cparams(("parallel",)),
        name="mlstm_long",
    )(q, kt, v, om, gc, gml, ca0, m0)


def _mix_residual_norm(att_ref, hm_ref, x_ref, gt_ref, sh_ref, sc_ref, g_ref, wo_ref):
    mix = _mm(att_ref[0], wo_ref[:D_ATT, :]) + _mm(hm_ref[0], wo_ref[D_ATT:, :])
    x1 = x_ref[0] + gt_ref[...] * mix
    return x1, _norm_mod(x1, g_ref[...], sc_ref[...], sh_ref[...]).astype(wo_ref.dtype)


def _router(hb, wr_ref, br_ref):
    logit = _mm(hb, wr_ref[...]) + br_ref[...]
    lane = lax.broadcasted_iota(jnp.int32, logit.shape, 1).astype(F32)
    big = float(LANES)
    is_g = lane < N_GROUPS
    mg = jnp.max(jnp.where(is_g, logit, -jnp.inf), axis=-1, keepdims=True)
    eg = jnp.where(is_g, jnp.exp(logit - mg), 0.0)
    pg = eg / jnp.sum(eg, axis=-1, keepdims=True)
    p_top = jnp.max(pg, axis=-1, keepdims=True)
    g_idx = jnp.min(jnp.where(is_g & (pg == p_top), lane, big), axis=-1, keepdims=True)
    e_lo = ROUTER_E0 + g_idx * EXP_PER_GROUP
    sel = (lane >= e_lo) & (lane < e_lo + EXP_PER_GROUP)
    me = jnp.max(jnp.where(sel, logit, -jnp.inf), axis=-1, keepdims=True)
    ee = jnp.where(sel, jnp.exp(logit - me), 0.0)
    pe = ee / jnp.sum(ee, axis=-1, keepdims=True)
    v1 = jnp.max(jnp.where(sel, pe, -1.0), axis=-1, keepdims=True)
    i1 = jnp.min(jnp.where(sel & (pe == v1), lane, big), axis=-1, keepdims=True)
    sel2 = sel & (lane != i1)
    v2 = jnp.max(jnp.where(sel2, pe, -1.0), axis=-1, keepdims=True)
    i2 = jnp.min(jnp.where(sel2 & (pe == v2), lane, big), axis=-1, keepdims=True)
    tot = v1 + v2
    w1 = v1 / tot * p_top
    w2 = v2 / tot * p_top
    return jnp.where(lane == i1, w1, jnp.where(lane == i2, w2, 0.0)), g_idx


def _outproj_kernel(att_ref, hm_ref, x_ref, gt_ref, sh_ref, sc_ref, g_ref, wo_ref, wr_ref, br_ref,
                    x1_ref, h2_ref, gates_ref):
    x1, hb = _mix_residual_norm(att_ref, hm_ref, x_ref, gt_ref, sh_ref, sc_ref, g_ref, wo_ref)
    x1_ref[0] = x1
    h2_ref[0] = hb
    gates_ref[0] = _router(hb, wr_ref, br_ref)[0]


def _pack_bf16_pairs(lo, hi):
    lo_bits = lax.bitcast_convert_type(lo.astype(BF16).astype(F32), jnp.uint32)
    hi_bits = lax.bitcast_convert_type(hi.astype(BF16).astype(F32), jnp.uint32)
    return (hi_bits & jnp.uint32(0xFFFF0000)) | (lo_bits >> 16)


def _unpack_bf16_pairs(packed):
    lo = lax.bitcast_convert_type(packed << 16, F32)
    hi = lax.bitcast_convert_type(packed & jnp.uint32(0xFFFF0000), F32)
    return lo, hi


def _outproj_sort_kernel(att_ref, hm_ref, x_ref, gt_ref, sh_ref, sc_ref, g_ref, wo_ref, wr_ref, br_ref,
                         x1_ref, row_ref, idx_ref, cnt_ref, fill_ref, *, region_rows):
    first = (pl.program_id(0) == 0) & (pl.program_id(1) == 0)

    @pl.when(first)
    def _():
        fill_ref[...] = jnp.zeros_like(fill_ref)

    x1, hb = _mix_residual_norm(att_ref, hm_ref, x_ref, gt_ref, sh_ref, sc_ref, g_ref, wo_ref)
    x1_ref[0] = x1
    gates, g_idx = _router(hb, wr_ref, br_ref)
    hf = hb.astype(F32)
    half = hf.shape[1] // 2
    ts = gates.shape[0]
    packed = _pack_bf16_pairs(hf[:, :half], hf[:, half:])
    for k in range(SUBLANES):
        if k < GATE_GROUP:
            piece = packed[:, k * LANES:(k + 1) * LANES]
        elif k == GATE_GROUP:
            piece = lax.bitcast_convert_type(gates, jnp.uint32)
        else:
            piece = jnp.zeros((ts, LANES), jnp.uint32)
        row_ref[_lane_group(k, ts), :] = piece
    lane = lax.broadcasted_iota(jnp.int32, gates.shape, 1)
    onehot = lane.astype(F32) == g_idx
    tt = lax.broadcasted_iota(jnp.int32, (ts, ts), 0)
    ss = lax.broadcasted_iota(jnp.int32, (ts, ts), 1)
    before = jnp.where(ss < tt, 1.0, 0.0).astype(BF16)
    rank = jnp.dot(before, jnp.where(onehot, 1.0, 0.0).astype(BF16), preferred_element_type=F32)
    fill = fill_ref[...]
    slot = jnp.sum(jnp.where(onehot, rank + fill, 0.0), axis=-1, keepdims=True)
    slot = slot + g_idx * float(region_rows)
    slot_row = jnp.sum(jnp.where(ss == tt, slot, 0.0), axis=0, keepdims=True)
    idx_ref[...] = slot_row.astype(jnp.int32)
    fill = fill + jnp.sum(jnp.where(onehot, 1.0, 0.0), axis=0, keepdims=True)
    fill_ref[...] = fill
    cnt_ref[...] = fill.astype(jnp.int32)


def _out_proj(att, hm, x, mods, g, w_out, w_router, b_router, ts, region_rows=None):
    B, S, D = x.shape
    nt = S // ts
    tile = lambda w: pl.BlockSpec((1, ts, w), lambda b, s: (b, s, 0))
    mod = lambda i: pl.BlockSpec((None, None, 1, D), lambda b, s: (b, i, 0, 0))
    const = lambda shape: pl.BlockSpec(shape, lambda b, s: (0, 0))
    in_specs = [tile(D_ATT), tile(D_MLSTM), tile(D), mod(2), mod(3), mod(4), const((1, D)),
                const((D, D)), const((D, LANES)), const((1, LANES))]
    args = (att, hm, x, mods, mods, mods, g, w_out, w_router, b_router)
    if region_rows is None:
        return pl.pallas_call(
            _outproj_kernel,
            out_shape=(jax.ShapeDtypeStruct((B, S, D), F32),
                       jax.ShapeDtypeStruct((B, S, D), w_out.dtype),
                       jax.ShapeDtypeStruct((B, S, LANES), F32)),
            grid=(B, nt), in_specs=in_specs,
            out_specs=(tile(D), tile(D), tile(LANES)),
            compiler_params=_cparams(("parallel", "parallel")),
            name="out_proj",
        )(*args)
    return pl.pallas_call(
        functools.partial(_outproj_sort_kernel, region_rows=region_rows),
        out_shape=(jax.ShapeDtypeStruct((B, S, D), F32),
                   jax.ShapeDtypeStruct((B, S * SUBLANES, LANES), jnp.uint32),
                   jax.ShapeDtypeStruct((B * nt, 1, ts), jnp.int32),
                   jax.ShapeDtypeStruct((1, LANES), jnp.int32)),
        grid=(B, nt), in_specs=in_specs,
        out_specs=(tile(D), pl.BlockSpec((None, ts * SUBLANES, LANES), lambda b, s: (b, s, 0)),
                   pl.BlockSpec((None, 1, ts), lambda b, s: (b * nt + s, 0, 0)),
                   const((1, LANES))),
        scratch_shapes=[pltpu.VMEM((1, LANES), F32)],
        compiler_params=_cparams(("arbitrary", "arbitrary")),
        name="out_proj_sort",
    )(*args)


def _moe_kernel(h_ref, gates_ref, x_ref, gt_ref, wg_ref, wu_ref, wd_ref, o_ref, acc_ref):
    e = pl.program_id(2)

    @pl.when(e == 0)
    def _():
        acc_ref[...] = jnp.zeros_like(acc_ref)

    h = h_ref[0]
    a = _mm(h, wg_ref[0])
    u = _mm(h, wu_ref[0])
    act = (a * jax.nn.sigmoid(a) * u).astype(h.dtype)
    y = _mm(act, wd_ref[0])
    gates = gates_ref[0]
    lane = lax.broadcasted_iota(jnp.int32, gates.shape, 1)
    ge = jnp.sum(jnp.where(lane == ROUTER_E0 + e, gates, 0.0), axis=-1, keepdims=True)
    acc_ref[...] += ge * y

    @pl.when(e == N_EXPERTS - 1)
    def _():
        o_ref[0] = x_ref[0] + gt_ref[0] * acc_ref[...]


def _moe(h2, gates, x1, mods, w_gate, w_up, w_down, ts):
    B, S, D = x1.shape
    tile = lambda w: pl.BlockSpec((1, ts, w), lambda b, s, e: (b, s, 0))
    return pl.pallas_call(
        _moe_kernel,
        out_shape=jax.ShapeDtypeStruct((B, S, D), F32),
        grid=(B, S // ts, N_EXPERTS),
        in_specs=[tile(D), tile(LANES), tile(D), tile(D),
                  pl.BlockSpec((1, D, D_EXPERT), lambda b, s, e: (e, 0, 0)),
                  pl.BlockSpec((1, D, D_EXPERT), lambda b, s, e: (e, 0, 0)),
                  pl.BlockSpec((1, D_EXPERT, D), lambda b, s, e: (e, 0, 0))],
        out_specs=tile(D),
        scratch_shapes=[pltpu.VMEM((ts, D), F32)],
        compiler_params=_cparams(("parallel", "parallel", "arbitrary")),
        name="moe",
    )(h2, gates, x1, mods, w_gate, w_up, w_down)


def _token_rows(token):
    return pl.ds(pl.multiple_of(token * SUBLANES, SUBLANES), SUBLANES)


def _lane_group(k, n, first_token=0):
    return pl.ds(first_token * SUBLANES + k, n, stride=SUBLANES)


def _for_each_token(ts, fn):
    def body(j, c):
        for u in range(DMA_UNROLL):
            fn(j * DMA_UNROLL + u, u % 2)
        return c
    lax.fori_loop(0, ts // DMA_UNROLL, body, 0)


def _dispatch_kernel(cnt_ref, row_ref, idx_ref, hs_ref, buf_ref, zero_ref, sem_ref, zsem_ref, *, region_rows, tm):
    i = pl.program_id(0)
    n = pl.num_programs(0)
    ts = idx_ref.shape[1]
    slot = i % 2

    def wait_slot(s):
        pltpu.make_async_copy(buf_ref.at[s], hs_ref.at[pl.ds(0, ts * SUBLANES), :], sem_ref.at[s]).wait()

    @pl.when(i >= 2)
    def _():
        wait_slot(slot)

    buf_ref[slot] = row_ref[...]

    def issue(r, priority):
        pltpu.make_async_copy(buf_ref.at[slot, _token_rows(r), :], hs_ref.at[_token_rows(idx_ref[0, r]), :],
                              sem_ref.at[slot]).start(priority=priority)
    _for_each_token(ts, issue)

    @pl.when(i == n - 1)
    def _():
        wait_slot(slot)

        @pl.when(n >= 2)
        def _():
            wait_slot(1 - slot)

        zero_ref[...] = jnp.zeros_like(zero_ref)

        def tail_copy(g):
            first = g * region_rows + cnt_ref[g]
            dst = hs_ref.at[pl.ds(pl.multiple_of(first * SUBLANES, SUBLANES), tm * SUBLANES), :]
            return pltpu.make_async_copy(zero_ref, dst, zsem_ref.at[g])

        for g in range(N_GROUPS):
            tail_copy(g).start()
        for g in range(N_GROUPS):
            tail_copy(g).wait()


def _dispatch(rows, idx, counts, ts, tm, region_rows):
    T = rows.shape[0] // SUBLANES
    grid_spec = pltpu.PrefetchScalarGridSpec(
        num_scalar_prefetch=1,
        grid=(T // ts,),
        in_specs=[pl.BlockSpec((ts * SUBLANES, LANES), lambda i, cnt: (i, 0)),
                  pl.BlockSpec((None, 1, ts), lambda i, cnt: (i, 0, 0), memory_space=pltpu.SMEM)],
        out_specs=pl.BlockSpec(memory_space=pl.ANY),
        scratch_shapes=[pltpu.VMEM((2, ts * SUBLANES, LANES), jnp.uint32),
                        pltpu.VMEM((tm * SUBLANES, LANES), jnp.uint32),
                        pltpu.SemaphoreType.DMA((2,)),
                        pltpu.SemaphoreType.DMA((N_GROUPS,))])
    return pl.pallas_call(
        functools.partial(_dispatch_kernel, region_rows=region_rows, tm=tm),
        out_shape=jax.ShapeDtypeStruct((N_GROUPS * region_rows * SUBLANES, LANES), jnp.uint32),
        grid_spec=grid_spec,
        compiler_params=_cparams(("arbitrary",)),
        name="moe_dispatch",
    )(counts, rows, idx)


def _moe_group_kernel(blk_ref, grp_ref, live_ref, hs_ref, wgu_ref, wd_ref, o_ref):
    i = pl.program_id(0)

    @pl.when(live_ref[i] == 0)
    def _():
        o_ref[...] = jnp.zeros_like(o_ref)

    @pl.when(live_ref[i] != 0)
    def _():
        tm = hs_ref.shape[0] // SUBLANES
        halves = [_unpack_bf16_pairs(hs_ref[_lane_group(k, tm), :]) for k in range(PACKED // LANES)]
        lo = jnp.concatenate([h[0] for h in halves], axis=1).astype(BF16)
        hi = jnp.concatenate([h[1] for h in halves], axis=1).astype(BF16)
        gates = lax.bitcast_convert_type(hs_ref[_lane_group(GATE_GROUP, tm), :], F32)
        lane = lax.broadcasted_iota(jnp.int32, gates.shape, 1)
        e0 = ROUTER_E0 + grp_ref[i] * EXP_PER_GROUP
        y = jnp.zeros((tm, D_MODEL), F32)
        for e in range(EXP_PER_GROUP):
            au = (jnp.dot(lo, wgu_ref[e, :PACKED, :], preferred_element_type=F32)
                  + jnp.dot(hi, wgu_ref[e, PACKED:, :], preferred_element_type=F32))
            a = au[:, :D_EXPERT]
            u = au[:, D_EXPERT:]
            act = (a * jax.nn.sigmoid(a) * u).astype(BF16)
            ge = jnp.sum(jnp.where(lane == e0 + e, gates, 0.0), axis=-1, keepdims=True)
            y = y + ge * jnp.dot(act, wd_ref[e], preferred_element_type=F32)
        for k in range(D_MODEL // LANES):
            o_ref[_lane_group(k, tm), :] = y[:, k * LANES:(k + 1) * LANES]


def _moe_group(hs, blk, grp, live, w_gu, w_down, tm):
    n_steps = blk.shape[0]
    rows = tm * SUBLANES
    n_blocks = hs.shape[0] // rows
    grid_spec = pltpu.PrefetchScalarGridSpec(
        num_scalar_prefetch=3,
        grid=(n_steps,),
        in_specs=[pl.BlockSpec((rows, LANES), lambda i, blk, grp, live: (blk[i], 0)),
                  pl.BlockSpec((None, EXP_PER_GROUP, D_MODEL, 2 * D_EXPERT), lambda i, blk, grp, live: (grp[i], 0, 0, 0)),
                  pl.BlockSpec((None, EXP_PER_GROUP, D_EXPERT, D_MODEL), lambda i, blk, grp, live: (grp[i], 0, 0, 0))],
        out_specs=pl.BlockSpec((rows, LANES),
                               lambda i, blk, grp, live: (jnp.where(live[i] != 0, blk[i], n_blocks), 0)))
    return pl.pallas_call(
        _moe_group_kernel,
        out_shape=jax.ShapeDtypeStruct(((n_blocks + 1) * rows, LANES), F32),
        grid_spec=grid_spec,
        compiler_params=_cparams(("arbitrary",)),
        name="moe_group",
    )(blk, grp, live, hs, w_gu, w_down)


def _combine_kernel(x1_ref, gt_ref, idx0_ref, idx1_ref, ys_ref, *rest, final):
    if final:
        shf_ref, scf_ref, gf_ref, o_ref, buf_ref, sem_ref = rest
    else:
        o_ref, buf_ref, sem_ref = rest
    i = pl.program_id(0)
    n = pl.num_programs(0)
    ts = idx0_ref.shape[1]
    slot = i % 2

    def issue(idx_ref, s):
        def one(r, priority):
            pltpu.make_async_copy(ys_ref.at[_token_rows(idx_ref[0, r]), :], buf_ref.at[s, _token_rows(r), :],
                                  sem_ref.at[s]).start(priority=priority)
        _for_each_token(ts, one)

    @pl.when(i == 0)
    def _():
        issue(idx0_ref, 0)

    @pl.when(i + 1 < n)
    def _():
        issue(idx1_ref, 1 - slot)

    pltpu.make_async_copy(ys_ref.at[pl.ds(0, ts * SUBLANES), :], buf_ref.at[slot], sem_ref.at[slot]).wait()

    g = gt_ref[...]
    for k in range(D_MODEL // LANES):
        cols = slice(k * LANES, (k + 1) * LANES)
        y = buf_ref[slot, _lane_group(k, ts), :]
        o_ref[:, cols] = x1_ref[:, cols] + g[:, cols] * y
    if final:
        o_ref[...] = _norm_mod(o_ref[...], gf_ref[...], scf_ref[...], shf_ref[...])


def _combine(x1, mods, idx, ys, ts, final=None):
    B, S, D = x1.shape
    nt = S // ts
    n = B * nt
    in_specs = [pl.BlockSpec((ts, D), lambda i: (i, 0)),
                pl.BlockSpec((None, None, 1, D), lambda i: (i // nt, 5, 0, 0)),
                pl.BlockSpec((None, 1, ts), lambda i: (i, 0, 0), memory_space=pltpu.SMEM),
                pl.BlockSpec((None, 1, ts), lambda i: (jnp.minimum(i + 1, n - 1), 0, 0), memory_space=pltpu.SMEM),
                pl.BlockSpec(memory_space=pl.ANY)]
    args = [x1.reshape(B * S, D), mods, idx, idx, ys]
    if final is not None:
        mods_f, g_final = final
        in_specs += [pl.BlockSpec((None, None, 1, D), lambda i: (i // nt, 0, 0, 0)),
                     pl.BlockSpec((None, None, 1, D), lambda i: (i // nt, 1, 0, 0)),
                     pl.BlockSpec((1, D), lambda i: (0, 0))]
        args += [mods_f, mods_f, g_final]
    out = pl.pallas_call(
        functools.partial(_combine_kernel, final=final is not None),
        out_shape=jax.ShapeDtypeStruct((B * S, D), F32),
        grid=(n,),
        in_specs=in_specs,
        out_specs=pl.BlockSpec((ts, D), lambda i: (i, 0)),
        scratch_shapes=[pltpu.VMEM((2, ts * SUBLANES, LANES), F32), pltpu.SemaphoreType.DMA((2,))],
        compiler_params=_cparams(("arbitrary",)),
        name="moe_combine",
    )(*args)
    return out.reshape(B, S, D)


def _tile_plan(counts, tm, n_steps, region_blocks):
    tiles = (counts + tm - 1) // tm
    ends = jnp.cumsum(tiles)
    total = ends[-1]
    step = jnp.minimum(jnp.arange(n_steps, dtype=jnp.int32), jnp.maximum(total - 1, 0))
    grp = jnp.sum((step[:, None] >= ends[None, :]).astype(jnp.int32), axis=1)
    grp = jnp.minimum(grp, N_GROUPS - 1)
    starts = ends - tiles
    blk = grp * region_blocks + (step - starts[grp])
    live = (jnp.arange(n_steps, dtype=jnp.int32) < total).astype(jnp.int32)
    return blk.astype(jnp.int32), grp.astype(jnp.int32), live


def _moe_sorted(rows, idx, counts, x1, mods, w_gu, w_down, ts, tm, final):
    B, S, D = x1.shape
    T = B * S
    region_rows = T + tm
    n_steps = T // tm + N_GROUPS
    cnt = counts[0, :N_GROUPS]
    hs = _dispatch(rows.reshape(T * SUBLANES, LANES), idx, cnt, ts, tm, region_rows)
    blk, grp, live = _tile_plan(cnt, tm, n_steps, region_rows // tm)
    ys = _moe_group(hs, blk, grp, live, w_gu, w_down, tm)
    return _combine(x1, mods, idx, ys, ts, final)


def _final_kernel(x_ref, sh_ref, sc_ref, g_ref, o_ref):
    o_ref[0] = _norm_mod(x_ref[0], g_ref[...], sc_ref[...], sh_ref[...])


def _final(x, mods_f, g, ts):
    B, S, D = x.shape
    return pl.pallas_call(
        _final_kernel,
        out_shape=jax.ShapeDtypeStruct((B, S, D), F32),
        grid=(B, S // ts),
        in_specs=[pl.BlockSpec((1, ts, D), lambda b, s: (b, s, 0)),
                  pl.BlockSpec((None, None, 1, D), lambda b, s: (b, 0, 0, 0)),
                  pl.BlockSpec((None, None, 1, D), lambda b, s: (b, 1, 0, 0)),
                  pl.BlockSpec((1, D), lambda b, s: (0, 0))],
        out_specs=pl.BlockSpec((1, ts, D), lambda b, s: (b, s, 0)),
        compiler_params=_cparams(("parallel", "parallel")),
        name="final_norm",
    )(x, mods_f, mods_f, g)


def _heads(t, n):
    B, S, _ = t.shape
    return t.reshape(B, S, n, -1).transpose(0, 2, 1, 3)


def _trunk(x, mods, mods_f, wts, attn_fn, state_fn, ts, keep, L, long_sequence):
    B, S, D = x.shape
    sort_tokens = long_sequence
    states = []
    depth = mods.shape[0]
    final = (mods_f.reshape(B, 2, 1, D), wts[0]["g_final"])
    for l in range(depth):
        w = wts[l]
        m = mods[l].reshape(B, 6, 1, D)
        if long_sequence:
            qa, ka, va, qm, km, vm, om, gc, kt, vt = _in_proj(
                x, m, w["g1"], w["w_main"], w["w_gate"], w["b_gate"], ts, keep, w["w_kt"], L)
        else:
            spread = jnp.stack([jnp.broadcast_to(m[:, 0], (B, S, D)), jnp.broadcast_to(m[:, 1], (B, S, D))])
            outs = _in_proj(x.reshape(1, B * S, D), spread.reshape(2, 1, B * S, D), w["g1"], w["w_main"],
                            w["w_gate"], w["b_gate"], B * S, B * S)
            qa, ka, va, qm, km, vm, om, gc, kt, vt = (o.reshape(B, S, o.shape[-1]) for o in outs)
        att = attn_fn(l, qa, ka, va, kt, vt)
        c0, n0, m0 = state_fn(l, B)
        if long_sequence:
            n_cols = jnp.broadcast_to(jnp.swapaxes(n0, 2, 3), (B, N_HEADS_M, HD_M, HD_M))
            hm, ca, mm = _mlstm_long(qm, km, vm, om, gc, w["g_mlstm"], jnp.concatenate([c0, n_cols], axis=-1), m0, L)
            C, n = ca[..., :HD_M], ca[:, :, None, :, HD_M]
        else:
            hm, C, n, mm = _mlstm(qm, km, vm, om, gc, w["g_mlstm"], c0, n0, m0, L)
        if sort_tokens:
            x1, rows, idx, counts = _out_proj(att, hm, x, m, w["g2"], w["w_out"], w["w_router"], w["b_router"], ts,
                                              region_rows=B * S + MOE_TILE)
            x = _moe_sorted(rows, idx, counts, x1, m, w["w_gu4"], w["w_down4"], ts, MOE_TILE,
                            final if l == depth - 1 else None)
        else:
            x1, h2, gates = _out_proj(att, hm, x, m, w["g2"], w["w_out"], w["w_router"], w["b_router"], ts)
            flat = lambda t: t.reshape(1, B * S, t.shape[-1])
            gate = jnp.broadcast_to(m[:, 5], (B, S, D))
            x = _moe(flat(h2), flat(gates), flat(x1), flat(gate), w["w_gate_e"], w["w_up_e"], w["w_down_e"],
                     B * S).reshape(B, S, D)
        states.append((_heads(kt, N_HEADS_A), _heads(vt, N_HEADS_A), C, n[:, :, 0, :], mm[:, :, 0, 0]))
    if not sort_tokens:
        x = _final(x, *final, ts)
    return x, states


def kernel(x_prompt, x_sample, c_prompt, c_sample, cache_attn_k, cache_attn_v, state_mlstm_C, state_mlstm_n,
           state_mlstm_m, w_ada, b_ada, g_norm, w_in, b_igate, b_fgate, rel_bias, g_mlstm, w_out, w_router_g,
           b_router_g, w_router_e, b_router_e, w_gate, w_up, w_down, w_ada_f, b_ada_f, g_final):
    depth = w_ada.shape[0]
    Bp, Sp, D = x_prompt.shape
    Bs, Ss, _ = x_sample.shape

    c_all = jnp.concatenate([c_prompt, c_sample], axis=0)
    pad = (-c_all.shape[0]) % 16
    c_all = jnp.pad(c_all, ((0, pad), (0, 0)))
    mods_all = _ada(c_all, w_ada, b_ada, 1536)
    mods_fin = _ada(c_all, w_ada_f[None], b_ada_f[None], 1024)[0]

    def layer_weights(l, full_precision):
        wl = w_in[l]
        n_main = N_SLABS * SLAB
        w_gate_cols = jnp.pad(wl[:, n_main:], ((0, 0), (0, LANES - 2 * N_HEADS_M)))
        b_gate_cols = jnp.pad(jnp.concatenate([b_igate[l], b_fgate[l]]), (0, LANES - 2 * N_HEADS_M))
        w_r = jnp.concatenate([w_router_g[l], w_router_e[l]], axis=1)
        b_r = jnp.concatenate([b_router_g[l], b_router_e[l]])
        npad = LANES - N_GROUPS - N_EXPERTS
        w_r = jnp.pad(w_r, ((0, 0), (0, npad)))
        shared = dict(g1=g_norm[l, 0][None], g2=g_norm[l, 1][None], g_final=g_final[None],
                      b_gate=b_gate_cols[None], g_mlstm=g_mlstm[l][None], b_router=jnp.pad(b_r, (0, npad))[None])
        if full_precision:
            return dict(shared, w_main=wl, w_gate=w_gate_cols, w_out=w_out[l], w_router=w_r,
                        w_gate_e=w_gate[l], w_up_e=w_up[l], w_down_e=w_down[l])
        w_gu = jnp.concatenate([w_gate[l], w_up[l]], axis=-1).astype(BF16)
        return dict(shared, w_main=wl[:, :n_main].astype(BF16), w_gate=w_gate_cols.astype(BF16),
                    w_kt=wl[:, 4 * SLAB:5 * SLAB].T.astype(BF16), w_out=w_out[l].astype(BF16),
                    w_router=w_r.astype(BF16),
                    w_gu4=w_gu.reshape(N_GROUPS, EXP_PER_GROUP, D_MODEL, 2 * D_EXPERT),
                    w_down4=w_down[l].astype(BF16).reshape(N_GROUPS, EXP_PER_GROUP, D_EXPERT, D_MODEL))

    wts = [layer_weights(l, False) for l in range(depth)]
    wts_full = [layer_weights(l, True) for l in range(depth)]

    bias_tabs = [_band_bias_table(rel_bias[l]) for l in range(depth)]

    def attn_p(l, qa, ka, va, kt, vt):
        return _attn_prompt(qa, ka, va, bias_tabs[l])

    def state_p(l, B):
        return (jnp.zeros((B, N_HEADS_M, HD_M, HD_M), F32), jnp.zeros((B, N_HEADS_M, 1, HD_M), F32),
                jnp.zeros((B, N_HEADS_M, 1, HD_M), F32))

    keep_p = min(BAND_PAST, Sp)
    y_p, sp = _trunk(x_prompt, mods_all[:, :Bp], mods_fin[:Bp], wts, attn_p, state_p,
                     ts=512, keep=keep_p, L=512, long_sequence=True)

    Wc = cache_attn_k.shape[3]
    tq = jnp.arange(Ss)[:, None]
    dist_c = Wc + tq - jnp.arange(Wc)[None, :]
    dist_n = tq - jnp.arange(Ss)[None, :]
    lookup = lambda l, dist: rel_bias[l][:, jnp.clip(dist, -MAX_REL, MAX_REL) + MAX_REL].astype(F32)

    def attn_s(l, qa, ka, va, kt, vt):
        o = _attn_sample(_heads(qa, N_HEADS_A), _heads(kt, N_HEADS_A), _heads(vt, N_HEADS_A),
                         cache_attn_k[l], cache_attn_v[l], lookup(l, dist_c), lookup(l, dist_n))
        return o.transpose(0, 2, 1, 3).reshape(Bs, Ss, D_ATT)

    def state_s(l, B):
        return (state_mlstm_C[l], state_mlstm_n[l][:, :, None, :],
                jnp.broadcast_to(state_mlstm_m[l][:, :, None, None], (B, N_HEADS_M, 1, HD_M)))

    y_s, ss = _trunk(x_sample, mods_all[:, Bp:Bp + Bs], mods_fin[Bp:Bp + Bs], wts_full, attn_s, state_s,
                     ts=Ss, keep=Ss, L=Ss, long_sequence=False)

    stack = lambda sts, i: jnp.stack([s[i] for s in sts])
    return (y_p, y_s,
            stack(sp, 0), stack(sp, 1), stack(sp, 2), stack(sp, 3), stack(sp, 4),
            stack(ss, 0), stack(ss, 1), stack(ss, 2), stack(ss, 3), stack(ss, 4))
```

```python
import functools

import jax
import jax.numpy as jnp
import numpy as np
from jax import lax
from jax.experimental import pallas as pl
from jax.experimental.pallas import tpu as pltpu

F32 = jnp.float32
BF16 = jnp.bfloat16

D_MODEL = 1024
CHUNK = 64
BAND_CHUNKS = 8
BAND_PAST = BAND_CHUNKS * CHUNK
D_ATT = 512
N_HEADS_A = 8
HD_A = 64
MAX_REL = 128
D_MLSTM = 512
N_HEADS_M = 4
HD_M = 128
N_GROUPS = 4
EXP_PER_GROUP = 4
N_EXPERTS = 16
D_EXPERT = 256
EPS = 1e-6
NEG = -1e30

LANES = 128
SUBLANES = 8
N_SLABS = 7
SLAB = 512
ATTN_HEADS = 4
Q_BLOCK = 256
K_WINDOW = BAND_PAST + Q_BLOCK
ROUTER_E0 = N_GROUPS
PACKED = D_MODEL // 2
GATE_GROUP = PACKED // LANES
DMA_UNROLL = 8
MOE_TILE = 512
VMEM_LIMIT = 56 * 1024 * 1024


def _cparams(sem):
    return pltpu.CompilerParams(dimension_semantics=sem, vmem_limit_bytes=VMEM_LIMIT)


def _mm(a, b, dims=None):
    precision = lax.Precision.HIGHEST if a.dtype == F32 and b.dtype == F32 else None
    if dims is None:
        return jnp.dot(a, b, precision=precision, preferred_element_type=F32)
    return lax.dot_general(a, b, dims, precision=precision, preferred_element_type=F32)


def _ada_kernel(c_ref, w_ref, b_ref, o_ref):
    c = c_ref[...]
    o_ref[0] = _mm(c * jax.nn.sigmoid(c), w_ref[0]) + b_ref[0]


def _ada(c, w, b, tn):
    nl, d, n = w.shape
    bc = c.shape[0]
    return pl.pallas_call(
        _ada_kernel,
        out_shape=jax.ShapeDtypeStruct((nl, bc, n), F32),
        grid=(nl, n // tn),
        in_specs=[pl.BlockSpec((bc, d), lambda l, j: (0, 0)),
                  pl.BlockSpec((1, d, tn), lambda l, j: (l, 0, j)),
                  pl.BlockSpec((1, 1, tn), lambda l, j: (l, 0, j))],
        out_specs=pl.BlockSpec((1, bc, tn), lambda l, j: (l, 0, j)),
        compiler_params=_cparams(("parallel", "parallel")),
        name="ada",
    )(c, w, b.reshape(nl, 1, n))


def _norm_mod(x, g, scale, shift):
    r = lax.rsqrt(jnp.mean(x * x, axis=-1, keepdims=True) + EPS)
    return (x * r * g) * (1.0 + scale) + shift


def _inproj_kernel(x_ref, sh_ref, sc_ref, g_ref, w_ref, wg_ref, bg_ref, *rest, km_block):
    if km_block:
        wkt_ref, rest = rest[0], rest[1:]
    qa_ref, ka_ref, va_ref, qm_ref, km_ref, vm_ref, om_ref, gc_ref, kt_ref, vt_ref = rest
    hb = _norm_mod(x_ref[0], g_ref[...], sc_ref[...], sh_ref[...]).astype(w_ref.dtype)
    outs = (qa_ref, ka_ref, va_ref, qm_ref, km_ref, vm_ref, om_ref)
    for j, o_ref in enumerate(outs):
        cols = slice(j * SLAB, (j + 1) * SLAB)
        if j == 4 and km_block:
            t = lax.dot_general(wkt_ref[...], hb, (((1,), (1,)), ((), ())), preferred_element_type=F32)
            t = t * (HD_M ** -0.5)
            for c in range(hb.shape[0] // km_block):
                o_ref[0, c] = t[:, c * km_block:(c + 1) * km_block].astype(BF16)
            continue
        p = _mm(hb, w_ref[:, cols])
        if j == 4:
            p = p * (HD_M ** -0.5)
        o_ref[0] = p.astype(o_ref.dtype)
        if j in (1, 2):
            (kt_ref if j == 1 else vt_ref)[0] = p
    pg = _mm(hb, wg_ref[...]) + bg_ref[...]
    lsig = jnp.minimum(pg, 0.0) - jnp.log1p(jnp.exp(-jnp.abs(pg)))
    lane = lax.broadcasted_iota(jnp.int32, pg.shape, 1)
    gc_ref[0] = jnp.where(lane < N_HEADS_M, pg, jnp.where(lane < 2 * N_HEADS_M, lsig, 0.0))


def _in_proj(x, mods, g, w_main, w_gate, b_gate, ts, keep, w_kt=None, km_block=None):
    B, S, D = x.shape
    n_tiles = S // ts
    n_tail = keep // ts
    first_tail = n_tiles - n_tail
    slab = jax.ShapeDtypeStruct((B, S, SLAB), w_main.dtype)
    tail = jax.ShapeDtypeStruct((B, keep, SLAB), F32)
    slab_spec = pl.BlockSpec((1, ts, SLAB), lambda b, s: (b, s, 0))
    tail_spec = pl.BlockSpec((1, ts, SLAB), lambda b, s: (b, jnp.maximum(s - first_tail, 0), 0))
    slabs = [slab] * N_SLABS
    slab_specs = [slab_spec] * N_SLABS
    if mods.shape[2] == 1:
        mod_spec = lambda i: pl.BlockSpec((None, None, 1, D), lambda b, s: (b, i, 0, 0))
    else:
        mod_spec = lambda i: pl.BlockSpec((None, None, ts, D), lambda b, s: (i, b, s, 0))
    in_specs = [pl.BlockSpec((1, ts, D), lambda b, s: (b, s, 0)),
                mod_spec(0), mod_spec(1),
                pl.BlockSpec((1, D), lambda b, s: (0, 0)),
                pl.BlockSpec(w_main.shape, lambda b, s: (0, 0)),
                pl.BlockSpec((D, LANES), lambda b, s: (0, 0)),
                pl.BlockSpec((1, LANES), lambda b, s: (0, 0))]
    args = [x, mods, mods, g, w_main, w_gate, b_gate]
    if km_block:
        slabs[4] = jax.ShapeDtypeStruct((B, S // km_block, SLAB, km_block), BF16)
        slab_specs[4] = pl.BlockSpec((1, ts // km_block, SLAB, km_block), lambda b, s: (b, s, 0, 0))
        in_specs.append(pl.BlockSpec((SLAB, D), lambda b, s: (0, 0)))
        args.append(w_kt)
    return pl.pallas_call(
        functools.partial(_inproj_kernel, km_block=km_block),
        out_shape=tuple(slabs) + (jax.ShapeDtypeStruct((B, S, LANES), F32), tail, tail),
        grid=(B, n_tiles),
        in_specs=in_specs,
        out_specs=tuple(slab_specs) + (pl.BlockSpec((1, ts, LANES), lambda b, s: (b, s, 0)),
                                       tail_spec, tail_spec),
        compiler_params=_cparams(("parallel", "arbitrary")),
        name="in_proj",
    )(*args)


def _attn_prompt_kernel(q_ref, k_ref, v_ref, bias_ref, o_ref):
    i = pl.program_id(2)
    ws = pl.multiple_of(jnp.maximum(i * Q_BLOCK - BAND_PAST, 0), Q_BLOCK)
    variant = jnp.minimum(i, BAND_PAST // Q_BLOCK)
    q = q_ref[0]
    kw = k_ref[0, pl.ds(ws, K_WINDOW), :]
    vw = v_ref[0, pl.ds(ws, K_WINDOW), :]
    lane = lax.broadcasted_iota(jnp.int32, q.shape, 1)
    out = None
    for hh in range(ATTN_HEADS):
        in_head = (lane >= hh * HD_A) & (lane < (hh + 1) * HD_A)
        qh = jnp.where(in_head, q * (HD_A ** -0.5), 0.0).astype(BF16)
        s = lax.dot_general(qh, kw, (((1,), (1,)), ((), ())), preferred_element_type=F32)
        s = s + bias_ref[variant, hh]
        mx = jnp.max(s, axis=-1, keepdims=True)
        p = jnp.exp(s - mx)
        l = jnp.sum(p, axis=-1, keepdims=True)
        o = jnp.dot(p.astype(BF16), vw, preferred_element_type=F32) / l
        out = o if out is None else jnp.where(in_head, o, out)
    o_ref[0] = out.astype(BF16)


def _lookup_static(table, idx):
    n = table.shape[1]
    rev = table[:, ::-1]
    pieces = []
    start = 0
    while start < len(idx):
        first = int(idx[start])
        step = int(idx[start + 1]) - first if start + 1 < len(idx) else 0
        if step not in (-1, 0, 1):
            step = 0
        length = 1
        while start + length < len(idx) and int(idx[start + length]) - int(idx[start + length - 1]) == step:
            length += 1
        if step == 0:
            pieces.append(jnp.broadcast_to(table[:, first:first + 1], (table.shape[0], length)))
        elif step == 1:
            pieces.append(table[:, first:first + length])
        else:
            pieces.append(rev[:, n - 1 - first:n - 1 - first + length])
        start += length
    return jnp.concatenate(pieces, axis=1)


def _band_bias_table(table):
    H = table.shape[0]
    a = jnp.arange(Q_BLOCK)[:, None]
    j = jnp.arange(K_WINDOW)[None, :]
    period = Q_BLOCK + K_WINDOW
    u = np.arange(period)
    v = np.where(u < K_WINDOW, u, u - period)
    tabs = []
    for off in range(0, BAND_PAST + 1, Q_BLOCK):
        prof = _lookup_static(table, np.clip(off - v, -MAX_REL, MAX_REL) + MAX_REL).astype(F32)
        skew = jnp.tile(prof, (1, Q_BLOCK))[:, :Q_BLOCK * (period - 1)].reshape(H, Q_BLOCK, period - 1)
        bias = skew[:, :, :K_WINDOW]
        qc = (a + off) // CHUNK
        kc = j // CHUNK
        valid = (kc <= qc) & (kc >= qc - BAND_CHUNKS)
        tabs.append(jnp.where(valid[None], bias, NEG))
    return jnp.stack(tabs)


def _attn_prompt(q, k, v, bias_tab):
    B, S, _ = q.shape
    nv = bias_tab.shape[0]
    width = ATTN_HEADS * HD_A
    return pl.pallas_call(
        _attn_prompt_kernel,
        out_shape=jax.ShapeDtypeStruct((B, S, D_ATT), BF16),
        grid=(N_HEADS_A // ATTN_HEADS, B, S // Q_BLOCK),
        in_specs=[pl.BlockSpec((1, Q_BLOCK, width), lambda hp, b, i: (b, i, hp)),
                  pl.BlockSpec((1, S, width), lambda hp, b, i: (b, 0, hp)),
                  pl.BlockSpec((1, S, width), lambda hp, b, i: (b, 0, hp)),
                  pl.BlockSpec((nv, ATTN_HEADS, Q_BLOCK, K_WINDOW), lambda hp, b, i: (0, hp, 0, 0))],
        out_specs=pl.BlockSpec((1, Q_BLOCK, width), lambda hp, b, i: (b, i, hp)),
        compiler_params=_cparams(("parallel", "parallel", "arbitrary")),
        name="attn_prompt",
    )(q, k, v, bias_tab)


def _attn_sample_kernel(q_ref, kn_ref, vn_ref, kc_ref, vc_ref, bc_ref, bn_ref, o_ref):
    for h in range(N_HEADS_A):
        q = q_ref[0, h] * (HD_A ** -0.5)
        kn = kn_ref[0, h]
        vn = vn_ref[0, h]
        kc = kc_ref[0, h]
        vc = vc_ref[0, h]
        nt = (((1,), (1,)), ((), ()))
        sc = _mm(q, kc, nt) + bc_ref[h]
        sn = _mm(q, kn, nt) + bn_ref[h]
        mx = jnp.maximum(jnp.max(sc, axis=-1, keepdims=True), jnp.max(sn, axis=-1, keepdims=True))
        pc = jnp.exp(sc - mx)
        pn = jnp.exp(sn - mx)
        l = jnp.sum(pc, axis=-1, keepdims=True) + jnp.sum(pn, axis=-1, keepdims=True)
        o = _mm(pc, vc) + _mm(pn, vn)
        o_ref[0, h] = o / l


def _attn_sample(q, kn, vn, kc, vc, bias_c, bias_n):
    B, H, T, d = q.shape
    W = kc.shape[2]
    new = pl.BlockSpec((1, H, T, d), lambda b: (b, 0, 0, 0))
    old = pl.BlockSpec((1, H, W, d), lambda b: (b, 0, 0, 0))
    return pl.pallas_call(
        _attn_sample_kernel,
        out_shape=jax.ShapeDtypeStruct((B, H, T, d), F32),
        grid=(B,),
        in_specs=[new, new, new, old, old,
                  pl.BlockSpec((H, T, W), lambda b: (0, 0, 0)),
                  pl.BlockSpec((H, T, T), lambda b: (0, 0, 0))],
        out_specs=new,
        compiler_params=_cparams(("parallel",)),
        name="attn_sample",
    )(q, kn, vn, kc, vc, bias_c, bias_n)


def _mlstm_kernel(q_ref, k_ref, v_ref, om_ref, gc_ref, gml_ref, c0_ref, n0_ref, m0_ref,
                  h_ref, c_ref, n_ref, m_ref, *, L, nc):
    c_ref[0] = c0_ref[0]
    n_ref[0] = n0_ref[0]
    m_ref[0] = m0_ref[0]
    tt = lax.broadcasted_iota(jnp.int32, (L, L), 0)
    ss = lax.broadcasted_iota(jnp.int32, (L, L), 1)
    causal = ss <= tt
    eye = ss == tt
    nt = (((1,), (1,)), ((), ()))
    tn = (((0,), (0,)), ((), ()))

    def chunk(ci, carry):
        off = pl.multiple_of(ci * L, L)
        g = gc_ref[0, pl.ds(off, L), :]
        for hh in range(N_HEADS_M):
            cols = slice(hh * HD_M, (hh + 1) * HD_M)
            ig_c = g[:, hh:hh + 1]
            lf_c = g[:, N_HEADS_M + hh:N_HEADS_M + hh + 1]
            ig_r = jnp.sum(jnp.where(eye, ig_c, 0.0), axis=0, keepdims=True)
            lf_r = jnp.sum(jnp.where(eye, lf_c, 0.0), axis=0, keepdims=True)
            b_r = jnp.sum(jnp.where(tt <= ss, lf_c, 0.0), axis=0, keepdims=True)
            b_c = jnp.sum(jnp.where(causal, lf_r, 0.0), axis=1, keepdims=True)
            a_r = ig_r - b_r
            a_c = ig_c - b_c
            m_prev = m_ref[0, hh][:, 0:1]
            Mt = jnp.maximum(jnp.max(jnp.where(causal, a_r, -jnp.inf), axis=1, keepdims=True), m_prev)
            P = jnp.where(causal, jnp.exp(a_r - Mt), 0.0)
            decay = jnp.exp(m_prev - Mt)
            q = q_ref[0, pl.ds(off, L), cols]
            k = k_ref[0, pl.ds(off, L), cols]
            v = v_ref[0, pl.ds(off, L), cols]
            C = c_ref[0, hh]
            n = n_ref[0, hh]
            S = P * _mm(q, k, nt)
            num = decay * _mm(q, C.astype(q.dtype)) + _mm(S.astype(q.dtype), v)
            den = decay * jnp.sum(q.astype(F32) * n, axis=1, keepdims=True) + jnp.sum(S, axis=1, keepdims=True)
            hv = num / jnp.maximum(jnp.abs(den), jnp.exp(-(b_c + Mt)))
            hn = hv * lax.rsqrt(jnp.mean(hv * hv, axis=-1, keepdims=True) + EPS) * gml_ref[:, cols]
            om = om_ref[0, pl.ds(off, L), cols].astype(F32)
            h_ref[0, pl.ds(off, L), cols] = (hn * jax.nn.sigmoid(om)).astype(h_ref.dtype)
            M_last = Mt[L - 1:L, :]
            b_last = b_c[L - 1:L, :]
            kw = k.astype(F32) * jnp.exp(a_c - M_last)
            cd = jnp.exp(m_prev - M_last)
            c_ref[0, hh] = cd * C + _mm(kw.astype(q.dtype), v, tn)
            n_ref[0, hh] = cd * n + jnp.sum(kw, axis=0, keepdims=True)
            m_ref[0, hh] = jnp.broadcast_to(b_last + M_last, (1, HD_M))
        return carry

    lax.fori_loop(0, nc, chunk, 0)


def _mlstm(q, k, v, om, gc, gml, c0, n0, m0, L):
    B, S, _ = q.shape
    H = N_HEADS_M
    seq = pl.BlockSpec((1, S, D_MLSTM), lambda b: (b, 0, 0))
    cst = pl.BlockSpec((1, H, HD_M, HD_M), lambda b: (b, 0, 0, 0))
    vec = pl.BlockSpec((1, H, 1, HD_M), lambda b: (b, 0, 0, 0))
    return pl.pallas_call(
        functools.partial(_mlstm_kernel, L=L, nc=S // L),
        out_shape=(jax.ShapeDtypeStruct((B, S, D_MLSTM), q.dtype),
                   jax.ShapeDtypeStruct((B, H, HD_M, HD_M), F32),
                   jax.ShapeDtypeStruct((B, H, 1, HD_M), F32),
                   jax.ShapeDtypeStruct((B, H, 1, HD_M), F32)),
        grid=(B,),
        in_specs=[seq, seq, seq, seq,
                  pl.BlockSpec((1, S, LANES), lambda b: (b, 0, 0)),
                  pl.BlockSpec((1, D_MLSTM), lambda b: (0, 0)),
                  cst, vec, vec],
        out_specs=(seq, cst, vec, vec),
        compiler_params=_cparams(("parallel",)),
        name="mlstm",
    )(q, k, v, om, gc, gml, c0, n0, m0)


def _split3(x):
    hi = x.astype(BF16)
    r1 = x - hi.astype(F32)
    mid = r1.astype(BF16)
    lo = (r1 - mid.astype(F32)).astype(BF16)
    return hi, mid, lo


def _mlstm_long_kernel(q_ref, kt_ref, v_ref, om_ref, gc_ref, gml_ref, ca0_ref, m0_ref,
                       h_ref, ca_ref, m_ref, *, L, nc):
    ca_ref[0] = ca0_ref[0]
    m_ref[0] = m0_ref[0]
    tt = lax.broadcasted_iota(jnp.int32, (L, L), 0)
    ss = lax.broadcasted_iota(jnp.int32, (L, L), 1)
    causal = ss <= tt
    lower = jnp.where(causal, 1.0, 0.0).astype(BF16)
    upper = jnp.where(tt <= ss, 1.0, 0.0).astype(BF16)
    ones_blk = jnp.ones((L, HD_M), BF16)
    mean_mat = jnp.full((HD_M, HD_M), 1.0 / HD_M, BF16)
    lane = lax.broadcasted_iota(jnp.int32, (L, LANES), 1)
    H = N_HEADS_M

    def chunk(ci, carry):
        off = pl.multiple_of(ci * L, L)
        g = gc_ref[0, pl.ds(off, L), :]
        g_rows = g.T[:2 * H, :]
        lf_c = jnp.where(lane < H, pltpu.roll(g, LANES - H, axis=1), 0.0)
        bc3 = jnp.dot(lower, jnp.concatenate(_split3(lf_c), axis=1), preferred_element_type=F32)
        b_c = bc3[:, :LANES] + bc3[:, LANES:2 * LANES] + bc3[:, 2 * LANES:]
        terms = [t.astype(F32) for t in _split3(g_rows)] + [jnp.zeros((SUBLANES, L), F32)]
        rows3 = jnp.concatenate(terms, axis=0).astype(BF16)
        br3 = jnp.dot(rows3, upper, preferred_element_type=F32)
        b_r = br3[:2 * H] + br3[2 * H:4 * H] + br3[4 * H:6 * H]
        a_r4 = g_rows[:H] - b_r[H:]
        for hh in range(H):
            cols = slice(hh * HD_M, (hh + 1) * HD_M)
            m_prev = m_ref[0, hh][:, 0:1]
            a_r = a_r4[hh:hh + 1, :]
            Mt = jnp.maximum(jnp.max(jnp.where(causal, a_r, -jnp.inf), axis=1, keepdims=True), m_prev)
            Mb = jnp.broadcast_to(Mt, (L, LANES))
            P = jnp.where(causal, jnp.exp(a_r - jnp.concatenate([Mb] * (L // LANES), axis=1)), 0.0)
            q = q_ref[0, pl.ds(off, L), cols]
            kt = kt_ref[0, ci, cols, :]
            v = v_ref[0, pl.ds(off, L), cols]
            V = jnp.concatenate([v, ones_blk], axis=1)
            CA = ca_ref[0, hh]
            S = (P * jnp.dot(q, kt, preferred_element_type=F32)).astype(BF16)
            inter = jnp.dot(q, CA.astype(BF16), preferred_element_type=F32)
            intra = jnp.dot(S, V, preferred_element_type=F32)
            decay = jnp.exp(m_prev - Mb)
            num = decay * inter[:, :HD_M] + intra[:, :HD_M]
            den = decay * inter[:, HD_M:] + intra[:, HD_M:]
            b_cb = jnp.broadcast_to(b_c[:, hh:hh + 1], (L, LANES))
            hv = num / jnp.maximum(jnp.abs(den), jnp.exp(-(b_cb + Mb)))
            sq = hv * hv
            sq_hi = sq.astype(BF16)
            sq_lo = (sq - sq_hi.astype(F32)).astype(BF16)
            msq = (jnp.dot(sq_hi, mean_mat, preferred_element_type=F32)
                   + jnp.dot(sq_lo, mean_mat, preferred_element_type=F32))
            hn = hv * lax.rsqrt(msq + EPS) * gml_ref[:, cols]
            om = om_ref[0, pl.ds(off, L), cols].astype(F32)
            h_ref[0, pl.ds(off, L), cols] = (hn * jax.nn.sigmoid(om)).astype(BF16)
            M_last = jnp.maximum(jnp.max(a_r, axis=1, keepdims=True), m_prev)
            kw = (kt.astype(F32) * jnp.exp(a_r - M_last)).astype(BF16)
            cd = jnp.exp(m_prev - M_last)
            ca_ref[0, hh] = cd * CA + jnp.dot(kw, V, preferred_element_type=F32)
            m_ref[0, hh] = jnp.broadcast_to(b_r[H + hh:H + hh + 1, L - 1:L] + M_last, (1, HD_M))
        return carry

    lax.fori_loop(0, nc, chunk, 0)


def _mlstm_long(q, kt, v, om, gc, gml, ca0, m0, L):
    B, S, _ = q.shape
    H = N_HEADS_M
    seq = pl.BlockSpec((1, S, D_MLSTM), lambda b: (b, 0, 0))
    cst = pl.BlockSpec((1, H, HD_M, 2 * HD_M), lambda b: (b, 0, 0, 0))
    vec = pl.BlockSpec((1, H, 1, HD_M), lambda b: (b, 0, 0, 0))
    return pl.pallas_call(
        functools.partial(_mlstm_long_kernel, L=L, nc=S // L),
        out_shape=(jax.ShapeDtypeStruct((B, S, D_MLSTM), BF16),
                   jax.ShapeDtypeStruct((B, H, HD_M, 2 * HD_M), F32),
                   jax.ShapeDtypeStruct((B, H, 1, HD_M), F32)),
        grid=(B,),
        in_specs=[seq,
                  pl.BlockSpec((1, S // L, D_MLSTM, L), lambda b: (b, 0, 0, 0)),
                  seq, seq,
                  pl.BlockSpec((1, S, LANES), lambda b: (b, 0, 0)),
                  pl.BlockSpec((1, D_MLSTM), lambda b: (0, 0)),
                  cst, vec],
        out_specs=(seq, cst, vec),
        compiler_params=_cparams(("parallel",)),
        name="mlstm_long",
    )(q, kt, v, om, gc, gml, ca0, m0)


def _mix_residual_norm(att_ref, hm_ref, x_ref, gt_ref, sh_ref, sc_ref, g_ref, wo_ref):
    mix = _mm(att_ref[0], wo_ref[:D_ATT, :]) + _mm(hm_ref[0], wo_ref[D_ATT:, :])
    x1 = x_ref[0] + gt_ref[...] * mix
    return x1, _norm_mod(x1, g_ref[...], sc_ref[...], sh_ref[...]).astype(wo_ref.dtype)


def _router(hb, wr_ref, br_ref):
    logit = _mm(hb, wr_ref[...]) + br_ref[...]
    lane = lax.broadcasted_iota(jnp.int32, logit.shape, 1).astype(F32)
    big = float(LANES)
    is_g = lane < N_GROUPS
    mg = jnp.max(jnp.where(is_g, logit, -jnp.inf), axis=-1, keepdims=True)
    eg = jnp.where(is_g, jnp.exp(logit - mg), 0.0)
    pg = eg / jnp.sum(eg, axis=-1, keepdims=True)
    p_top = jnp.max(pg, axis=-1, keepdims=True)
    g_idx = jnp.min(jnp.where(is_g & (pg == p_top), lane, big), axis=-1, keepdims=True)
    e_lo = ROUTER_E0 + g_idx * EXP_PER_GROUP
    sel = (lane >= e_lo) & (lane < e_lo + EXP_PER_GROUP)
    me = jnp.max(jnp.where(sel, logit, -jnp.inf), axis=-1, keepdims=True)
    ee = jnp.where(sel, jnp.exp(logit - me), 0.0)
    pe = ee / jnp.sum(ee, axis=-1, keepdims=True)
    v1 = jnp.max(jnp.where(sel, pe, -1.0), axis=-1, keepdims=True)
    i1 = jnp.min(jnp.where(sel & (pe == v1), lane, big), axis=-1, keepdims=True)
    sel2 = sel & (lane != i1)
    v2 = jnp.max(jnp.where(sel2, pe, -1.0), axis=-1, keepdims=True)
    i2 = jnp.min(jnp.where(sel2 & (pe == v2), lane, big), axis=-1, keepdims=True)
    tot = v1 + v2
    w1 = v1 / tot * p_top
    w2 = v2 / tot * p_top
    return jnp.where(lane == i1, w1, jnp.where(lane == i2, w2, 0.0)), g_idx


def _outproj_kernel(att_ref, hm_ref, x_ref, gt_ref, sh_ref, sc_ref, g_ref, wo_ref, wr_ref, br_ref,
                    x1_ref, h2_ref, gates_ref):
    x1, hb = _mix_residual_norm(att_ref, hm_ref, x_ref, gt_ref, sh_ref, sc_ref, g_ref, wo_ref)
    x1_ref[0] = x1
    h2_ref[0] = hb
    gates_ref[0] = _router(hb, wr_ref, br_ref)[0]


def _pack_bf16_pairs(lo, hi):
    lo_bits = lax.bitcast_convert_type(lo.astype(BF16).astype(F32), jnp.uint32)
    hi_bits = lax.bitcast_convert_type(hi.astype(BF16).astype(F32), jnp.uint32)
    return (hi_bits & jnp.uint32(0xFFFF0000)) | (lo_bits >> 16)


def _unpack_bf16_pairs(packed):
    lo = lax.bitcast_convert_type(packed << 16, F32)
    hi = lax.bitcast_convert_type(packed & jnp.uint32(0xFFFF0000), F32)
    return lo, hi


def _outproj_sort_kernel(att_ref, hm_ref, x_ref, gt_ref, sh_ref, sc_ref, g_ref, wo_ref, wr_ref, br_ref,
                         x1_ref, row_ref, idx_ref, cnt_ref, fill_ref, *, region_rows):
    first = (pl.program_id(0) == 0) & (pl.program_id(1) == 0)

    @pl.when(first)
    def _():
        fill_ref[...] = jnp.zeros_like(fill_ref)

    x1, hb = _mix_residual_norm(att_ref, hm_ref, x_ref, gt_ref, sh_ref, sc_ref, g_ref, wo_ref)
    x1_ref[0] = x1
    gates, g_idx = _router(hb, wr_ref, br_ref)
    hf = hb.astype(F32)
    half = hf.shape[1] // 2
    ts = gates.shape[0]
    packed = _pack_bf16_pairs(hf[:, :half], hf[:, half:])
    for k in range(SUBLANES):
        if k < GATE_GROUP:
            piece = packed[:, k * LANES:(k + 1) * LANES]
        elif k == GATE_GROUP:
            piece = lax.bitcast_convert_type(gates, jnp.uint32)
        else:
            piece = jnp.zeros((ts, LANES), jnp.uint32)
        row_ref[_lane_group(k, ts), :] = piece
    lane = lax.broadcasted_iota(jnp.int32, gates.shape, 1)
    onehot = lane.astype(F32) == g_idx
    tt = lax.broadcasted_iota(jnp.int32, (ts, ts), 0)
    ss = lax.broadcasted_iota(jnp.int32, (ts, ts), 1)
    before = jnp.where(ss < tt, 1.0, 0.0).astype(BF16)
    rank = jnp.dot(before, jnp.where(onehot, 1.0, 0.0).astype(BF16), preferred_element_type=F32)
    fill = fill_ref[...]
    slot = jnp.sum(jnp.where(onehot, rank + fill, 0.0), axis=-1, keepdims=True)
    slot = slot + g_idx * float(region_rows)
    slot_row = jnp.sum(jnp.where(ss == tt, slot, 0.0), axis=0, keepdims=True)
    idx_ref[...] = slot_row.astype(jnp.int32)
    fill = fill + jnp.sum(jnp.where(onehot, 1.0, 0.0), axis=0, keepdims=True)
    fill_ref[...] = fill
    cnt_ref[...] = fill.astype(jnp.int32)


def _out_proj(att, hm, x, mods, g, w_out, w_router, b_router, ts, region_rows=None):
    B, S, D = x.shape
    nt = S // ts
    tile = lambda w: pl.BlockSpec((1, ts, w), lambda b, s: (b, s, 0))
    mod = lambda i: pl.BlockSpec((None, None, 1, D), lambda b, s: (b, i, 0, 0))
    const = lambda shape: pl.BlockSpec(shape, lambda b, s: (0, 0))
    in_specs = [tile(D_ATT), tile(D_MLSTM), tile(D), mod(2), mod(3), mod(4), const((1, D)),
                const((D, D)), const((D, LANES)), const((1, LANES))]
    args = (att, hm, x, mods, mods, mods, g, w_out, w_router, b_router)
    if region_rows is None:
        return pl.pallas_call(
            _outproj_kernel,
            out_shape=(jax.ShapeDtypeStruct((B, S, D), F32),
                       jax.ShapeDtypeStruct((B, S, D), w_out.dtype),
                       jax.ShapeDtypeStruct((B, S, LANES), F32)),
            grid=(B, nt), in_specs=in_specs,
            out_specs=(tile(D), tile(D), tile(LANES)),
            compiler_params=_cparams(("parallel", "parallel")),
            name="out_proj",
        )(*args)
    return pl.pallas_call(
        functools.partial(_outproj_sort_kernel, region_rows=region_rows),
        out_shape=(jax.ShapeDtypeStruct((B, S, D), F32),
                   jax.ShapeDtypeStruct((B, S * SUBLANES, LANES), jnp.uint32),
                   jax.ShapeDtypeStruct((B * nt, 1, ts), jnp.int32),
                   jax.ShapeDtypeStruct((1, LANES), jnp.int32)),
        grid=(B, nt), in_specs=in_specs,
        out_specs=(tile(D), pl.BlockSpec((None, ts * SUBLANES, LANES), lambda b, s: (b, s, 0)),
                   pl.BlockSpec((None, 1, ts), lambda b, s: (b * nt + s, 0, 0)),
                   const((1, LANES))),
        scratch_shapes=[pltpu.VMEM((1, LANES), F32)],
        compiler_params=_cparams(("arbitrary", "arbitrary")),
        name="out_proj_sort",
    )(*args)


def _moe_kernel(h_ref, gates_ref, x_ref, gt_ref, wg_ref, wu_ref, wd_ref, o_ref, acc_ref):
    e = pl.program_id(2)

    @pl.when(e == 0)
    def _():
        acc_ref[...] = jnp.zeros_like(acc_ref)

    h = h_ref[0]
    a = _mm(h, wg_ref[0])
    u = _mm(h, wu_ref[0])
    act = (a * jax.nn.sigmoid(a) * u).astype(h.dtype)
    y = _mm(act, wd_ref[0])
    gates = gates_ref[0]
    lane = lax.broadcasted_iota(jnp.int32, gates.shape, 1)
    ge = jnp.sum(jnp.where(lane == ROUTER_E0 + e, gates, 0.0), axis=-1, keepdims=True)
    acc_ref[...] += ge * y

    @pl.when(e == N_EXPERTS - 1)
    def _():
        o_ref[0] = x_ref[0] + gt_ref[0] * acc_ref[...]


def _moe(h2, gates, x1, mods, w_gate, w_up, w_down, ts):
    B, S, D = x1.shape
    tile = lambda w: pl.BlockSpec((1, ts, w), lambda b, s, e: (b, s, 0))
    return pl.pallas_call(
        _moe_kernel,
        out_shape=jax.ShapeDtypeStruct((B, S, D), F32),
        grid=(B, S // ts, N_EXPERTS),
        in_specs=[tile(D), tile(LANES), tile(D), tile(D),
                  pl.BlockSpec((1, D, D_EXPERT), lambda b, s, e: (e, 0, 0)),
                  pl.BlockSpec((1, D, D_EXPERT), lambda b, s, e: (e, 0, 0)),
                  pl.BlockSpec((1, D_EXPERT, D), lambda b, s, e: (e, 0, 0))],
        out_specs=tile(D),
        scratch_shapes=[pltpu.VMEM((ts, D), F32)],
        compiler_params=_cparams(("parallel", "parallel", "arbitrary")),
        name="moe",
    )(h2, gates, x1, mods, w_gate, w_up, w_down)


def _token_rows(token):
    return pl.ds(pl.multiple_of(token * SUBLANES, SUBLANES), SUBLANES)


def _lane_group(k, n, first_token=0):
    return pl.ds(first_token * SUBLANES + k, n, stride=SUBLANES)


def _for_each_token(ts, fn):
    def body(j, c):
        for u in range(DMA_UNROLL):
            fn(j * DMA_UNROLL + u, u % 2)
        return c
    lax.fori_loop(0, ts // DMA_UNROLL, body, 0)


def _dispatch_kernel(cnt_ref, row_ref, idx_ref, hs_ref, buf_ref, zero_ref, sem_ref, zsem_ref, *, region_rows, tm):
    i = pl.program_id(0)
    n = pl.num_programs(0)
    ts = idx_ref.shape[1]
    slot = i % 2

    def wait_slot(s):
        pltpu.make_async_copy(buf_ref.at[s], hs_ref.at[pl.ds(0, ts * SUBLANES), :], sem_ref.at[s]).wait()

    @pl.when(i >= 2)
    def _():
        wait_slot(slot)

    buf_ref[slot] = row_ref[...]

    def issue(r, priority):
        pltpu.make_async_copy(buf_ref.at[slot, _token_rows(r), :], hs_ref.at[_token_rows(idx_ref[0, r]), :],
                              sem_ref.at[slot]).start(priority=priority)
    _for_each_token(ts, issue)

    @pl.when(i == n - 1)
    def _():
        wait_slot(slot)

        @pl.when(n >= 2)
        def _():
            wait_slot(1 - slot)

        zero_ref[...] = jnp.zeros_like(zero_ref)

        def tail_copy(g):
            first = g * region_rows + cnt_ref[g]
            dst = hs_ref.at[pl.ds(pl.multiple_of(first * SUBLANES, SUBLANES), tm * SUBLANES), :]
            return pltpu.make_async_copy(zero_ref, dst, zsem_ref.at[g])

        for g in range(N_GROUPS):
            tail_copy(g).start()
        for g in range(N_GROUPS):
            tail_copy(g).wait()


def _dispatch(rows, idx, counts, ts, tm, region_rows):
    T = rows.shape[0] // SUBLANES
    grid_spec = pltpu.PrefetchScalarGridSpec(
        num_scalar_prefetch=1,
        grid=(T // ts,),
        in_specs=[pl.BlockSpec((ts * SUBLANES, LANES), lambda i, cnt: (i, 0)),
                  pl.BlockSpec((None, 1, ts), lambda i, cnt: (i, 0, 0), memory_space=pltpu.SMEM)],
        out_specs=pl.BlockSpec(memory_space=pl.ANY),
        scratch_shapes=[pltpu.VMEM((2, ts * SUBLANES, LANES), jnp.uint32),
                        pltpu.VMEM((tm * SUBLANES, LANES), jnp.uint32),
                        pltpu.SemaphoreType.DMA((2,)),
                        pltpu.SemaphoreType.DMA((N_GROUPS,))])
    return pl.pallas_call(
        functools.partial(_dispatch_kernel, region_rows=region_rows, tm=tm),
        out_shape=jax.ShapeDtypeStruct((N_GROUPS * region_rows * SUBLANES, LANES), jnp.uint32),
        grid_spec=grid_spec,
        compiler_params=_cparams(("arbitrary",)),
        name="moe_dispatch",
    )(counts, rows, idx)


def _moe_group_kernel(blk_ref, grp_ref, live_ref, hs_ref, wgu_ref, wd_ref, o_ref):
    i = pl.program_id(0)

    @pl.when(live_ref[i] == 0)
    def _():
        o_ref[...] = jnp.zeros_like(o_ref)

    @pl.when(live_ref[i] != 0)
    def _():
        tm = hs_ref.shape[0] // SUBLANES
        halves = [_unpack_bf16_pairs(hs_ref[_lane_group(k, tm), :]) for k in range(PACKED // LANES)]
        lo = jnp.concatenate([h[0] for h in halves], axis=1).astype(BF16)
        hi = jnp.concatenate([h[1] for h in halves], axis=1).astype(BF16)
        gates = lax.bitcast_convert_type(hs_ref[_lane_group(GATE_GROUP, tm), :], F32)
        lane = lax.broadcasted_iota(jnp.int32, gates.shape, 1)
        e0 = ROUTER_E0 + grp_ref[i] * EXP_PER_GROUP
        y = jnp.zeros((tm, D_MODEL), F32)
        for e in range(EXP_PER_GROUP):
            au = (jnp.dot(lo, wgu_ref[e, :PACKED, :], preferred_element_type=F32)
                  + jnp.dot(hi, wgu_ref[e, PACKED:, :], preferred_element_type=F32))
            a = au[:, :D_EXPERT]
            u = au[:, D_EXPERT:]
            act = (a * jax.nn.sigmoid(a) * u).astype(BF16)
            ge = jnp.sum(jnp.where(lane == e0 + e, gates, 0.0), axis=-1, keepdims=True)
            y = y + ge * jnp.dot(act, wd_ref[e], preferred_element_type=F32)
        for k in range(D_MODEL // LANES):
            o_ref[_lane_group(k, tm), :] = y[:, k * LANES:(k + 1) * LANES]


def _moe_group(hs, blk, grp, live, w_gu, w_down, tm):
    n_steps = blk.shape[0]
    rows = tm * SUBLANES
    n_blocks = hs.shape[0] // rows
    grid_spec = pltpu.PrefetchScalarGridSpec(
        num_scalar_prefetch=3,
        grid=(n_steps,),
        in_specs=[pl.BlockSpec((rows, LANES), lambda i, blk, grp, live: (blk[i], 0)),
                  pl.BlockSpec((None, EXP_PER_GROUP, D_MODEL, 2 * D_EXPERT), lambda i, blk, grp, live: (grp[i], 0, 0, 0)),
                  pl.BlockSpec((None, EXP_PER_GROUP, D_EXPERT, D_MODEL), lambda i, blk, grp, live: (grp[i], 0, 0, 0))],
        out_specs=pl.BlockSpec((rows, LANES),
                               lambda i, blk, grp, live: (jnp.where(live[i] != 0, blk[i], n_blocks), 0)))
    return pl.pallas_call(
        _moe_group_kernel,
        out_shape=jax.ShapeDtypeStruct(((n_blocks + 1) * rows, LANES), F32),
        grid_spec=grid_spec,
        compiler_params=_cparams(("arbitrary",)),
        name="moe_group",
    )(blk, grp, live, hs, w_gu, w_down)


def _combine_kernel(x1_ref, gt_ref, idx0_ref, idx1_ref, ys_ref, *rest, final):
    if final:
        shf_ref, scf_ref, gf_ref, o_ref, buf_ref, sem_ref = rest
    else:
        o_ref, buf_ref, sem_ref = rest
    i = pl.program_id(0)
    n = pl.num_programs(0)
    ts = idx0_ref.shape[1]
    slot = i % 2

    def issue(idx_ref, s):
        def one(r, priority):
            pltpu.make_async_copy(ys_ref.at[_token_rows(idx_ref[0, r]), :], buf_ref.at[s, _token_rows(r), :],
                                  sem_ref.at[s]).start(priority=priority)
        _for_each_token(ts, one)

    @pl.when(i == 0)
    def _():
        issue(idx0_ref, 0)

    @pl.when(i + 1 < n)
    def _():
        issue(idx1_ref, 1 - slot)

    pltpu.make_async_copy(ys_ref.at[pl.ds(0, ts * SUBLANES), :], buf_ref.at[slot], sem_ref.at[slot]).wait()

    g = gt_ref[...]
    for k in range(D_MODEL // LANES):
        cols = slice(k * LANES, (k + 1) * LANES)
        y = buf_ref[slot, _lane_group(k, ts), :]
        o_ref[:, cols] = x1_ref[:, cols] + g[:, cols] * y
    if final:
        o_ref[...] = _norm_mod(o_ref[...], gf_ref[...], scf_ref[...], shf_ref[...])


def _combine(x1, mods, idx, ys, ts, final=None):
    B, S, D = x1.shape
    nt = S // ts
    n = B * nt
    in_specs = [pl.BlockSpec((ts, D), lambda i: (i, 0)),
                pl.BlockSpec((None, None, 1, D), lambda i: (i // nt, 5, 0, 0)),
                pl.BlockSpec((None, 1, ts), lambda i: (i, 0, 0), memory_space=pltpu.SMEM),
                pl.BlockSpec((None, 1, ts), lambda i: (jnp.minimum(i + 1, n - 1), 0, 0), memory_space=pltpu.SMEM),
                pl.BlockSpec(memory_space=pl.ANY)]
    args = [x1.reshape(B * S, D), mods, idx, idx, ys]
    if final is not None:
        mods_f, g_final = final
        in_specs += [pl.BlockSpec((None, None, 1, D), lambda i: (i // nt, 0, 0, 0)),
                     pl.BlockSpec((None, None, 1, D), lambda i: (i // nt, 1, 0, 0)),
                     pl.BlockSpec((1, D), lambda i: (0, 0))]
        args += [mods_f, mods_f, g_final]
    out = pl.pallas_call(
        functools.partial(_combine_kernel, final=final is not None),
        out_shape=jax.ShapeDtypeStruct((B * S, D), F32),
        grid=(n,),
        in_specs=in_specs,
        out_specs=pl.BlockSpec((ts, D), lambda i: (i, 0)),
        scratch_shapes=[pltpu.VMEM((2, ts * SUBLANES, LANES), F32), pltpu.SemaphoreType.DMA((2,))],
        compiler_params=_cparams(("arbitrary",)),
        name="moe_combine",
    )(*args)
    return out.reshape(B, S, D)


def _tile_plan(counts, tm, n_steps, region_blocks):
    tiles = (counts + tm - 1) // tm
    ends = jnp.cumsum(tiles)
    total = ends[-1]
    step = jnp.minimum(jnp.arange(n_steps, dtype=jnp.int32), jnp.maximum(total - 1, 0))
    grp = jnp.sum((step[:, None] >= ends[None, :]).astype(jnp.int32), axis=1)
    grp = jnp.minimum(grp, N_GROUPS - 1)
    starts = ends - tiles
    blk = grp * region_blocks + (step - starts[grp])
    live = (jnp.arange(n_steps, dtype=jnp.int32) < total).astype(jnp.int32)
    return blk.astype(jnp.int32), grp.astype(jnp.int32), live


def _moe_sorted(rows, idx, counts, x1, mods, w_gu, w_down, ts, tm, final):
    B, S, D = x1.shape
    T = B * S
    region_rows = T + tm
    n_steps = T // tm + N_GROUPS
    cnt = counts[0, :N_GROUPS]
    hs = _dispatch(rows.reshape(T * SUBLANES, LANES), idx, cnt, ts, tm, region_rows)
    blk, grp, live = _tile_plan(cnt, tm, n_steps, region_rows // tm)
    ys = _moe_group(hs, blk, grp, live, w_gu, w_down, tm)
    return _combine(x1, mods, idx, ys, ts, final)


def _final_kernel(x_ref, sh_ref, sc_ref, g_ref, o_ref):
    o_ref[0] = _norm_mod(x_ref[0], g_ref[...], sc_ref[...], sh_ref[...])


def _final(x, mods_f, g, ts):
    B, S, D = x.shape
    return pl.pallas_call(
        _final_kernel,
        out_shape=jax.ShapeDtypeStruct((B, S, D), F32),
        grid=(B, S // ts),
        in_specs=[pl.BlockSpec((1, ts, D), lambda b, s: (b, s, 0)),
                  pl.BlockSpec((None, None, 1, D), lambda b, s: (b, 0, 0, 0)),
                  pl.BlockSpec((None, None, 1, D), lambda b, s: (b, 1, 0, 0)),
                  pl.BlockSpec((1, D), lambda b, s: (0, 0))],
        out_specs=pl.BlockSpec((1, ts, D), lambda b, s: (b, s, 0)),
        compiler_params=_cparams(("parallel", "parallel")),
        name="final_norm",
    )(x, mods_f, mods_f, g)


def _heads(t, n):
    B, S, _ = t.shape
    return t.reshape(B, S, n, -1).transpose(0, 2, 1, 3)


def _trunk(x, mods, mods_f, wts, attn_fn, state_fn, ts, keep, L, long_sequence):
    B, S, D = x.shape
    sort_tokens = long_sequence
    states = []
    depth = mods.shape[0]
    final = (mods_f.reshape(B, 2, 1, D), wts[0]["g_final"])
    for l in range(depth):
        w = wts[l]
        m = mods[l].reshape(B, 6, 1, D)
        if long_sequence:
            qa, ka, va, qm, km, vm, om, gc, kt, vt = _in_proj(
                x, m, w["g1"], w["w_main"], w["w_gate"], w["b_gate"], ts, keep, w["w_kt"], L)
        else:
            spread = jnp.stack([jnp.broadcast_to(m[:, 0], (B, S, D)), jnp.broadcast_to(m[:, 1], (B, S, D))])
            outs = _in_proj(x.reshape(1, B * S, D), spread.reshape(2, 1, B * S, D), w["g1"], w["w_main"],
                            w["w_gate"], w["b_gate"], B * S, B * S)
            qa, ka, va, qm, km, vm, om, gc, kt, vt = (o.reshape(B, S, o.shape[-1]) for o in outs)
        att = attn_fn(l, qa, ka, va, kt, vt)
        c0, n0, m0 = state_fn(l, B)
        if long_sequence:
            n_cols = jnp.broadcast_to(jnp.swapaxes(n0, 2, 3), (B, N_HEADS_M, HD_M, HD_M))
            hm, ca, mm = _mlstm_long(qm, km, vm, om, gc, w["g_mlstm"], jnp.concatenate([c0, n_cols], axis=-1), m0, L)
            C, n = ca[..., :HD_M], ca[:, :, None, :, HD_M]
        else:
            hm, C, n, mm = _mlstm(qm, km, vm, om, gc, w["g_mlstm"], c0, n0, m0, L)
        if sort_tokens:
            x1, rows, idx, counts = _out_proj(att, hm, x, m, w["g2"], w["w_out"], w["w_router"], w["b_router"], ts,
                                              region_rows=B * S + MOE_TILE)
            x = _moe_sorted(rows, idx, counts, x1, m, w["w_gu4"], w["w_down4"], ts, MOE_TILE,
                            final if l == depth - 1 else None)
        else:
            x1, h2, gates = _out_proj(att, hm, x, m, w["g2"], w["w_out"], w["w_router"], w["b_router"], ts)
            flat = lambda t: t.reshape(1, B * S, t.shape[-1])
            gate = jnp.broadcast_to(m[:, 5], (B, S, D))
            x = _moe(flat(h2), flat(gates), flat(x1), flat(gate), w["w_gate_e"], w["w_up_e"], w["w_down_e"],
                     B * S).reshape(B, S, D)
        states.append((kt, vt, C, n[:, :, 0, :], mm[:, :, 0, 0]))
    if not sort_tokens:
        x = _final(x, *final, ts)
    return x, states


def kernel(x_prompt, x_sample, c_prompt, c_sample, cache_attn_k, cache_attn_v, state_mlstm_C, state_mlstm_n,
           state_mlstm_m, w_ada, b_ada, g_norm, w_in, b_igate, b_fgate, rel_bias, g_mlstm, w_out, w_router_g,
           b_router_g, w_router_e, b_router_e, w_gate, w_up, w_down, w_ada_f, b_ada_f, g_final):
    depth = w_ada.shape[0]
    Bp, Sp, D = x_prompt.shape
    Bs, Ss, _ = x_sample.shape

    c_all = jnp.concatenate([c_prompt, c_sample], axis=0)
    pad = (-c_all.shape[0]) % 16
    c_all = jnp.pad(c_all, ((0, pad), (0, 0)))
    mods_all = _ada(c_all, w_ada, b_ada, 1536)
    mods_fin = _ada(c_all, w_ada_f[None], b_ada_f[None], 1024)[0]

    def layer_weights(l, full_precision):
        wl = w_in[l]
        n_main = N_SLABS * SLAB
        w_gate_cols = jnp.pad(wl[:, n_main:], ((0, 0), (0, LANES - 2 * N_HEADS_M)))
        b_gate_cols = jnp.pad(jnp.concatenate([b_igate[l], b_fgate[l]]), (0, LANES - 2 * N_HEADS_M))
        w_r = jnp.concatenate([w_router_g[l], w_router_e[l]], axis=1)
        b_r = jnp.concatenate([b_router_g[l], b_router_e[l]])
        npad = LANES - N_GROUPS - N_EXPERTS
        w_r = jnp.pad(w_r, ((0, 0), (0, npad)))
        shared = dict(g1=g_norm[l, 0][None], g2=g_norm[l, 1][None], g_final=g_final[None],
                      b_gate=b_gate_cols[None], g_mlstm=g_mlstm[l][None], b_router=jnp.pad(b_r, (0, npad))[None])
        if full_precision:
            return dict(shared, w_main=wl, w_gate=w_gate_cols, w_out=w_out[l], w_router=w_r,
                        w_gate_e=w_gate[l], w_up_e=w_up[l], w_down_e=w_down[l])
        w_gu = jnp.concatenate([w_gate[l], w_up[l]], axis=-1).astype(BF16)
        return dict(shared, w_main=wl[:, :n_main].astype(BF16), w_gate=w_gate_cols.astype(BF16),
                    w_kt=wl[:, 4 * SLAB:5 * SLAB].T.astype(BF16), w_out=w_out[l].astype(BF16),
                    w_router=w_r.astype(BF16),
                    w_gu4=w_gu.reshape(N_GROUPS, EXP_PER_GROUP, D_MODEL, 2 * D_EXPERT),
                    w_down4=w_down[l].astype(BF16).reshape(N_GROUPS, EXP_PER_GROUP, D_EXPERT, D_MODEL))

    wts = [layer_weights(l, False) for l in range(depth)]
    wts_full = [layer_weights(l, True) for l in range(depth)]

    bias_tabs = [_band_bias_table(rel_bias[l]) for l in range(depth)]

    def attn_p(l, qa, ka, va, kt, vt):
        return _attn_prompt(qa, ka, va, bias_tabs[l])

    def state_p(l, B):
        return (jnp.zeros((B, N_HEADS_M, HD_M, HD_M), F32), jnp.zeros((B, N_HEADS_M, 1, HD_M), F32),
                jnp.zeros((B, N_HEADS_M, 1, HD_M), F32))

    keep_p = min(BAND_PAST, Sp)
    y_p, sp = _trunk(x_prompt, mods_all[:, :Bp], mods_fin[:Bp], wts, attn_p, state_p,
                     ts=512, keep=keep_p, L=512, long_sequence=True)

    Wc = cache_attn_k.shape[3]
    tq = jnp.arange(Ss)[:, None]
    dist_c = Wc + tq - jnp.arange(Wc)[None, :]
    dist_n = tq - jnp.arange(Ss)[None, :]
    lookup = lambda l, dist: rel_bias[l][:, jnp.clip(dist, -MAX_REL, MAX_REL) + MAX_REL].astype(F32)

    def attn_s(l, qa, ka, va, kt, vt):
        o = _attn_sample(_heads(qa, N_HEADS_A), _heads(kt, N_HEADS_A), _heads(vt, N_HEADS_A),
                         cache_attn_k[l], cache_attn_v[l], lookup(l, dist_c), lookup(l, dist_n))
        return o.transpose(0, 2, 1, 3).reshape(Bs, Ss, D_ATT)

    def state_s(l, B):
        return (state_mlstm_C[l], state_mlstm_n[l][:, :, None, :],
                jnp.broadcast_to(state_mlstm_m[l][:, :, None, None], (B, N_HEADS_M, 1, HD_M)))

    y_s, ss = _trunk(x_sample, mods_all[:, Bp:Bp + Bs], mods_fin[Bp:Bp + Bs], wts_full, attn_s, state_s,
                     ts=Ss, keep=Ss, L=Ss, long_sequence=False)

    def stack(sts, i):
        t = jnp.stack([s[i] for s in sts])
        if i < 2:
            t = t.reshape(t.shape[:3] + (N_HEADS_A, HD_A)).transpose(0, 1, 3, 2, 4)
        return t
    return (y_p, y_s,
            stack(sp, 0), stack(sp, 1), stack(sp, 2), stack(sp, 3), stack(sp, 4),
            stack(ss, 0), stack(ss, 1), stack(ss, 2), stack(ss, 3), stack(ss, 4))
```
